```python
import math
import jax
import jax.numpy as jnp
from jax import lax
import numpy as np

D_MODEL = 1024
BATCH = 16
SEQ = 4096
DEPTH = 2

GRID_W = 64
CTX_LEN = 256
N_MOD = 6
EPS = 1e-6

MLA_HEADS = 8
MLA_NOPE = 64
MLA_ROPE = 32
MLA_V = 64
MLA_Q_LORA = 256
MLA_KV_LORA = 128
ROPE_BASE = 10000.0
Q_BLOCK = 128

CONV_CH = 256
CONV_K = 31

HY_CH = 256
HY_ORDER = 2
HY_SHORT_K = 3
HY_EMB = 33
HY_BANDS = 16
HY_HID = 64
HY_DECAY_SLOW = 3.07
HY_DECAY_FAST = 15.35

MLA_OUT = MLA_HEADS * MLA_V
MIX_WIDTH = MLA_OUT + CONV_CH + HY_CH
OFF_KV = MLA_Q_LORA
OFF_CONV = OFF_KV + MLA_KV_LORA + MLA_ROPE
OFF_HY = OFF_CONV + 2 * CONV_CH
IN_COLS = OFF_HY + (HY_ORDER + 1) * HY_CH

N_EXPERTS = 16
EXPERT_FF = 1024
CAPACITY_FACTOR = 2

kernel_name = 'hybrid_mla_conformer_hyena_ec_block'


def rmsnorm(x, g):
    xf = x.astype(jnp.float32)
    y = xf * lax.rsqrt(jnp.mean(xf * xf, axis=-1, keepdims=True) + EPS)
    return (y * g.astype(jnp.float32)).astype(x.dtype)


def layernorm(x, g, b):
    xf = x.astype(jnp.float32)
    mu = jnp.mean(xf, axis=-1, keepdims=True)
    var = jnp.mean(jnp.square(xf - mu), axis=-1, keepdims=True)
    y = (xf - mu) * lax.rsqrt(var + EPS)
    return (y * g.astype(jnp.float32) + b.astype(jnp.float32)).astype(x.dtype)


def modulate(h, shift, scale):
    return h * (1.0 + scale) + shift


def dwconv(x, w, b):
    ch = x.shape[-1]
    y = lax.conv_general_dilated(
        x, w[:, None, :].astype(x.dtype), window_strides=(1,), padding='SAME',
        dimension_numbers=('NWC', 'WIO', 'NWC'), feature_group_count=ch)
    return y + b.astype(x.dtype)


def axial_rope_tables(n_tok, dtype):
    rows = n_tok // GRID_W
    row = jnp.repeat(jnp.arange(rows, dtype=jnp.float32), GRID_W)
    col = jnp.tile(jnp.arange(GRID_W, dtype=jnp.float32), rows)
    half = MLA_ROPE // 2
    inv = ROPE_BASE ** (-jnp.arange(0, half, 2, dtype=jnp.float32) / half)
    ang = jnp.stack([row[:, None] * inv, col[:, None] * inv], axis=1)
    return jnp.cos(ang).astype(dtype), jnp.sin(ang).astype(dtype)


def apply_axial_rope(x, cos, sin):
    xs = x.reshape(x.shape[:-1] + (2, 2, MLA_ROPE // 4))
    x1, x2 = xs[..., 0, :], xs[..., 1, :]
    out = jnp.stack([x1 * cos - x2 * sin, x1 * sin + x2 * cos], axis=-2)
    return out.reshape(x.shape)


def mla_queries(u_q, p, cos, sin):
    bsz, n_tok, _ = u_q.shape
    q = (rmsnorm(u_q, p['q_a_g']) @ p['w_q_b']).reshape(bsz, n_tok, MLA_HEADS, MLA_NOPE + MLA_ROPE)
    q_nope, q_rope = q[..., :MLA_NOPE], q[..., MLA_NOPE:]
    if cos is not None:
        q_rope = apply_axial_rope(q_rope, cos[:, None], sin[:, None])
    return q_nope, q_rope


def mla_keys_values(u_kv, p, cos, sin):
    bsz, n_tok, _ = u_kv.shape
    kv = (rmsnorm(u_kv[..., :MLA_KV_LORA], p['kv_a_g']) @ p['w_kv_b']).reshape(
        bsz, n_tok, MLA_HEADS, MLA_NOPE + MLA_V)
    k_rope = u_kv[..., MLA_KV_LORA:]
    if cos is not None:
        k_rope = apply_axial_rope(k_rope, cos, sin)
    return kv[..., :MLA_NOPE], k_rope, kv[..., MLA_NOPE:]


def mla_attend(q_nope, q_rope, k_nope, k_rope, v):
    bsz, n_q, n_h, _ = q_nope.shape
    qb = min(Q_BLOCK, n_q)
    nb = n_q // qb
    scale = (MLA_NOPE + MLA_ROPE) ** -0.5

    def blocks(t):
        return jnp.moveaxis(t.reshape((bsz, nb, qb) + t.shape[2:]), 1, 0)

    def attend_block(args):
        qn, qr = args
        s = (jnp.einsum('bqhd,bkhd->bhqk', qn, k_nope, preferred_element_type=jnp.float32)
             + jnp.einsum('bqhr,bkr->bhqk', qr, k_rope, preferred_element_type=jnp.float32))
        w = jax.nn.softmax(s * scale, axis=-1).astype(v.dtype)
        return jnp.einsum('bhqk,bkhd->bqhd', w, v)

    o = lax.map(attend_block, (blocks(q_nope), blocks(q_rope)))
    return jnp.moveaxis(o, 0, 1).reshape(bsz, n_q, n_h * MLA_V)


def conformer_conv(u, p):
    a, g = u[..., :CONV_CH], u[..., CONV_CH:]
    y = a * jax.nn.sigmoid(g)
    y = dwconv(y, p['conv_dw_w'], p['conv_dw_b'])
    y = layernorm(y, p['conv_ln_g'], p['conv_ln_b'])
    return jax.nn.silu(y)


def hyena_filters(n_tok, p):
    f32 = jnp.float32
    t = jnp.linspace(0.0, 1.0, n_tok, dtype=f32)[:, None]
    w = 2.0 * math.pi * jnp.arange(n_tok, dtype=f32) / n_tok
    f = jnp.linspace(1e-4, HY_BANDS - 1, HY_BANDS, dtype=f32)
    fw = w[:, None] * f[None, :]
    z = jnp.concatenate([t, jnp.cos(fw), -jnp.sin(fw)], axis=-1)
    freq = p['hy_sin_freq'].astype(f32)
    h = jnp.sin(freq[0] * (z @ p['hy_w1'].astype(f32) + p['hy_b1'].astype(f32)))
    h = jnp.sin(freq[1] * (h @ p['hy_w2'].astype(f32) + p['hy_b2'].astype(f32)))
    h = (h @ p['hy_w3'].astype(f32)).reshape(n_tok, HY_ORDER, 2, HY_CH)
    decay = jnp.exp(-t * jnp.abs(p['hy_decay'].astype(f32)))
    h = h * decay[:, None, None, :]
    h_fwd, h_bwd = h[:, :, 0], h[:, :, 1]
    g = jnp.concatenate([h_fwd, jnp.zeros((1, HY_ORDER, HY_CH), f32), h_bwd[:0:-1]], axis=0)
    return g * lax.rsqrt(jnp.sum(g * g, axis=0, keepdims=True) + EPS)


def fft_long_conv(u, g_freq, skip):
    n_tok = u.shape[1]
    uf = u.astype(jnp.float32)
    spec = jnp.fft.rfft(uf, n=2 * n_tok, axis=1)
    y = jnp.fft.irfft(spec * g_freq[None], n=2 * n_tok, axis=1)[:, :n_tok]
    return (y + uf * skip.astype(jnp.float32)).astype(u.dtype)


def hyena(u, p):
    n_tok = u.shape[1]
    z = dwconv(u, p['hy_short_w'], p['hy_short_b'])
    v, x1, x2 = z[..., :HY_CH], z[..., HY_CH:2 * HY_CH], z[..., 2 * HY_CH:]
    g_freq = jnp.fft.rfft(hyena_filters(n_tok, p), n=2 * n_tok, axis=0)
    y = v
    for o, gate in enumerate((x1, x2)):
        y = gate * fft_long_conv(y, g_freq[:, o], p['hy_skip'][o])
    return y


def token_mixers(u, p, cos, sin, keys):
    q_nope, q_rope = mla_queries(u[..., :OFF_KV], p, cos, sin)
    att = mla_attend(q_nope, q_rope, *keys)
    conv = conformer_conv(u[..., OFF_CONV:OFF_HY], p)
    hy = hyena(u[..., OFF_HY:], p)
    gn = p['group_norm_g']
    y = jnp.concatenate([rmsnorm(att, gn[:MLA_OUT]),
                         rmsnorm(conv, gn[MLA_OUT:MLA_OUT + CONV_CH]),
                         rmsnorm(hy, gn[MLA_OUT + CONV_CH:])], axis=-1)
    return y @ p['w_out']


def expert_choice_ffn(h, p):
    bsz, n_tok, _ = h.shape
    cap = CAPACITY_FACTOR * n_tok // N_EXPERTS
    logits = jnp.einsum('bld,de->ble', h, p['router_w']).astype(jnp.float32)
    aff = jax.nn.softmax(logits, axis=-1)
    gates, idx = lax.top_k(jnp.swapaxes(aff, 1, 2), cap)
    bidx = jnp.arange(bsz)[:, None, None]
    xg = h[bidx, idx]
    a = jnp.einsum('becd,edf->becf', xg, p['w_gate'])
    b = jnp.einsum('becd,edf->becf', xg, p['w_up'])
    y = jnp.einsum('becf,efd->becd', jax.nn.silu(a) * b, p['w_down'])
    y = y * gates[..., None].astype(h.dtype)
    return jnp.zeros_like(h).at[bidx, idx].add(y)


def trunk_layer(xl, xc, c, c_ctx, p, cos, sin, last):
    mod_l = (jax.nn.silu(c) @ p['mod_w'] + p['mod_b'])[:, None, :]
    mod_c = jax.nn.silu(c_ctx) @ p['mod_w'] + p['mod_b']
    sh1_l, sc1_l, g1_l, sh2_l, sc2_l, g2_l = jnp.split(mod_l, N_MOD, axis=-1)
    sh1_c, sc1_c, g1_c, sh2_c, sc2_c, g2_c = jnp.split(mod_c, N_MOD, axis=-1)

    hc = modulate(rmsnorm(xc, p['norm1_g']), sh1_c, sc1_c)
    if last:
        uc_kv = hc @ p['w_in'][:, OFF_KV:OFF_CONV]
    else:
        uc = hc @ p['w_in']
        uc_kv = uc[..., OFF_KV:OFF_CONV]
    kv_c = mla_keys_values(uc_kv, p, None, None)

    hl = modulate(rmsnorm(xl, p['norm1_g']), sh1_l, sc1_l)
    ul = hl @ p['w_in']
    kv_l = mla_keys_values(ul[..., OFF_KV:OFF_CONV], p, cos, sin)
    keys_l = (jnp.concatenate([kv_l[0], kv_c[0]], axis=1),
              jnp.concatenate([kv_l[1], kv_c[1]], axis=1),
              jnp.concatenate([kv_l[2], kv_c[2]], axis=1))
    xl = xl + g1_l * token_mixers(ul, p, cos, sin, keys_l)
    hl2 = modulate(rmsnorm(xl, p['norm2_g']), sh2_l, sc2_l)
    xl = xl + g2_l * expert_choice_ffn(hl2, p)
    if last:
        return xl, None

    xc = xc + g1_c * token_mixers(uc, p, None, None, kv_c)
    hc2 = modulate(rmsnorm(xc, p['norm2_g']), sh2_c, sc2_c)
    xc = xc + g2_c * expert_choice_ffn(hc2, p)
    return xl, xc


def setup_inputs(seed: int = 0) -> dict:
    key = jax.random.key(seed)
    ks = iter(jax.random.split(key, 40))

    def nrm(shape, scale):
        return jax.random.normal(next(ks), shape, jnp.float32) * scale

    def gain(shape):
        return 1.0 + nrm(shape, 0.02)

    L = DEPTH
    hy_decay = jnp.linspace(HY_DECAY_SLOW, HY_DECAY_FAST, HY_CH, dtype=jnp.float32)[None, :] + nrm((L, HY_CH), 0.1)
    return {
        'x': nrm((BATCH, SEQ, D_MODEL), 1.0),
        'c': nrm((BATCH, D_MODEL), 1.0),
        'ctx': nrm((BATCH, CTX_LEN, D_MODEL), 1.0),
        'c_ctx': nrm((D_MODEL,), 1.0),
        'mod_w': nrm((L, D_MODEL, N_MOD * D_MODEL), 0.5 * D_MODEL ** -0.5),
        'mod_b': nrm((L, N_MOD * D_MODEL), 0.01),
        'norm1_g': gain((L, D_MODEL)),
        'w_in': nrm((L, D_MODEL, IN_COLS), D_MODEL ** -0.5),
        'q_a_g': gain((L, MLA_Q_LORA)),
        'w_q_b': nrm((L, MLA_Q_LORA, MLA_HEADS * (MLA_NOPE + MLA_ROPE)), MLA_Q_LORA ** -0.5),
        'kv_a_g': gain((L, MLA_KV_LORA)),
        'w_kv_b': nrm((L, MLA_KV_LORA, MLA_HEADS * (MLA_NOPE + MLA_V)), MLA_KV_LORA ** -0.5),
        'conv_dw_w': nrm((L, CONV_K, CONV_CH), CONV_K ** -0.5),
        'conv_dw_b': nrm((L, CONV_CH), 0.01),
        'conv_ln_g': gain((L, CONV_CH)),
        'conv_ln_b': nrm((L, CONV_CH), 0.01),
        'hy_short_w': nrm((L, HY_SHORT_K, (HY_ORDER + 1) * HY_CH), HY_SHORT_K ** -0.5),
        'hy_short_b': nrm((L, (HY_ORDER + 1) * HY_CH), 0.01),
        'hy_w1': nrm((L, HY_EMB, HY_HID), HY_EMB ** -0.5),
        'hy_b1': nrm((L, HY_HID), 0.1),
        'hy_w2': nrm((L, HY_HID, HY_HID), HY_HID ** -0.5),
        'hy_b2': nrm((L, HY_HID), 0.1),
        'hy_w3': nrm((L, HY_HID, HY_ORDER * 2 * HY_CH), HY_HID ** -0.5),
        'hy_sin_freq': 1.0 + nrm((L, 2, HY_HID), 0.1),
        'hy_decay': hy_decay,
        'hy_skip': nrm((L, HY_ORDER, HY_CH), 0.5),
        'group_norm_g': gain((L, MIX_WIDTH)),
        'w_out': nrm((L, MIX_WIDTH, D_MODEL), MIX_WIDTH ** -0.5),
        'norm2_g': gain((L, D_MODEL)),
        'router_w': nrm((L, D_MODEL, N_EXPERTS), D_MODEL ** -0.5),
        'w_gate': nrm((L, N_EXPERTS, D_MODEL, EXPERT_FF), D_MODEL ** -0.5),
        'w_up': nrm((L, N_EXPERTS, D_MODEL, EXPERT_FF), D_MODEL ** -0.5),
        'w_down': nrm((L, N_EXPERTS, EXPERT_FF, D_MODEL), EXPERT_FF ** -0.5),
        'final_norm_g': gain((D_MODEL,)),
    }


def reference(x, c, ctx, c_ctx, mod_w, mod_b, norm1_g, w_in, q_a_g, w_q_b, kv_a_g, w_kv_b,
              conv_dw_w, conv_dw_b, conv_ln_g, conv_ln_b, hy_short_w, hy_short_b,
              hy_w1, hy_b1, hy_w2, hy_b2, hy_w3, hy_sin_freq, hy_decay, hy_skip,
              group_norm_g, w_out, norm2_g, router_w, w_gate, w_up, w_down, final_norm_g):
    cos, sin = axial_rope_tables(x.shape[1], x.dtype)
    xl, xc = x, ctx
    for i in range(DEPTH):
        p = {
            'mod_w': mod_w[i], 'mod_b': mod_b[i], 'norm1_g': norm1_g[i], 'w_in': w_in[i],
            'q_a_g': q_a_g[i], 'w_q_b': w_q_b[i], 'kv_a_g': kv_a_g[i], 'w_kv_b': w_kv_b[i],
            'conv_dw_w': conv_dw_w[i], 'conv_dw_b': conv_dw_b[i],
            'conv_ln_g': conv_ln_g[i], 'conv_ln_b': conv_ln_b[i],
            'hy_short_w': hy_short_w[i], 'hy_short_b': hy_short_b[i],
            'hy_w1': hy_w1[i], 'hy_b1': hy_b1[i], 'hy_w2': hy_w2[i], 'hy_b2': hy_b2[i],
            'hy_w3': hy_w3[i], 'hy_sin_freq': hy_sin_freq[i], 'hy_decay': hy_decay[i],
            'hy_skip': hy_skip[i], 'group_norm_g': group_norm_g[i], 'w_out': w_out[i],
            'norm2_g': norm2_g[i], 'router_w': router_w[i],
            'w_gate': w_gate[i], 'w_up': w_up[i], 'w_down': w_down[i],
        }
        xl, xc = trunk_layer(xl, xc, c, c_ctx, p, cos, sin, i == DEPTH - 1)
    return rmsnorm(xl, final_norm_g)
```

```python
import functools
import math

import jax
import jax.numpy as jnp
from jax import lax
from jax.experimental import pallas as pl
from jax.experimental.pallas import tpu as pltpu

F32 = jnp.float32
BF16 = jnp.bfloat16
EPS = 1e-6

GRID_W = 64
N_MOD = 6
HEADS = 8
NOPE = 64
ROPE = 32
VDIM = 64
Q_LORA = 256
KV_LORA = 128
ROPE_BASE = 10000.0
CONV_CH = 256
CONV_K = 31
HY_CH = 256
HY_ORDER = 2
HY_BANDS = 16
N_EXPERTS = 16
CAPACITY_FACTOR = 2

LANE = 128
HEAD_PAD = LANE
HALO = 16
VMEM_LIMIT = 56 * 1024 * 1024

C_Q = 0
C_KVN = C_Q + Q_LORA
C_KR = C_KVN + KV_LORA
C_CONV = C_KR + LANE
C_HY = C_CONV + 2 * CONV_CH
C_END = C_HY + 3 * HY_CH


def _cparams(sem):
    return pltpu.CompilerParams(dimension_semantics=sem, vmem_limit_bytes=VMEM_LIMIT)


def _rms(x, g):
    return x * lax.rsqrt(jnp.mean(x * x, axis=-1, keepdims=True) + EPS) * g


def _split(a):
    hi = a.astype(BF16)
    lo = (a - hi.astype(F32)).astype(BF16)
    return hi, lo


def _dot(a, b):
    return jnp.dot(a, b, preferred_element_type=F32)


def _mod_kernel(c_ref, w_ref, b_ref, o_ref):
    c = c_ref[...]
    a = c * jax.nn.sigmoid(c)
    a_hi, a_lo = _split(a)
    w_hi, w_lo = _split(w_ref[0])
    o_ref[0] = _dot(a_hi, w_hi) + _dot(a_lo, w_hi) + _dot(a_hi, w_lo) + b_ref[0]


def _modulation(cc, mod_w, mod_b):
    depth, d, n = mod_w.shape
    rows = cc.shape[0]
    tn = 1536
    return pl.pallas_call(
        _mod_kernel,
        grid=(depth, n // tn),
        in_specs=[
            pl.BlockSpec((rows, d), lambda l, j: (0, 0)),
            pl.BlockSpec((1, d, tn), lambda l, j: (l, 0, j)),
            pl.BlockSpec((1, 1, tn), lambda l, j: (l, 0, j)),
        ],
        out_specs=pl.BlockSpec((1, rows, tn), lambda l, j: (l, 0, j)),
        out_shape=jax.ShapeDtypeStruct((depth, rows, n), F32),
        compiler_params=_cparams(("arbitrary", "arbitrary")),
        name="modulation",
    )(cc, mod_w, mod_b.reshape(depth, 1, n))


def _rope(x, c, s1, s2):
    return x * c + pltpu.roll(x, LANE - 8, 1) * s1 + pltpu.roll(x, 8, 1) * s2


def _inproj_kernel(*refs, fuse_prev, scale):
    if fuse_prev:
        xa_ref, xb_ref, g2_ref = refs[:3]
        refs = refs[3:]
    else:
        xa_ref = refs[0]
        refs = refs[1:]
    (sh_ref, sc_ref, n1g_ref, win_ref, qag_ref, wq_ref, kvg_ref, wk_ref, wv_ref,
     c_ref, s1_ref, s2_ref, vone_ref) = refs[:13]
    outs = refs[13:]
    if fuse_prev:
        x_out, q_out, k_out, v_out, uc_out, uh_out = outs
        x = xa_ref[0] + g2_ref[...] * xb_ref[0]
        x_out[0] = x
    else:
        q_out, k_out, v_out, uc_out, uh_out = outs
        x = xa_ref[0]
    hn = _rms(x, n1g_ref[...]) * (1.0 + sc_ref[...]) + sh_ref[...]
    u = _dot(hn.astype(BF16), win_ref[...])
    c, s1, s2 = c_ref[...], s1_ref[...], s2_ref[...]

    nq = _rms(u[:, C_Q:C_KVN], qag_ref[...]).astype(BF16)
    q = _dot(nq, wq_ref[...])
    for h in range(HEADS):
        sl = slice(h * HEAD_PAD, (h + 1) * HEAD_PAD)
        q_out[0, :, sl] = (_rope(q[:, sl], c, s1, s2) * scale).astype(BF16)

    kvn = _rms(u[:, C_KVN:C_KR], kvg_ref[...]).astype(BF16)
    kr = _rope(u[:, C_KR:C_CONV], c, s1, s2)
    k = _dot(kvn, wk_ref[...])
    for h in range(HEADS):
        sl = slice(h * HEAD_PAD, (h + 1) * HEAD_PAD)
        k_out[0, :, sl] = (k[:, sl] + kr).astype(BF16)
    v_out[0] = (_dot(kvn, wv_ref[...]) + vone_ref[...]).astype(BF16)
    uc_out[0] = u[:, C_CONV:C_HY]
    uh_out[0] = u[:, C_HY:C_END]


def _inproj(x, prev, mod, row_of_b, lw, tabs, tm):
    bsz, n_tok, d = x.shape
    fuse_prev = prev is not None
    grid = (bsz, n_tok // tm)
    tok = lambda w: pl.BlockSpec((1, tm, w), lambda b, i: (b, i, 0))
    modspec = lambda k: pl.BlockSpec((None, 1, d), lambda b, i: (row_of_b(b), 0, k))
    full = lambda a: pl.BlockSpec(a.shape, lambda b, i: (0,) * a.ndim)
    tabspec = pl.BlockSpec((tm, LANE), lambda b, i: (i, 0))

    args, specs = [x], [tok(d)]
    if fuse_prev:
        args += [prev[0], prev[1]]
        specs += [tok(d), modspec(5)]
    args += [mod, mod, lw['norm1_g'], lw['w_in'], lw['q_a_g'], lw['w_q'], lw['kv_a_g'],
             lw['w_k'], lw['w_v'], tabs[0], tabs[1], tabs[2], lw['v_one']]
    specs += [modspec(0), modspec(1), full(lw['norm1_g']), full(lw['w_in']), full(lw['q_a_g']),
              full(lw['w_q']), full(lw['kv_a_g']), full(lw['w_k']), full(lw['w_v']),
              tabspec, tabspec, tabspec, full(lw['v_one'])]
    hp = HEADS * HEAD_PAD
    out_shape, out_specs = [], []
    if fuse_prev:
        out_shape.append(jax.ShapeDtypeStruct((bsz, n_tok, d), F32))
        out_specs.append(tok(d))
    out_shape += [jax.ShapeDtypeStruct((bsz, n_tok, hp), BF16)] * 3
    out_specs += [tok(hp)] * 3
    out_shape += [jax.ShapeDtypeStruct((bsz, n_tok, 2 * CONV_CH), F32),
                  jax.ShapeDtypeStruct((bsz, n_tok, 3 * HY_CH), F32)]
    out_specs += [tok(2 * CONV_CH), tok(3 * HY_CH)]
    scale = float((NOPE + ROPE) ** -0.5)
    res = pl.pallas_call(
        functools.partial(_inproj_kernel, fuse_prev=fuse_prev, scale=scale),
        grid=grid, in_specs=specs, out_specs=out_specs, out_shape=out_shape,
        compiler_params=_cparams(("parallel", "parallel")),
        name="inproj",
    )(*args)
    if not fuse_prev:
        res = [x] + list(res)
    return res


def _attn_kernel(*refs, n_sets):
    q_ref = refs[0]
    kv = refs[1:1 + 2 * n_sets]
    o_ref = refs[1 + 2 * n_sets]
    nt = (((1,), (1,)), ((), ()))
    outs = []
    for hh in range(2):
        sl = slice(hh * HEAD_PAD, (hh + 1) * HEAD_PAD)
        q = q_ref[0, :, sl]
        s = [lax.dot_general(q, kv[2 * i][0, :, sl], nt, preferred_element_type=F32)
             for i in range(n_sets)]
        m = jnp.max(s[0], axis=1, keepdims=True)
        for i in range(1, n_sets):
            m = jnp.maximum(m, jnp.max(s[i], axis=1, keepdims=True))
        o = None
        for i in range(n_sets):
            p = jnp.exp(s[i] - m).astype(BF16)
            t = _dot(p, kv[2 * i + 1][0, :, sl])
            o = t if o is None else o + t
        outs.append(o * (1.0 / o[:, VDIM:VDIM + 1]))
    lane = lax.broadcasted_iota(jnp.int32, outs[0].shape, 1)
    o_ref[0] = jnp.where(lane < VDIM, outs[0], pltpu.roll(outs[1], VDIM, 1))


def _attention(q, key_sets, tq):
    bsz, n_q, hp = q.shape
    grid = (bsz, HEADS // 2, n_q // tq)
    args = [q]
    specs = [pl.BlockSpec((1, tq, 2 * HEAD_PAD), lambda b, h, i: (b, i, h))]
    for k, v in key_sets:
        n_k = k.shape[1]
        args += [k, v]
        specs += [pl.BlockSpec((1, n_k, 2 * HEAD_PAD), lambda b, h, i: (b, 0, h))] * 2
    return pl.pallas_call(
        functools.partial(_attn_kernel, n_sets=len(key_sets)),
        grid=grid, in_specs=specs,
        out_specs=pl.BlockSpec((1, tq, 2 * VDIM), lambda b, h, i: (b, i, h)),
        out_shape=jax.ShapeDtypeStruct((bsz, n_q, HEADS * VDIM), F32),
        compiler_params=_cparams(("parallel", "parallel", "arbitrary")),
        name="attention",
    )(*args)


def _conv_kernel(ucp_ref, uc_ref, ucn_ref, uhp_ref, uh_ref, uhn_ref,
                 cw_ref, cb_ref, lg_ref, lb_ref, gn_ref, hw_ref, hb_ref,
                 cn_out, z_out, ypad, hpad, *, tl, row_tile):
    i = pl.program_id(1)
    has_prev = (i > 0).astype(F32)
    has_next = (i < pl.num_programs(1) - 1).astype(F32)

    def glu(u):
        return u[:, :CONV_CH] * jax.nn.sigmoid(u[:, CONV_CH:])

    ypad[0:HALO, :] = glu(ucp_ref[0]) * has_prev
    ypad[HALO:HALO + tl, :] = glu(uc_ref[0])
    ypad[HALO + tl:, :] = glu(ucn_ref[0]) * has_next
    hpad[0:HALO, :] = uhp_ref[0] * has_prev
    hpad[HALO:HALO + tl, :] = uh_ref[0]
    hpad[HALO + tl:, :] = uhn_ref[0] * has_next

    half = CONV_K // 2
    for r in range(tl // row_tile):
        base = r * row_tile
        acc = jnp.zeros((row_tile, CONV_CH), F32) + cb_ref[...]
        for k in range(CONV_K):
            off = HALO + base + k - half
            acc = acc + cw_ref[k:k + 1, :] * ypad[off:off + row_tile, :]
        mu = jnp.mean(acc, axis=-1, keepdims=True)
        cen = acc - mu
        var = jnp.mean(cen * cen, axis=-1, keepdims=True)
        y = cen * lax.rsqrt(var + EPS) * lg_ref[...] + lb_ref[...]
        y = y * jax.nn.sigmoid(y)
        cn_out[0, base:base + row_tile, :] = _rms(y, gn_ref[...])

        z = jnp.zeros((row_tile, 3 * HY_CH), F32) + hb_ref[...]
        for k in range(3):
            off = HALO + base + k - 1
            z = z + hw_ref[k:k + 1, :] * hpad[off:off + row_tile, :]
        z_out[0, base:base + row_tile, :] = z


def _conv(uc, uh, lw, tl):
    bsz, n_tok, _ = uc.shape
    nh = tl // HALO
    last = n_tok // HALO - 1
    cur = lambda w: pl.BlockSpec((1, tl, w), lambda b, i: (b, i, 0))
    prv = lambda w: pl.BlockSpec((1, HALO, w), lambda b, i: (b, jnp.maximum(i * nh - 1, 0), 0))
    nxt = lambda w: pl.BlockSpec((1, HALO, w), lambda b, i: (b, jnp.minimum((i + 1) * nh, last), 0))
    full = lambda a: pl.BlockSpec(a.shape, lambda b, i: (0,) * a.ndim)
    wts = [lw['conv_dw_w'], lw['conv_dw_b'], lw['conv_ln_g'], lw['conv_ln_b'], lw['gn_conv'],
           lw['hy_short_w'], lw['hy_short_b']]
    return pl.pallas_call(
        functools.partial(_conv_kernel, tl=tl, row_tile=min(64, tl)),
        grid=(bsz, n_tok // tl),
        in_specs=[prv(2 * CONV_CH), cur(2 * CONV_CH), nxt(2 * CONV_CH),
                  prv(3 * HY_CH), cur(3 * HY_CH), nxt(3 * HY_CH)] + [full(w) for w in wts],
        out_specs=[cur(CONV_CH), cur(3 * HY_CH)],
        out_shape=[jax.ShapeDtypeStruct((bsz, n_tok, CONV_CH), F32),
                   jax.ShapeDtypeStruct((bsz, n_tok, 3 * HY_CH), F32)],
        scratch_shapes=[pltpu.VMEM((tl + 2 * HALO, CONV_CH), F32),
                        pltpu.VMEM((tl + 2 * HALO, 3 * HY_CH), F32)],
        compiler_params=_cparams(("parallel", "parallel")),
        name="conv",
    )(uc, uc, uc, uh, uh, uh, *wts)


def _outproj_kernel(att_ref, cn_ref, hy_ref, x_ref, g1_ref, sh_ref, sc_ref, gna_ref, gnh_ref,
                    wo_ref, n2g_ref, rw_ref, x1_out, h2p_out, aff_out):
    a = _rms(att_ref[0], gna_ref[...]).astype(BF16)
    c = cn_ref[0].astype(BF16)
    h = _rms(hy_ref[0], gnh_ref[...]).astype(BF16)
    na, nc = a.shape[1], c.shape[1]
    y = (_dot(a, wo_ref[0:na, :]) + _dot(c, wo_ref[na:na + nc, :]) + _dot(h, wo_ref[na + nc:, :]))
    x1 = x_ref[0] + g1_ref[...] * y
    x1_out[0] = x1
    h2 = _rms(x1, n2g_ref[...]) * (1.0 + sc_ref[...]) + sh_ref[...]
    d = h2.shape[1]
    lo = lax.bitcast_convert_type(h2[:, :d // 2].astype(BF16).astype(F32), jnp.uint32)
    hi = lax.bitcast_convert_type(h2[:, d // 2:].astype(BF16).astype(F32), jnp.uint32)
    h2p_out[0] = hi | (lo >> 16)
    h_hi, h_lo = _split(h2)
    t = _dot(h_hi, rw_ref[...])
    logits = t[:, :LANE] + t[:, LANE:] + _dot(h_lo, rw_ref[:, :LANE])
    lane = lax.broadcasted_iota(jnp.int32, logits.shape, 1)
    logits = jnp.where(lane < N_EXPERTS, logits, -1e30)
    e = jnp.exp(logits - jnp.max(logits, axis=1, keepdims=True))
    aff = e / jnp.sum(e, axis=1, keepdims=True)
    aff_out[0] = aff.T[:N_EXPERTS, :]


def _outproj(att, cn, hy, x, mod, row_of_b, lw, tm):
    bsz, n_tok, d = x.shape
    tok = lambda w: pl.BlockSpec((1, tm, w), lambda b, i: (b, i, 0))
    modspec = lambda k: pl.BlockSpec((None, 1, d), lambda b, i: (row_of_b(b), 0, k))
    full = lambda a: pl.BlockSpec(a.shape, lambda b, i: (0,) * a.ndim)
    wts = [lw['gn_att'], lw['gn_hy'], lw['w_out'], lw['norm2_g'], lw['router_w']]
    return pl.pallas_call(
        _outproj_kernel,
        grid=(bsz, n_tok // tm),
        in_specs=[tok(att.shape[2]), tok(cn.shape[2]), tok(hy.shape[2]), tok(d),
                  modspec(2), modspec(3), modspec(4)] + [full(w) for w in wts],
        out_specs=[tok(d), tok(d // 2), pl.BlockSpec((1, N_EXPERTS, tm), lambda b, i: (b, 0, i))],
        out_shape=[jax.ShapeDtypeStruct((bsz, n_tok, d), F32),
                   jax.ShapeDtypeStruct((bsz, n_tok, d // 2), jnp.uint32),
                   jax.ShapeDtypeStruct((bsz, N_EXPERTS, n_tok), F32)],
        compiler_params=_cparams(("parallel", "parallel")),
        name="outproj",
    )(att, cn, hy, x, mod, mod, mod, *wts)


def _moe_kernel(idx_ref, gate_ref, h2p_ref, wg_ref, wu_ref, wd_ref, out_hbm,
                acc_ref, xg_ref, y_ref, sem, *, cap):
    b = pl.program_id(0)
    e = pl.program_id(1)

    @pl.when(e == 0)
    def _():
        acc_ref[...] = jnp.zeros_like(acc_ref)

    def gather(j, carry):
        t = idx_ref[0, 0, j]
        xg_ref[pl.ds(j, 1), :] = h2p_ref[0, pl.ds(t, 1), :]
        return carry
    lax.fori_loop(0, cap, gather, 0, unroll=8)

    w = xg_ref[...]
    half = w.shape[1]
    x_lo = lax.bitcast_convert_type(w << 16, F32).astype(BF16)
    x_hi = lax.bitcast_convert_type(w & jnp.uint32(0xFFFF0000), F32).astype(BF16)
    a = _dot(x_lo, wg_ref[0, :half, :]) + _dot(x_hi, wg_ref[0, half:, :])
    u = _dot(x_lo, wu_ref[0, :half, :]) + _dot(x_hi, wu_ref[0, half:, :])
    hmid = (a * jax.nn.sigmoid(a) * u).astype(BF16)
    y_ref[...] = _dot(hmid, wd_ref[0])

    def scatter(j, carry):
        t = idx_ref[0, 0, j]
        g = gate_ref[0, 0, j]
        acc_ref[pl.ds(t, 1), :] = acc_ref[pl.ds(t, 1), :] + g * y_ref[pl.ds(j, 1), :]
        return carry
    lax.fori_loop(0, cap, scatter, 0, unroll=8)

    @pl.when(e == pl.num_programs(1) - 1)
    def _():
        cp = pltpu.make_async_copy(acc_ref, out_hbm.at[b], sem)
        cp.start()
        cp.wait()


def _moe(h2p, idx, gates, lw):
    bsz, n_tok, half = h2p.shape
    d = 2 * half
    n_e, cap = idx.shape[1], idx.shape[2]
    ff = lw['w_gate'].shape[2]
    smem = lambda: pl.BlockSpec((1, 1, cap), lambda b, e: (b * n_e + e, 0, 0),
                                memory_space=pltpu.SMEM)
    return pl.pallas_call(
        functools.partial(_moe_kernel, cap=cap),
        grid=(bsz, n_e),
        in_specs=[smem(), smem(),
                  pl.BlockSpec((1, n_tok, half), lambda b, e: (b, 0, 0)),
                  pl.BlockSpec((1, d, ff), lambda b, e: (e, 0, 0)),
                  pl.BlockSpec((1, d, ff), lambda b, e: (e, 0, 0)),
                  pl.BlockSpec((1, ff, d), lambda b, e: (e, 0, 0))],
        out_specs=pl.BlockSpec(memory_space=pl.ANY),
        out_shape=jax.ShapeDtypeStruct((bsz, n_tok, d), F32),
        scratch_shapes=[pltpu.VMEM((n_tok, d), F32),
                        pltpu.VMEM((cap, half), jnp.uint32),
                        pltpu.VMEM((cap, d), F32),
                        pltpu.SemaphoreType.DMA(())],
        compiler_params=_cparams(("arbitrary", "arbitrary")),
        name="moe",
    )(idx.reshape(bsz * n_e, 1, cap), gates.reshape(bsz * n_e, 1, cap), h2p,
      lw['w_gate'], lw['w_up'], lw['w_down'])


def _final_kernel(x_ref, m_ref, g2_ref, fg_ref, o_ref):
    o_ref[0] = _rms(x_ref[0] + g2_ref[...] * m_ref[0], fg_ref[...])


def _final(x1, moe, mod, fg, tm):
    bsz, n_tok, d = x1.shape
    tok = pl.BlockSpec((1, tm, d), lambda b, i: (b, i, 0))
    return pl.pallas_call(
        _final_kernel,
        grid=(bsz, n_tok // tm),
        in_specs=[tok, tok, pl.BlockSpec((None, 1, d), lambda b, i: (b, 0, 5)),
                  pl.BlockSpec((1, d), lambda b, i: (0, 0))],
        out_specs=tok,
        out_shape=jax.ShapeDtypeStruct((bsz, n_tok, d), F32),
        compiler_params=_cparams(("parallel", "parallel")),
        name="final_norm",
    )(x1, moe, mod, fg)


def _rope_tables(n_tok):
    rows = n_tok // GRID_W
    row = jnp.repeat(jnp.arange(rows, dtype=F32), GRID_W)
    col = jnp.tile(jnp.arange(GRID_W, dtype=F32), rows)
    half = ROPE // 2
    inv = ROPE_BASE ** (-jnp.arange(0, half, 2, dtype=F32) / half)
    cr, sr = jnp.cos(row[:, None] * inv), jnp.sin(row[:, None] * inv)
    cc, sc = jnp.cos(col[:, None] * inv), jnp.sin(col[:, None] * inv)
    z8 = jnp.zeros_like(cr)
    ones = jnp.ones((n_tok, NOPE), F32)
    pad = jnp.zeros((n_tok, HEAD_PAD - NOPE - ROPE), F32)
    c = jnp.concatenate([ones, cr, cr, cc, cc, pad + 1.0], axis=1)
    s1 = jnp.concatenate([ones * 0.0, -sr, z8, -sc, z8, pad], axis=1)
    s2 = jnp.concatenate([ones * 0.0, z8, sr, z8, sc, pad], axis=1)
    return c, s1, s2


def _identity_tables(n_tok):
    return (jnp.ones((n_tok, HEAD_PAD), F32), jnp.zeros((n_tok, HEAD_PAD), F32),
            jnp.zeros((n_tok, HEAD_PAD), F32))


def _pad_heads(w, per_head, take):
    k = w.shape[0]
    w = w.reshape(k, HEADS, per_head)[:, :, take]
    w = jnp.pad(w, ((0, 0), (0, 0), (0, HEAD_PAD - w.shape[2])))
    return w.reshape(k, HEADS * HEAD_PAD).astype(BF16)


def _layer_weights(i, p):
    d = p['w_in'].shape[1]
    w_in = p['w_in'][i]
    off_kv = Q_LORA
    off_conv = off_kv + KV_LORA + ROPE
    off_hy = off_conv + 2 * CONV_CH
    kr = jnp.zeros((d, LANE), F32).at[:, NOPE:NOPE + ROPE].set(w_in[:, off_kv + KV_LORA:off_conv])
    w_in_r = jnp.concatenate([w_in[:, :off_kv], w_in[:, off_kv:off_kv + KV_LORA], kr,
                              w_in[:, off_conv:off_hy], w_in[:, off_hy:]], axis=1).astype(BF16)
    gn = p['group_norm_g'][i]
    n_att = HEADS * VDIM
    rw_hi, rw_lo = _split(jnp.pad(p['router_w'][i], ((0, 0), (0, LANE - N_EXPERTS))))
    v_one = jnp.zeros((HEADS, HEAD_PAD), F32).at[:, VDIM].set(1.0).reshape(1, HEADS * HEAD_PAD)
    row = lambda a: a.reshape(1, -1)
    return {
        'norm1_g': row(p['norm1_g'][i]), 'w_in': w_in_r,
        'q_a_g': row(p['q_a_g'][i]), 'w_q': _pad_heads(p['w_q_b'][i], NOPE + ROPE, slice(None)),
        'kv_a_g': row(p['kv_a_g'][i]),
        'w_k': _pad_heads(p['w_kv_b'][i], NOPE + VDIM, slice(0, NOPE)),
        'w_v': _pad_heads(p['w_kv_b'][i], NOPE + VDIM, slice(NOPE, NOPE + VDIM)),
        'v_one': v_one,
        'conv_dw_w': p['conv_dw_w'][i], 'conv_dw_b': row(p['conv_dw_b'][i]),
        'conv_ln_g': row(p['conv_ln_g'][i]), 'conv_ln_b': row(p['conv_ln_b'][i]),
        'gn_att': row(gn[:n_att]), 'gn_conv': row(gn[n_att:n_att + CONV_CH]),
        'gn_hy': row(gn[n_att + CONV_CH:]),
        'hy_short_w': p['hy_short_w'][i], 'hy_short_b': row(p['hy_short_b'][i]),
        'w_out': p['w_out'][i].astype(BF16), 'norm2_g': row(p['norm2_g'][i]),
        'router_w': jnp.concatenate([rw_hi, rw_lo], axis=1),
        'w_gate': p['w_gate'][i].astype(BF16), 'w_up': p['w_up'][i].astype(BF16),
        'w_down': p['w_down'][i].astype(BF16),
    }


def _hyena_filters(n_tok, i, p):
    t = jnp.linspace(0.0, 1.0, n_tok, dtype=F32)[:, None]
    w = 2.0 * math.pi * jnp.arange(n_tok, dtype=F32) / n_tok
    f = jnp.linspace(1e-4, HY_BANDS - 1, HY_BANDS, dtype=F32)
    fw = w[:, None] * f[None, :]
    z = jnp.concatenate([t, jnp.cos(fw), -jnp.sin(fw)], axis=-1)
    freq = p['hy_sin_freq'][i]
    hp = lax.Precision.HIGHEST
    h = jnp.sin(freq[0] * (jnp.dot(z, p['hy_w1'][i], precision=hp) + p['hy_b1'][i]))
    h = jnp.sin(freq[1] * (jnp.dot(h, p['hy_w2'][i], precision=hp) + p['hy_b2'][i]))
    h = jnp.dot(h, p['hy_w3'][i], precision=hp).reshape(n_tok, HY_ORDER, 2, HY_CH)
    decay = jnp.exp(-t * jnp.abs(p['hy_decay'][i]))
    h = h * decay[:, None, None, :]
    h_fwd, h_bwd = h[:, :, 0], h[:, :, 1]
    g = jnp.concatenate([h_fwd, jnp.zeros((1, HY_ORDER, HY_CH), F32), h_bwd[:0:-1]], axis=0)
    return g * lax.rsqrt(jnp.sum(g * g, axis=0, keepdims=True) + EPS)


def _hyena_long(z, i, p):
    n_tok = z.shape[1]
    v, x1, x2 = z[..., :HY_CH], z[..., HY_CH:2 * HY_CH], z[..., 2 * HY_CH:]
    g_freq = jnp.fft.rfft(_hyena_filters(n_tok, i, p), n=2 * n_tok, axis=0)
    y = v
    for o, gate in enumerate((x1, x2)):
        spec = jnp.fft.rfft(y, n=2 * n_tok, axis=1)
        conv = jnp.fft.irfft(spec * g_freq[None, :, o], n=2 * n_tok, axis=1)[:, :n_tok]
        y = gate * (conv + y * p['hy_skip'][i][o])
    return y


def _route(aff_t, cap):
    gates, idx = lax.top_k(aff_t, cap)
    return idx.astype(jnp.int32), gates


def _tile(n, pref):
    return pref if n % pref == 0 else n


def kernel(x, c, ctx, c_ctx, mod_w, mod_b, norm1_g, w_in, q_a_g, w_q_b, kv_a_g, w_kv_b, conv_dw_w, conv_dw_b, conv_ln_g, conv_ln_b, hy_short_w, hy_short_b, hy_w1, hy_b1, hy_w2, hy_b2, hy_w3, hy_sin_freq, hy_decay, hy_skip, group_norm_g, w_out, norm2_g, router_w, w_gate, w_up, w_down, final_norm_g):
    p = dict(mod_w=mod_w, mod_b=mod_b, norm1_g=norm1_g, w_in=w_in, q_a_g=q_a_g, w_q_b=w_q_b,
             kv_a_g=kv_a_g, w_kv_b=w_kv_b, conv_dw_w=conv_dw_w, conv_dw_b=conv_dw_b,
             conv_ln_g=conv_ln_g, conv_ln_b=conv_ln_b, hy_short_w=hy_short_w,
             hy_short_b=hy_short_b, hy_w1=hy_w1, hy_b1=hy_b1, hy_w2=hy_w2, hy_b2=hy_b2,
             hy_w3=hy_w3, hy_sin_freq=hy_sin_freq, hy_decay=hy_decay, hy_skip=hy_skip,
             group_norm_g=group_norm_g, w_out=w_out, norm2_g=norm2_g, router_w=router_w,
             w_gate=w_gate, w_up=w_up, w_down=w_down)
    depth = mod_w.shape[0]
    bsz, n_lat, d = x.shape
    n_ctx = ctx.shape[1]

    rows = -(-(bsz + 1) // 8) * 8
    cc = jnp.concatenate([c, c_ctx[None, :], jnp.zeros((rows - bsz - 1, d), F32)], axis=0)
    mod_all = _modulation(cc, mod_w, mod_b)
    lat_row = lambda b: b
    ctx_row = lambda b: bsz

    tabs_l = _rope_tables(n_lat)
    tabs_c = _identity_tables(n_ctx)
    tm_l, tm_c = _tile(n_lat, 512), _tile(n_ctx, 256)
    tq_l, tq_c = _tile(n_lat, 256), _tile(n_ctx, 256)
    tl_l, tl_c = _tile(n_lat, 256), _tile(n_ctx, 256)

    xl, xc = x, ctx
    prev_l = prev_c = None
    for i in range(depth):
        last = i == depth - 1
        lw = _layer_weights(i, p)
        mod = mod_all[i].reshape(rows, 1, N_MOD * d)

        def side(xs, prev, row_of_b, tabs, tm, tq, tl, extra_keys, need_mix):
            xs, q, k, v, uc, uh = _inproj(xs, prev, mod, row_of_b, lw, tabs, tm)
            if not need_mix:
                return xs, (k, v), None
            att = _attention(q, [(k, v)] + extra_keys, tq)
            cn, z = _conv(uc, uh, lw, tl)
            hy = _hyena_long(z, i, p)
            x1, h2p, aff_t = _outproj(att, cn, hy, xs, mod, row_of_b, lw, tm)
            cap = CAPACITY_FACTOR * xs.shape[1] // N_EXPERTS
            idx, gates = _route(aff_t, cap)
            moe = _moe(h2p, idx, gates, lw)
            return x1, (k, v), moe

        xc, kv_c, moe_c = side(xc, prev_c, ctx_row, tabs_c, tm_c, tq_c, tl_c, [], not last)
        xl, _, moe_l = side(xl, prev_l, lat_row, tabs_l, tm_l, tq_l, tl_l, [kv_c], True)
        prev_l = (moe_l, mod)
        prev_c = None if moe_c is None else (moe_c, mod)

    mod = mod_all[depth - 1].reshape(rows, 1, N_MOD * d)
    return _final(xl, prev_l[0], mod, final_norm_g.reshape(1, d), tm_l)
```

```python
import functools
import math
from typing import NamedTuple

import numpy as np
import jax
import jax.numpy as jnp
from jax import lax
from jax.experimental import pallas as pl
from jax.experimental.pallas import tpu as pltpu

F32 = jnp.float32
BF16 = jnp.bfloat16
EPS = 1e-6

GRID_W = 64
N_MOD = 6
HEADS = 8
NOPE = 64
ROPE = 32
VDIM = 64
Q_LORA = 256
KV_LORA = 128
ROPE_BASE = 10000.0
CONV_CH = 256
CONV_K = 31
HY_CH = 256
HY_ORDER = 2
HY_BANDS = 16
N_EXPERTS = 16
CAPACITY_FACTOR = 2

LANE = 128
SUBLANE = 8
MXU_DIM = 256
HEAD_PAD = LANE
HALO = 16
VMEM_LIMIT = 56 * 1024 * 1024

C_Q = 0
C_KVN = C_Q + Q_LORA
C_KR = C_KVN + KV_LORA
C_CONV = C_KR + LANE
C_HY = C_CONV + 2 * CONV_CH
C_END = C_HY + 3 * HY_CH


def _cparams(sem):
    return pltpu.CompilerParams(dimension_semantics=sem, vmem_limit_bytes=VMEM_LIMIT)


def _rms(x, g):
    return x * lax.rsqrt(jnp.mean(x * x, axis=-1, keepdims=True) + EPS) * g


def _split(a):
    hi = a.astype(BF16)
    lo = (a - hi.astype(F32)).astype(BF16)
    return hi, lo


def _dot(a, b):
    return jnp.dot(a, b, preferred_element_type=F32)


def _mod_kernel(c_ref, w_ref, b_ref, o_ref):
    c = c_ref[...]
    a = c * jax.nn.sigmoid(c)
    a_hi, a_lo = _split(a)
    w_hi, w_lo = _split(w_ref[0])
    o_ref[0] = _dot(a_hi, w_hi) + _dot(a_lo, w_hi) + _dot(a_hi, w_lo) + b_ref[0]


def _modulation(cc, mod_w, mod_b):
    depth, d, n = mod_w.shape
    rows = cc.shape[0]
    tn = 1536
    return pl.pallas_call(
        _mod_kernel,
        grid=(depth, n // tn),
        in_specs=[
            pl.BlockSpec((rows, d), lambda l, j: (0, 0)),
            pl.BlockSpec((1, d, tn), lambda l, j: (l, 0, j)),
            pl.BlockSpec((1, 1, tn), lambda l, j: (l, 0, j)),
        ],
        out_specs=pl.BlockSpec((1, rows, tn), lambda l, j: (l, 0, j)),
        out_shape=jax.ShapeDtypeStruct((depth, rows, n), F32),
        compiler_params=_cparams(("arbitrary", "arbitrary")),
        name="modulation",
    )(cc, mod_w, mod_b.reshape(depth, 1, n))


def _rope(x, c, s1, s2):
    return x * c + pltpu.roll(x, LANE - 8, 1) * s1 + pltpu.roll(x, 8, 1) * s2


def _inproj_kernel(*refs, fuse_prev, scale):
    if fuse_prev:
        xa_ref, xb_ref, g2_ref = refs[:3]
        refs = refs[3:]
    else:
        xa_ref = refs[0]
        refs = refs[1:]
    (sh_ref, sc_ref, n1g_ref, win_ref, qag_ref, wq_ref, kvg_ref, wk_ref, wv_ref,
     c_ref, s1_ref, s2_ref, vone_ref) = refs[:13]
    outs = refs[13:]
    if fuse_prev:
        x_out, q_out, k_out, v_out, uc_out, uh_out = outs
        x = xa_ref[0] + g2_ref[...] * xb_ref[0]
        x_out[0] = x
    else:
        q_out, k_out, v_out, uc_out, uh_out = outs
        x = xa_ref[0]
    hn = _rms(x, n1g_ref[...]) * (1.0 + sc_ref[...]) + sh_ref[...]
    u = _dot(hn.astype(BF16), win_ref[...])
    c, s1, s2 = c_ref[...], s1_ref[...], s2_ref[...]

    nq = _rms(u[:, C_Q:C_KVN], qag_ref[...]).astype(BF16)
    q = _dot(nq, wq_ref[...])
    for h in range(HEADS):
        sl = slice(h * HEAD_PAD, (h + 1) * HEAD_PAD)
        q_out[0, :, sl] = (_rope(q[:, sl], c, s1, s2) * scale).astype(BF16)

    kvn = _rms(u[:, C_KVN:C_KR], kvg_ref[...]).astype(BF16)
    kr = _rope(u[:, C_KR:C_CONV], c, s1, s2)
    k = _dot(kvn, wk_ref[...])
    for h in range(HEADS):
        sl = slice(h * HEAD_PAD, (h + 1) * HEAD_PAD)
        k_out[0, :, sl] = (k[:, sl] + kr).astype(BF16)
    vt = lax.dot_general(wv_ref[...], kvn, (((1,), (1,)), ((), ())), preferred_element_type=F32)
    v_out[0] = (vt + vone_ref[...]).astype(BF16)
    uc_out[0] = u[:, C_CONV:C_HY]
    uh_out[0] = u[:, C_HY:C_END]


def _inproj(x, prev, mod, row_of_b, lw, tabs, tm):
    bsz, n_tok, d = x.shape
    fuse_prev = prev is not None
    grid = (bsz, n_tok // tm)
    tok = lambda w: pl.BlockSpec((1, tm, w), lambda b, i: (b, i, 0))
    modspec = lambda k: pl.BlockSpec((None, 1, d), lambda b, i: (row_of_b(b), 0, k))
    full = lambda a: pl.BlockSpec(a.shape, lambda b, i: (0,) * a.ndim)
    tabspec = pl.BlockSpec((tm, LANE), lambda b, i: (i, 0))

    args, specs = [x], [tok(d)]
    if fuse_prev:
        args += [prev[0], prev[1]]
        specs += [tok(d), modspec(5)]
    args += [mod, mod, lw['norm1_g'], lw['w_in'], lw['q_a_g'], lw['w_q'], lw['kv_a_g'],
             lw['w_k'], lw['w_v'], tabs[0], tabs[1], tabs[2], lw['v_one']]
    specs += [modspec(0), modspec(1), full(lw['norm1_g']), full(lw['w_in']), full(lw['q_a_g']),
              full(lw['w_q']), full(lw['kv_a_g']), full(lw['w_k']), full(lw['w_v']),
              tabspec, tabspec, tabspec, full(lw['v_one'])]
    hp = HEADS * HEAD_PAD
    out_shape, out_specs = [], []
    if fuse_prev:
        out_shape.append(jax.ShapeDtypeStruct((bsz, n_tok, d), F32))
        out_specs.append(tok(d))
    out_shape += [jax.ShapeDtypeStruct((bsz, n_tok, hp), BF16)] * 2
    out_specs += [tok(hp)] * 2
    out_shape += [jax.ShapeDtypeStruct((bsz, hp, n_tok), BF16),
                  jax.ShapeDtypeStruct((bsz, n_tok, 2 * CONV_CH), F32),
                  jax.ShapeDtypeStruct((bsz, n_tok, 3 * HY_CH), F32)]
    out_specs += [pl.BlockSpec((1, hp, tm), lambda b, i: (b, 0, i)),
                  tok(2 * CONV_CH), tok(3 * HY_CH)]
    scale = float((NOPE + ROPE) ** -0.5 * math.log2(math.e))
    res = pl.pallas_call(
        functools.partial(_inproj_kernel, fuse_prev=fuse_prev, scale=scale),
        grid=grid, in_specs=specs, out_specs=out_specs, out_shape=out_shape,
        compiler_params=_cparams(("parallel", "parallel")),
        name="inproj",
    )(*args)
    if not fuse_prev:
        res = [x] + list(res)
    return res


def _attn_kernel(*refs, n_sets, tk, ahead):
    q_ref = refs[0]
    kv = refs[1:1 + 2 * n_sets]
    o_ref = refs[1 + 2 * n_sets]
    nt = (((1,), (1,)), ((), ()))
    sls = [slice(hh * HEAD_PAD, (hh + 1) * HEAD_PAD) for hh in range(2)]
    qs = [q_ref[0, :, sl] for sl in sls]
    items = [(hh, i, c) for i in range(n_sets) for c in range(kv[2 * i].shape[1] // tk)
             for hh in range(2)]

    def scores(item):
        hh, i, c = item
        return lax.dot_general(kv[2 * i][0, c * tk:(c + 1) * tk, sls[hh]], qs[hh], nt,
                               preferred_element_type=F32)

    m, o = [None, None], [None, None]
    pending = [scores(it) for it in items[:ahead]]
    for n, (hh, i, c) in enumerate(items):
        s = pending.pop(0)
        if n + ahead < len(items):
            pending.append(scores(items[n + ahead]))
        mt = jnp.max(s, axis=0, keepdims=True)
        m_new = mt if m[hh] is None else jnp.maximum(m[hh], mt)
        pv = _dot(kv[2 * i + 1][0, sls[hh], c * tk:(c + 1) * tk],
                  jnp.exp2(s - m_new).astype(BF16))
        o[hh] = pv if o[hh] is None else o[hh] * jnp.exp2(m[hh] - m_new) + pv
        m[hh] = m_new
    outs = [(oh * (1.0 / oh[VDIM:VDIM + 1, :])).T for oh in o]
    lane = lax.broadcasted_iota(jnp.int32, outs[0].shape, 1)
    o_ref[0] = jnp.where(lane < VDIM, outs[0], pltpu.roll(outs[1], VDIM, 1))


def _attention(q, key_sets, tq):
    bsz, n_q, hp = q.shape
    grid = (bsz, HEADS // 2, n_q // tq)
    args = [q]
    specs = [pl.BlockSpec((1, tq, 2 * HEAD_PAD), lambda b, h, i: (b, i, h))]
    for k, v in key_sets:
        n_k = k.shape[1]
        args += [k, v]
        specs += [pl.BlockSpec((1, n_k, 2 * HEAD_PAD), lambda b, h, i: (b, 0, h)),
                  pl.BlockSpec((1, 2 * HEAD_PAD, n_k), lambda b, h, i: (b, h, 0))]
    return pl.pallas_call(
        functools.partial(_attn_kernel, n_sets=len(key_sets),
                          tk=min([MXU_DIM] + [k.shape[1] for k, _ in key_sets]), ahead=8),
        grid=grid, in_specs=specs,
        out_specs=pl.BlockSpec((1, tq, 2 * VDIM), lambda b, h, i: (b, i, h)),
        out_shape=jax.ShapeDtypeStruct((bsz, n_q, HEADS * VDIM), F32),
        compiler_params=_cparams(("parallel", "parallel", "arbitrary")),
        name="attention",
    )(*args)


def _conv_kernel(ucp_ref, uc_ref, ucn_ref, uhp_ref, uh_ref, uhn_ref,
                 cw_ref, cb_ref, lg_ref, lb_ref, gn_ref, hw_ref, hb_ref,
                 cn_out, v_out, x1_out, x2_out, ypad, hpad, *, tl, row_tile):
    i = pl.program_id(1)
    has_prev = (i > 0).astype(F32)
    has_next = (i < pl.num_programs(1) - 1).astype(F32)

    def glu(u):
        return u[:, :CONV_CH] * jax.nn.sigmoid(u[:, CONV_CH:])

    ypad[0:HALO, :] = glu(ucp_ref[0]) * has_prev
    ypad[HALO:HALO + tl, :] = glu(uc_ref[0])
    ypad[HALO + tl:, :] = glu(ucn_ref[0]) * has_next
    hpad[0:HALO, :] = uhp_ref[0] * has_prev
    hpad[HALO:HALO + tl, :] = uh_ref[0]
    hpad[HALO + tl:, :] = uhn_ref[0] * has_next

    half = CONV_K // 2
    for r in range(tl // row_tile):
        base = r * row_tile
        acc = jnp.zeros((row_tile, CONV_CH), F32) + cb_ref[...]
        for k in range(CONV_K):
            off = HALO + base + k - half
            acc = acc + cw_ref[k:k + 1, :] * ypad[off:off + row_tile, :]
        mu = jnp.mean(acc, axis=-1, keepdims=True)
        cen = acc - mu
        var = jnp.mean(cen * cen, axis=-1, keepdims=True)
        y = cen * lax.rsqrt(var + EPS) * lg_ref[...] + lb_ref[...]
        y = y * jax.nn.sigmoid(y)
        cn_out[0, base:base + row_tile, :] = _rms(y, gn_ref[...])

        z = jnp.zeros((row_tile, 3 * HY_CH), F32) + hb_ref[...]
        for k in range(3):
            off = HALO + base + k - 1
            z = z + hw_ref[k:k + 1, :] * hpad[off:off + row_tile, :]
        v_out[0, base:base + row_tile, :] = z[:, :HY_CH]
        x1_out[0, base:base + row_tile, :] = z[:, HY_CH:2 * HY_CH]
        x2_out[0, base:base + row_tile, :] = z[:, 2 * HY_CH:]


def _conv(uc, uh, lw, tl):
    bsz, n_tok, _ = uc.shape
    nh = tl // HALO
    last = n_tok // HALO - 1
    cur = lambda w: pl.BlockSpec((1, tl, w), lambda b, i: (b, i, 0))
    prv = lambda w: pl.BlockSpec((1, HALO, w), lambda b, i: (b, jnp.maximum(i * nh - 1, 0), 0))
    nxt = lambda w: pl.BlockSpec((1, HALO, w), lambda b, i: (b, jnp.minimum((i + 1) * nh, last), 0))
    full = lambda a: pl.BlockSpec(a.shape, lambda b, i: (0,) * a.ndim)
    wts = [lw['conv_dw_w'], lw['conv_dw_b'], lw['conv_ln_g'], lw['conv_ln_b'], lw['gn_conv'],
           lw['hy_short_w'], lw['hy_short_b']]
    return pl.pallas_call(
        functools.partial(_conv_kernel, tl=tl, row_tile=min(64, tl)),
        grid=(bsz, n_tok // tl),
        in_specs=[prv(2 * CONV_CH), cur(2 * CONV_CH), nxt(2 * CONV_CH),
                  prv(3 * HY_CH), cur(3 * HY_CH), nxt(3 * HY_CH)] + [full(w) for w in wts],
        out_specs=[cur(CONV_CH)] + [cur(HY_CH)] * 3,
        out_shape=[jax.ShapeDtypeStruct((bsz, n_tok, CONV_CH), F32)]
        + [jax.ShapeDtypeStruct((bsz, n_tok, HY_CH), F32)] * 3,
        scratch_shapes=[pltpu.VMEM((tl + 2 * HALO, CONV_CH), F32),
                        pltpu.VMEM((tl + 2 * HALO, 3 * HY_CH), F32)],
        compiler_params=_cparams(("parallel", "parallel")),
        name="conv",
    )(uc, uc, uc, uh, uh, uh, *wts)


def _outproj_kernel(att_ref, cn_ref, hy_ref, x_ref, g1_ref, sh_ref, sc_ref, gna_ref, gnh_ref,
                    wo_ref, n2g_ref, rw_ref, x1_out, h2p_out, aff_out):
    a = _rms(att_ref[0], gna_ref[...]).astype(BF16)
    c = cn_ref[0].astype(BF16)
    h = _rms(hy_ref[0], gnh_ref[...]).astype(BF16)
    na, nc = a.shape[1], c.shape[1]
    y = (_dot(a, wo_ref[0:na, :]) + _dot(c, wo_ref[na:na + nc, :]) + _dot(h, wo_ref[na + nc:, :]))
    x1 = x_ref[0] + g1_ref[...] * y
    x1_out[0] = x1
    h2 = _rms(x1, n2g_ref[...]) * (1.0 + sc_ref[...]) + sh_ref[...]
    d = h2.shape[1]
    lo = lax.bitcast_convert_type(h2[:, :d // 2].astype(BF16).astype(F32), jnp.uint32)
    hi = lax.bitcast_convert_type(h2[:, d // 2:].astype(BF16).astype(F32), jnp.uint32)
    h2p_out[0] = hi | (lo >> 16)
    h_hi, h_lo = _split(h2)
    t = _dot(h_hi, rw_ref[...])
    logits = t[:, :LANE] + t[:, LANE:] + _dot(h_lo, rw_ref[:, :LANE])
    lane = lax.broadcasted_iota(jnp.int32, logits.shape, 1)
    logits = jnp.where(lane < N_EXPERTS, logits, -1e30)
    e = jnp.exp(logits - jnp.max(logits, axis=1, keepdims=True))
    aff = e / jnp.sum(e, axis=1, keepdims=True)
    aff_out[0] = aff.T[:N_EXPERTS, :]


def _outproj(att, cn, hy, x, mod, row_of_b, lw, tm):
    bsz, n_tok, d = x.shape
    tok = lambda w: pl.BlockSpec((1, tm, w), lambda b, i: (b, i, 0))
    modspec = lambda k: pl.BlockSpec((None, 1, d), lambda b, i: (row_of_b(b), 0, k))
    full = lambda a: pl.BlockSpec(a.shape, lambda b, i: (0,) * a.ndim)
    wts = [lw['gn_att'], lw['gn_hy'], lw['w_out'], lw['norm2_g'], lw['router_w']]
    return pl.pallas_call(
        _outproj_kernel,
        grid=(bsz, n_tok // tm),
        in_specs=[tok(att.shape[2]), tok(cn.shape[2]), tok(hy.shape[2]), tok(d),
                  modspec(2), modspec(3), modspec(4)] + [full(w) for w in wts],
        out_specs=[tok(d), tok(d // 2), pl.BlockSpec((1, N_EXPERTS, tm), lambda b, i: (b, 0, i))],
        out_shape=[jax.ShapeDtypeStruct((bsz, n_tok, d), F32),
                   jax.ShapeDtypeStruct((bsz, n_tok, d // 2), jnp.uint32),
                   jax.ShapeDtypeStruct((bsz, N_EXPERTS, n_tok), F32)],
        compiler_params=_cparams(("parallel", "parallel")),
        name="outproj",
    )(att, cn, hy, x, mod, mod, mod, *wts)


def _moe_kernel(idx_ref, gate_ref, h2p_ref, wg_ref, wu_ref, wd_ref, out_hbm,
                acc_ref, xg_ref, y_ref, sem, *, cap, pc, fc):
    b = pl.program_id(0)
    e = pl.program_id(1)

    @pl.when(e == 0)
    def _():
        acc_ref[...] = jnp.zeros_like(acc_ref)

    def gather(j, carry):
        t = idx_ref[0, 0, j]
        src = pl.ds(pl.multiple_of(t * pc, pc), pc)
        xg_ref[pl.ds(pl.multiple_of(j * pc, pc), pc), :] = h2p_ref[0, src, :]
        return carry
    lax.fori_loop(0, cap, gather, 0, unroll=8)

    lo, hi = [], []
    for c in range(pc):
        w = xg_ref[pl.ds(c, cap, stride=pc), :]
        lo.append(lax.bitcast_convert_type(w << 16, F32).astype(BF16))
        hi.append(lax.bitcast_convert_type(w & jnp.uint32(0xFFFF0000), F32).astype(BF16))
    x = jnp.concatenate(lo + hi, axis=1)
    a = _dot(x, wg_ref[0])
    u = _dot(x, wu_ref[0])
    hmid = (a * jax.nn.sigmoid(a) * u).astype(BF16)
    y = _dot(hmid, wd_ref[0])
    for c in range(fc):
        y_ref[pl.ds(c, cap, stride=fc), :] = y[:, c * LANE:(c + 1) * LANE]

    def scatter(j, carry):
        t = idx_ref[0, 0, j]
        g = gate_ref[0, 0, j]
        dst = pl.ds(pl.multiple_of(t * fc, fc), fc)
        acc_ref[dst, :] = acc_ref[dst, :] + g * y_ref[pl.ds(pl.multiple_of(j * fc, fc), fc), :]
        return carry
    lax.fori_loop(0, cap, scatter, 0, unroll=8)

    @pl.when(e == pl.num_programs(1) - 1)
    def _():
        cp = pltpu.make_async_copy(acc_ref, out_hbm.at[b], sem)
        cp.start()
        cp.wait()


def _moe(h2p, idx, gates, lw):
    bsz, n_tok, half = h2p.shape
    d = 2 * half
    pc, fc = half // LANE, d // LANE
    n_e, cap = idx.shape[1], idx.shape[2]
    ff = lw['w_gate'].shape[2]
    smem = lambda: pl.BlockSpec((1, 1, cap), lambda b, e: (b * n_e + e, 0, 0),
                                memory_space=pltpu.SMEM)
    out = pl.pallas_call(
        functools.partial(_moe_kernel, cap=cap, pc=pc, fc=fc),
        grid=(bsz, n_e),
        in_specs=[smem(), smem(),
                  pl.BlockSpec((1, n_tok * pc, LANE), lambda b, e: (b, 0, 0)),
                  pl.BlockSpec((1, d, ff), lambda b, e: (e, 0, 0)),
                  pl.BlockSpec((1, d, ff), lambda b, e: (e, 0, 0)),
                  pl.BlockSpec((1, ff, d), lambda b, e: (e, 0, 0))],
        out_specs=pl.BlockSpec(memory_space=pl.ANY),
        out_shape=jax.ShapeDtypeStruct((bsz, n_tok * fc, LANE), F32),
        scratch_shapes=[pltpu.VMEM((n_tok * fc, LANE), F32),
                        pltpu.VMEM((cap * pc, LANE), jnp.uint32),
                        pltpu.VMEM((cap * fc, LANE), F32),
                        pltpu.SemaphoreType.DMA(())],
        compiler_params=_cparams(("arbitrary", "arbitrary")),
        name="moe",
    )(idx.reshape(bsz * n_e, 1, cap), gates.reshape(bsz * n_e, 1, cap),
      h2p.reshape(bsz, n_tok * pc, LANE), lw['w_gate'], lw['w_up'], lw['w_down'])
    return out.reshape(bsz, n_tok, d)


def _final_kernel(x_ref, m_ref, g2_ref, fg_ref, o_ref):
    o_ref[0] = _rms(x_ref[0] + g2_ref[...] * m_ref[0], fg_ref[...])


def _final(x1, moe, mod, fg, tm):
    bsz, n_tok, d = x1.shape
    tok = pl.BlockSpec((1, tm, d), lambda b, i: (b, i, 0))
    return pl.pallas_call(
        _final_kernel,
        grid=(bsz, n_tok // tm),
        in_specs=[tok, tok, pl.BlockSpec((None, 1, d), lambda b, i: (b, 0, 5)),
                  pl.BlockSpec((1, d), lambda b, i: (0, 0))],
        out_specs=tok,
        out_shape=jax.ShapeDtypeStruct((bsz, n_tok, d), F32),
        compiler_params=_cparams(("parallel", "parallel")),
        name="final_norm",
    )(x1, moe, mod, fg)


def _rope_tables(n_tok):
    rows = n_tok // GRID_W
    row = jnp.repeat(jnp.arange(rows, dtype=F32), GRID_W)
    col = jnp.tile(jnp.arange(GRID_W, dtype=F32), rows)
    half = ROPE // 2
    inv = ROPE_BASE ** (-jnp.arange(0, half, 2, dtype=F32) / half)
    cr, sr = jnp.cos(row[:, None] * inv), jnp.sin(row[:, None] * inv)
    cc, sc = jnp.cos(col[:, None] * inv), jnp.sin(col[:, None] * inv)
    z8 = jnp.zeros_like(cr)
    ones = jnp.ones((n_tok, NOPE), F32)
    pad = jnp.zeros((n_tok, HEAD_PAD - NOPE - ROPE), F32)
    c = jnp.concatenate([ones, cr, cr, cc, cc, pad + 1.0], axis=1)
    s1 = jnp.concatenate([ones * 0.0, -sr, z8, -sc, z8, pad], axis=1)
    s2 = jnp.concatenate([ones * 0.0, z8, sr, z8, sc, pad], axis=1)
    return c, s1, s2


def _identity_tables(n_tok):
    return (jnp.ones((n_tok, HEAD_PAD), F32), jnp.zeros((n_tok, HEAD_PAD), F32),
            jnp.zeros((n_tok, HEAD_PAD), F32))


def _pad_heads(w, per_head, take):
    k = w.shape[0]
    w = w.reshape(k, HEADS, per_head)[:, :, take]
    w = jnp.pad(w, ((0, 0), (0, 0), (0, HEAD_PAD - w.shape[2])))
    return w.reshape(k, HEADS * HEAD_PAD).astype(BF16)


def _layer_weights(i, p):
    d = p['w_in'].shape[1]
    w_in = p['w_in'][i]
    off_kv = Q_LORA
    off_conv = off_kv + KV_LORA + ROPE
    off_hy = off_conv + 2 * CONV_CH
    kr = jnp.zeros((d, LANE), F32).at[:, NOPE:NOPE + ROPE].set(w_in[:, off_kv + KV_LORA:off_conv])
    w_in_r = jnp.concatenate([w_in[:, :off_kv], w_in[:, off_kv:off_kv + KV_LORA], kr,
                              w_in[:, off_conv:off_hy], w_in[:, off_hy:]], axis=1).astype(BF16)
    gn = p['group_norm_g'][i]
    n_att = HEADS * VDIM
    rw_hi, rw_lo = _split(jnp.pad(p['router_w'][i], ((0, 0), (0, LANE - N_EXPERTS))))
    v_one = jnp.zeros((HEADS, HEAD_PAD), F32).at[:, VDIM].set(1.0).reshape(HEADS * HEAD_PAD, 1)
    row = lambda a: a.reshape(1, -1)
    return {
        'norm1_g': row(p['norm1_g'][i]), 'w_in': w_in_r,
        'q_a_g': row(p['q_a_g'][i]), 'w_q': _pad_heads(p['w_q_b'][i], NOPE + ROPE, slice(None)),
        'kv_a_g': row(p['kv_a_g'][i]),
        'w_k': _pad_heads(p['w_kv_b'][i], NOPE + VDIM, slice(0, NOPE)),
        'w_v': _pad_heads(p['w_kv_b'][i], NOPE + VDIM, slice(NOPE, NOPE + VDIM)).T,
        'v_one': v_one,
        'conv_dw_w': p['conv_dw_w'][i], 'conv_dw_b': row(p['conv_dw_b'][i]),
        'conv_ln_g': row(p['conv_ln_g'][i]), 'conv_ln_b': row(p['conv_ln_b'][i]),
        'gn_att': row(gn[:n_att]), 'gn_conv': row(gn[n_att:n_att + CONV_CH]),
        'gn_hy': row(gn[n_att + CONV_CH:]),
        'hy_short_w': p['hy_short_w'][i], 'hy_short_b': row(p['hy_short_b'][i]),
        'w_out': p['w_out'][i].astype(BF16), 'norm2_g': row(p['norm2_g'][i]),
        'router_w': jnp.concatenate([rw_hi, rw_lo], axis=1),
        'w_gate': p['w_gate'][i].astype(BF16), 'w_up': p['w_up'][i].astype(BF16),
        'w_down': p['w_down'][i].astype(BF16),
    }


def _tile(n, pref):
    return pref if n % pref == 0 else n


def _dot3(a, b):
    a_hi, a_lo = _split(a)
    b_hi, b_lo = _split(b)
    return _dot(a_hi, b_hi) + _dot(a_lo, b_hi) + _dot(a_hi, b_lo)


class _FftPlan(NamedTuple):
    n2: int
    nh: int
    k1p: int
    kb: int
    f1: np.ndarray
    g1: np.ndarray
    mf: np.ndarray
    mi: np.ndarray


@functools.lru_cache(maxsize=None)
def _fft_plan(n_tok):
    n = 2 * n_tok
    n2 = 64 if n_tok >= 2048 else 16
    n1 = n // n2
    nh = n1 // 2
    k1 = nh + 1
    k1p = -(-k1 // SUBLANE) * SUBLANE
    two_pi = 2.0 * np.pi
    r = np.arange(k1)
    ang1 = two_pi * ((np.arange(nh)[None, :] * r[:, None]) % n1) / n1
    f1 = np.zeros((2 * k1p, nh))
    f1[:k1], f1[k1p:k1p + k1] = np.cos(ang1), -np.sin(ang1)
    w = np.where((r == 0) | (r == nh), 1.0, 2.0)[None, :] / n
    g1 = np.zeros((nh, 2 * k1p))
    g1[:, :k1], g1[:, k1p:k1p + k1] = np.cos(ang1.T) * w, -np.sin(ang1.T) * w
    k = r[:, None, None] + n1 * np.arange(n2)[None, :, None]
    th = two_pi * ((k * np.arange(n2)[None, None, :]) % n) / n
    tc, ts = np.cos(th), -np.sin(th)
    mf = np.zeros((k1p, 2 * n2, 2 * n2))
    mi = np.zeros((k1p, 2 * n2, 2 * n2))
    mf[:k1, :n2, :n2], mf[:k1, :n2, n2:], mf[:k1, n2:, :n2], mf[:k1, n2:, n2:] = tc, -ts, ts, tc
    tct, tst = tc.transpose(0, 2, 1), ts.transpose(0, 2, 1)
    mi[:k1, :n2, :n2], mi[:k1, :n2, n2:], mi[:k1, n2:, :n2], mi[:k1, n2:, n2:] = tct, tst, -tst, tct
    return _FftPlan(n2, nh, k1p, SUBLANE, f1, g1, mf, mi)


def _dft1_kernel(x_ref, f_ref, a_out, *, precise):
    if precise:
        a_out[0] = _dot3(f_ref[...], x_ref[0])
    else:
        a_out[0] = _dot(f_ref[...], x_ref[0].astype(BF16)).astype(a_out.dtype)


def _dft1(xv, f1, precise, tj):
    bsz, nh, j = xv.shape
    rows = f1.shape[0]
    return pl.pallas_call(
        functools.partial(_dft1_kernel, precise=precise),
        grid=(bsz, j // tj),
        in_specs=[pl.BlockSpec((1, nh, tj), lambda b, i: (b, 0, i)),
                  pl.BlockSpec((rows, nh), lambda b, i: (0, 0))],
        out_specs=pl.BlockSpec((1, rows, tj), lambda b, i: (b, 0, i)),
        out_shape=jax.ShapeDtypeStruct((bsz, rows, j), F32 if precise else BF16),
        compiler_params=_cparams(("parallel", "parallel")),
        name="hyena_dft1",
    )(xv, f1)


def _spec_kernel(a_ref, mf_ref, *rest, kb, n2, filt):
    if filt:
        (y_out,) = rest
    else:
        mi_ref, yf_ref, ss_ref, v_out = rest
        s = lax.rsqrt(ss_ref[0] + ss_ref[1] + EPS)
    for k in range(kb):
        a = jnp.concatenate([a_ref[0, 0, k], a_ref[0, 1, k]], axis=0)
        if filt:
            y = _dot3(mf_ref[k], a)
            y_out[0, 0, k] = y[:n2]
            y_out[0, 1, k] = y[n2:]
        else:
            y = _dot(mf_ref[k], a)
            yr, yi = y[:n2], y[n2:]
            gr = (yf_ref[0, 0, k] + yf_ref[1, 0, k]) * s
            gi = (yf_ref[0, 1, k] - yf_ref[1, 1, k]) * s
            z = jnp.concatenate([yr * gr - yi * gi, yr * gi + yi * gr], axis=0).astype(BF16)
            v = _dot(mi_ref[k], z)
            v_out[0, 0, k] = v[:n2].astype(BF16)
            v_out[0, 1, k] = v[n2:].astype(BF16)


def _spectral(a, plan, mats, filt, order=0):
    bsz, _, k1p, n2, ch = a.shape
    kb = plan.kb
    blk = pl.BlockSpec((1, 2, kb, n2, ch), lambda kk, b: (b, 0, kk, 0, 0))
    mat = pl.BlockSpec((kb, 2 * n2, 2 * n2), lambda kk, b: (kk, 0, 0))
    if filt is None:
        args, specs = [a, mats['mf32']], [blk, mat]
        out_dtype = F32
    else:
        yf, ss = filt
        args = [a, mats['mf'], mats['mi'], yf, ss]
        specs = [blk, mat, mat,
                 pl.BlockSpec((2, 2, kb, n2, ch), lambda kk, b: (order, 0, kk, 0, 0)),
                 pl.BlockSpec((2, 1, ch), lambda kk, b: (order, 0, 0))]
        out_dtype = BF16
    return pl.pallas_call(
        functools.partial(_spec_kernel, kb=kb, n2=n2, filt=filt is None),
        grid=(k1p // kb, bsz), in_specs=specs, out_specs=blk,
        out_shape=jax.ShapeDtypeStruct(a.shape, out_dtype),
        compiler_params=_cparams(("parallel", "parallel")),
        name="hyena_spectral",
    )(*args)


def _idft1_kernel(v_ref, g1_ref, u_ref, gate_ref, skip_ref, *rest, fuse_next):
    conv = _dot(g1_ref[...], v_ref[0])
    y = gate_ref[0] * (conv + u_ref[0] * skip_ref[...])
    if fuse_next:
        f_ref, y_out, a_out = rest
        a_out[0] = _dot(f_ref[...], y.astype(BF16)).astype(BF16)
    else:
        (y_out,) = rest
    y_out[0] = y


def _idft1(vv, mats, u, gate, skip, fuse_next, tj):
    bsz, rows, j = vv.shape
    nh = u.shape[1]
    tokspec = pl.BlockSpec((1, nh, tj), lambda b, i: (b, 0, i))
    specspec = pl.BlockSpec((1, rows, tj), lambda b, i: (b, 0, i))
    args = [vv, mats['g1'], u, gate, skip]
    specs = [specspec, pl.BlockSpec((nh, rows), lambda b, i: (0, 0)), tokspec, tokspec,
             pl.BlockSpec((1, tj), lambda b, i: (0, i))]
    out_shape, out_specs = [jax.ShapeDtypeStruct((bsz, nh, j), F32)], [tokspec]
    if fuse_next:
        args.append(mats['f1'])
        specs.append(pl.BlockSpec((rows, nh), lambda b, i: (0, 0)))
        out_shape.append(jax.ShapeDtypeStruct((bsz, rows, j), BF16))
        out_specs.append(specspec)
    return pl.pallas_call(
        functools.partial(_idft1_kernel, fuse_next=fuse_next),
        grid=(bsz, j // tj), in_specs=specs, out_specs=out_specs, out_shape=out_shape,
        compiler_params=_cparams(("parallel", "parallel")),
        name="hyena_idft1",
    )(*args)


def _filt_kernel(z_ref, w1_ref, b1_ref, w2_ref, b2_ref, fr_ref, w3_ref, dec_ref, h_out, ss_out, *, tl):
    i = pl.program_id(0)
    z = z_ref[...]
    h = jnp.sin(fr_ref[0:1, :] * (_dot3(z, w1_ref[...]) + b1_ref[...]))
    h = jnp.sin(fr_ref[1:2, :] * (_dot3(h, w2_ref[...]) + b2_ref[...]))
    h = _dot3(h, w3_ref[...])
    decay = jnp.exp(-z[:, 0:1] * jnp.abs(dec_ref[...]))
    row = i * tl + lax.broadcasted_iota(jnp.int32, (tl, 1), 0)

    @pl.when(i == 0)
    def _():
        ss_out[...] = jnp.zeros_like(ss_out)

    for g in range(2 * HY_ORDER):
        hg = h[:, g * HY_CH:(g + 1) * HY_CH] * decay
        if g % 2 == 1:
            hg = jnp.where(row > 0, hg, 0.0)
        h_out[g] = hg
        ss_out[g] += jnp.sum(hg * hg, axis=0, keepdims=True)


def _hyena_filter_spectrum(n_tok, i, p, plan, mats):
    t = jnp.linspace(0.0, 1.0, n_tok, dtype=F32)[:, None]
    w = 2.0 * math.pi * jnp.arange(n_tok, dtype=F32) / n_tok
    f = jnp.linspace(1e-4, HY_BANDS - 1, HY_BANDS, dtype=F32)
    fw = w[:, None] * f[None, :]
    feat = jnp.concatenate([t, jnp.cos(fw), -jnp.sin(fw)], axis=-1)
    n_feat = feat.shape[1]
    hid = p['hy_w1'].shape[2]
    feat = jnp.pad(feat, ((0, 0), (0, hid - n_feat)))
    w1 = jnp.pad(p['hy_w1'][i], ((0, hid - n_feat), (0, 0)))
    tl = _tile(n_tok, 512)
    groups = 2 * HY_ORDER
    full = lambda a: pl.BlockSpec(a.shape, lambda s: (0,) * a.ndim)
    wts = [w1, p['hy_b1'][i][None], p['hy_w2'][i], p['hy_b2'][i][None], p['hy_sin_freq'][i],
           p['hy_w3'][i], p['hy_decay'][i][None]]
    h, ss = pl.pallas_call(
        functools.partial(_filt_kernel, tl=tl),
        grid=(n_tok // tl,),
        in_specs=[pl.BlockSpec((tl, hid), lambda s: (s, 0))] + [full(a) for a in wts],
        out_specs=[pl.BlockSpec((groups, tl, HY_CH), lambda s: (0, s, 0)),
                   pl.BlockSpec((groups, 1, HY_CH), lambda s: (0, 0, 0))],
        out_shape=[jax.ShapeDtypeStruct((groups, n_tok, HY_CH), F32),
                   jax.ShapeDtypeStruct((groups, 1, HY_CH), F32)],
        compiler_params=_cparams(("arbitrary",)),
        name="hyena_filter_mlp",
    )(feat, *wts)
    j = plan.n2 * HY_CH
    a = _dft1(h.reshape(groups, plan.nh, j), mats['f1_32'], True, _tile(j, 4096))
    yf = _spectral(a.reshape(groups, 2, plan.k1p, plan.n2, HY_CH), plan, mats, None)
    return yf, ss


def _hyena(v, x1, x2, skip, plan, mats, filt):
    bsz, n_tok, ch = v.shape
    j = plan.n2 * ch
    tj = _tile(j, 4096)
    view = lambda a: a.reshape(bsz, plan.nh, j)
    u = view(v)
    a = _dft1(u, mats['f1'], False, tj)
    for o, gate in enumerate((x1, x2)):
        vv = _spectral(a.reshape(bsz, 2, plan.k1p, plan.n2, ch), plan, mats, filt, o)
        sk = jnp.tile(skip[o], plan.n2)[None, :]
        res = _idft1(vv.reshape(bsz, 2 * plan.k1p, j), mats, u, view(gate), sk, o == 0, tj)
        if o == 0:
            u, a = res
        else:
            u = res[0]
    return u.reshape(bsz, n_tok, ch)


def _fft_mats(plan):
    return {'f1': jnp.asarray(plan.f1, BF16), 'f1_32': jnp.asarray(plan.f1, F32),
            'g1': jnp.asarray(plan.g1, BF16), 'mf': jnp.asarray(plan.mf, BF16),
            'mf32': jnp.asarray(plan.mf, F32), 'mi': jnp.asarray(plan.mi, BF16)}


def _cumsum_lanes(x):
    lane = lax.broadcasted_iota(jnp.int32, x.shape, 1)
    s = 1
    while s < x.shape[1]:
        x = x + jnp.where(lane >= s, pltpu.roll(x, s, 1), 0.0)
        s *= 2
    return x


def _topk_kernel(aff_ref, idx_out, gate_out, pos_ref, *, cap, chunk):
    a = aff_ref[0]
    n_e, n_tok = a.shape
    bits = lax.bitcast_convert_type(a, jnp.int32)

    def count(mask):
        return jnp.sum(jnp.where(mask, 1.0, 0.0), axis=1, keepdims=True)

    def bisect(_, lohi):
        lo, hi = lohi
        mid = lo + ((hi - lo) >> 1)
        ok = count(bits >= mid) >= cap
        return jnp.where(ok, mid, lo), jnp.where(ok, hi, mid)

    lo0 = jnp.zeros((n_e, 1), jnp.int32)
    hi0 = jnp.full((n_e, 1), 0x7F800000, jnp.int32)
    thr, _ = lax.fori_loop(0, 31, bisect, (lo0, hi0))
    gt = bits > thr
    eqf = jnp.where(bits == thr, 1.0, 0.0)
    need = cap - count(gt)
    rank = _cumsum_lanes(eqf) - eqf
    self_ = jnp.where(gt, 1.0, jnp.where(rank < need, eqf, 0.0))
    pos_ref[...] = _cumsum_lanes(self_) * self_

    tok = lax.broadcasted_iota(jnp.int32, (2 * SUBLANE, n_tok), 1)
    r = lax.broadcasted_iota(jnp.int32, (2 * SUBLANE, n_tok), 0)
    t_hi = (tok >> 6).astype(F32)
    t_lo = (tok & 63).astype(F32)
    slot = (lax.broadcasted_iota(jnp.int32, (cap, 1), 0) + 1).astype(F32)
    nt = (((1,), (1,)), ((), ()))

    def per_expert(e, carry):
        row = aff_ref[0, pl.ds(e, 1), :]
        a_hi = row.astype(BF16).astype(F32)
        a_mid = (row - a_hi).astype(BF16).astype(F32)
        a_lo = row - a_hi - a_mid
        pay = jnp.where(r == 0, t_hi, jnp.where(r == 1, t_lo, jnp.where(
            r == 2, a_hi, jnp.where(r == 3, a_mid, jnp.where(r == 4, a_lo, 0.0))))).astype(BF16)
        acc = jnp.zeros((2 * SUBLANE, cap), F32)
        for c in range(n_tok // chunk):
            sl = slice(c * chunk, (c + 1) * chunk)
            onehot = jnp.where(pos_ref[pl.ds(e, 1), sl] == slot, 1.0, 0.0).astype(BF16)
            acc = acc + lax.dot_general(pay[:, sl], onehot, nt, preferred_element_type=F32)
        idx_out[0, pl.ds(e, 1), :] = (acc[0:1] * 64.0 + acc[1:2]).astype(jnp.int32)
        gate_out[0, pl.ds(e, 1), :] = acc[2:3] + acc[3:4] + acc[4:5]
        return carry

    lax.fori_loop(0, n_e, per_expert, 0)


def _route(aff_t, cap):
    bsz, n_e, n_tok = aff_t.shape
    blk = lambda w: pl.BlockSpec((1, n_e, w), lambda b: (b, 0, 0))
    return pl.pallas_call(
        functools.partial(_topk_kernel, cap=cap, chunk=_tile(n_tok, 1024)),
        grid=(bsz,), in_specs=[blk(n_tok)], out_specs=[blk(cap), blk(cap)],
        out_shape=[jax.ShapeDtypeStruct((bsz, n_e, cap), jnp.int32),
                   jax.ShapeDtypeStruct((bsz, n_e, cap), F32)],
        scratch_shapes=[pltpu.VMEM((n_e, n_tok), F32)],
        compiler_params=_cparams(("parallel",)),
        name="topk_route",
    )(aff_t)


def kernel(x, c, ctx, c_ctx, mod_w, mod_b, norm1_g, w_in, q_a_g, w_q_b, kv_a_g, w_kv_b, conv_dw_w, conv_dw_b, conv_ln_g, conv_ln_b, hy_short_w, hy_short_b, hy_w1, hy_b1, hy_w2, hy_b2, hy_w3, hy_sin_freq, hy_decay, hy_skip, group_norm_g, w_out, norm2_g, router_w, w_gate, w_up, w_down, final_norm_g):
    p = dict(mod_w=mod_w, mod_b=mod_b, norm1_g=norm1_g, w_in=w_in, q_a_g=q_a_g, w_q_b=w_q_b,
             kv_a_g=kv_a_g, w_kv_b=w_kv_b, conv_dw_w=conv_dw_w, conv_dw_b=conv_dw_b,
             conv_ln_g=conv_ln_g, conv_ln_b=conv_ln_b, hy_short_w=hy_short_w,
             hy_short_b=hy_short_b, hy_w1=hy_w1, hy_b1=hy_b1, hy_w2=hy_w2, hy_b2=hy_b2,
             hy_w3=hy_w3, hy_sin_freq=hy_sin_freq, hy_decay=hy_decay, hy_skip=hy_skip,
             group_norm_g=group_norm_g, w_out=w_out, norm2_g=norm2_g, router_w=router_w,
             w_gate=w_gate, w_up=w_up, w_down=w_down)
    depth = mod_w.shape[0]
    bsz, n_lat, d = x.shape
    n_ctx = ctx.shape[1]

    rows = -(-(bsz + 1) // 8) * 8
    cc = jnp.concatenate([c, c_ctx[None, :], jnp.zeros((rows - bsz - 1, d), F32)], axis=0)
    mod_all = _modulation(cc, mod_w, mod_b)
    lat_row = lambda b: b
    ctx_row = lambda b: bsz

    tabs_l = _rope_tables(n_lat)
    tabs_c = _identity_tables(n_ctx)
    tm_l, tm_c = _tile(n_lat, 512), _tile(n_ctx, 256)
    tq_l, tq_c = _tile(n_lat, 256), _tile(n_ctx, 256)
    tl_l, tl_c = _tile(n_lat, 256), _tile(n_ctx, 256)

    xl, xc = x, ctx
    prev_l = prev_c = None
    for i in range(depth):
        last = i == depth - 1
        lw = _layer_weights(i, p)
        mod = mod_all[i].reshape(rows, 1, N_MOD * d)

        def side(xs, prev, row_of_b, tabs, tm, tq, tl, extra_keys, need_mix):
            xs, q, k, v, uc, uh = _inproj(xs, prev, mod, row_of_b, lw, tabs, tm)
            if not need_mix:
                return xs, (k, v), None
            att = _attention(q, [(k, v)] + extra_keys, tq)
            cn, hv, hx1, hx2 = _conv(uc, uh, lw, tl)
            plan = _fft_plan(xs.shape[1])
            mats = _fft_mats(plan)
            filt = _hyena_filter_spectrum(xs.shape[1], i, p, plan, mats)
            hy = _hyena(hv, hx1, hx2, p['hy_skip'][i], plan, mats, filt)
            x1, h2p, aff_t = _outproj(att, cn, hy, xs, mod, row_of_b, lw, tm)
            cap = CAPACITY_FACTOR * xs.shape[1] // N_EXPERTS
            idx, gates = _route(aff_t, cap)
            moe = _moe(h2p, idx, gates, lw)
            return x1, (k, v), moe

        xc, kv_c, moe_c = side(xc, prev_c, ctx_row, tabs_c, tm_c, tq_c, tl_c, [], not last)
        xl, _, moe_l = side(xl, prev_l, lat_row, tabs_l, tm_l, tq_l, tl_l, [kv_c], True)
        prev_l = (moe_l, mod)
        prev_c = None if moe_c is None else (moe_c, mod)

    mod = mod_all[depth - 1].reshape(rows, 1, N_MOD * d)
    return _final(xl, prev_l[0], mod, final_norm_g.reshape(1, d), tm_l)
```

```python
import functools
import math
from typing import NamedTuple

import numpy as np
import jax
import jax.numpy as jnp
from jax import lax
from jax.experimental import pallas as pl
from jax.experimental.pallas import tpu as pltpu

F32 = jnp.float32
BF16 = jnp.bfloat16
EPS = 1e-6

GRID_W = 64
N_MOD = 6
HEADS = 8
NOPE = 64
ROPE = 32
VDIM = 64
Q_LORA = 256
KV_LORA = 128
ROPE_BASE = 10000.0
CONV_CH = 256
CONV_K = 31
HY_CH = 256
HY_ORDER = 2
HY_BANDS = 16
N_EXPERTS = 16
CAPACITY_FACTOR = 2

LANE = 128
SUBLANE = 8
MXU_DIM = 256
HEAD_PAD = LANE
HALO = 16
VMEM_LIMIT = 56 * 1024 * 1024

C_Q = 0
C_KVN = C_Q + Q_LORA
C_KR = C_KVN + KV_LORA
C_CONV = C_KR + LANE
C_HY = C_CONV + 2 * CONV_CH
C_END = C_HY + 3 * HY_CH


def _cparams(sem):
    return pltpu.CompilerParams(dimension_semantics=sem, vmem_limit_bytes=VMEM_LIMIT)


def _rms(x, g):
    return x * lax.rsqrt(jnp.mean(x * x, axis=-1, keepdims=True) + EPS) * g


def _split(a):
    hi = a.astype(BF16)
    lo = (a - hi.astype(F32)).astype(BF16)
    return hi, lo


def _dot(a, b):
    return jnp.dot(a, b, preferred_element_type=F32)


def _from_token_major(ref, n_rows):
    fc = ref.shape[1] // n_rows
    return jnp.concatenate([ref[0, pl.ds(c, n_rows, stride=fc), :] for c in range(fc)], axis=1)


def _mod_kernel(c_ref, w_ref, b_ref, o_ref):
    c = c_ref[...]
    a = c * jax.nn.sigmoid(c)
    a_hi, a_lo = _split(a)
    w_hi, w_lo = _split(w_ref[0])
    o_ref[0] = _dot(a_hi, w_hi) + _dot(a_lo, w_hi) + _dot(a_hi, w_lo) + b_ref[0]


def _modulation(cc, mod_w, mod_b):
    depth, d, n = mod_w.shape
    rows = cc.shape[0]
    tn = 1536
    return pl.pallas_call(
        _mod_kernel,
        grid=(depth, n // tn),
        in_specs=[
            pl.BlockSpec((rows, d), lambda l, j: (0, 0)),
            pl.BlockSpec((1, d, tn), lambda l, j: (l, 0, j)),
            pl.BlockSpec((1, 1, tn), lambda l, j: (l, 0, j)),
        ],
        out_specs=pl.BlockSpec((1, rows, tn), lambda l, j: (l, 0, j)),
        out_shape=jax.ShapeDtypeStruct((depth, rows, n), F32),
        compiler_params=_cparams(("arbitrary", "arbitrary")),
        name="modulation",
    )(cc, mod_w, mod_b.reshape(depth, 1, n))


def _rope(x, c, s1, s2):
    return x * c + pltpu.roll(x, LANE - 8, 1) * s1 + pltpu.roll(x, 8, 1) * s2


def _inproj_kernel(*refs, fuse_prev, scale):
    if fuse_prev:
        xa_ref, xb_ref, g2_ref = refs[:3]
        refs = refs[3:]
    else:
        xa_ref = refs[0]
        refs = refs[1:]
    (sh_ref, sc_ref, n1g_ref, win_ref, qag_ref, wq_ref, kvg_ref, wk_ref, wv_ref,
     c_ref, s1_ref, s2_ref, vone_ref) = refs[:13]
    outs = refs[13:]
    if fuse_prev:
        x_out, q_out, k_out, v_out, uc_out, uh_out = outs
        x = xa_ref[0] + g2_ref[...] * _from_token_major(xb_ref, xa_ref.shape[1])
        x_out[0] = x
    else:
        q_out, k_out, v_out, uc_out, uh_out = outs
        x = xa_ref[0]
    hn = _rms(x, n1g_ref[...]) * (1.0 + sc_ref[...]) + sh_ref[...]
    u = _dot(hn.astype(BF16), win_ref[...])
    c, s1, s2 = c_ref[...], s1_ref[...], s2_ref[...]

    nq = _rms(u[:, C_Q:C_KVN], qag_ref[...]).astype(BF16)
    q = _dot(nq, wq_ref[...])
    for h in range(HEADS):
        sl = slice(h * HEAD_PAD, (h + 1) * HEAD_PAD)
        q_out[0, :, sl] = (_rope(q[:, sl], c, s1, s2) * scale).astype(BF16)

    kvn = _rms(u[:, C_KVN:C_KR], kvg_ref[...]).astype(BF16)
    kr = _rope(u[:, C_KR:C_CONV], c, s1, s2)
    k = _dot(kvn, wk_ref[...])
    for h in range(HEADS):
        sl = slice(h * HEAD_PAD, (h + 1) * HEAD_PAD)
        k_out[0, :, sl] = (k[:, sl] + kr).astype(BF16)
    vt = lax.dot_general(wv_ref[...], kvn, (((1,), (1,)), ((), ())), preferred_element_type=F32)
    v_out[0] = (vt + vone_ref[...]).astype(BF16)
    uc_out[0] = u[:, C_CONV:C_HY]
    uh_out[0] = u[:, C_HY:C_END]


def _inproj(x, prev, mod, row_of_b, lw, tabs, tm):
    bsz, n_tok, d = x.shape
    fuse_prev = prev is not None
    grid = (bsz, n_tok // tm)
    tok = lambda w: pl.BlockSpec((1, tm, w), lambda b, i: (b, i, 0))
    modspec = lambda k: pl.BlockSpec((None, 1, d), lambda b, i: (row_of_b(b), 0, k))
    full = lambda a: pl.BlockSpec(a.shape, lambda b, i: (0,) * a.ndim)
    tabspec = pl.BlockSpec((tm, LANE), lambda b, i: (i, 0))

    args, specs = [x], [tok(d)]
    if fuse_prev:
        args += [prev[0], prev[1]]
        specs += [pl.BlockSpec((1, tm * (d // LANE), LANE), lambda b, i: (b, i, 0)), modspec(5)]
    args += [mod, mod, lw['norm1_g'], lw['w_in'], lw['q_a_g'], lw['w_q'], lw['kv_a_g'],
             lw['w_k'], lw['w_v'], tabs[0], tabs[1], tabs[2], lw['v_one']]
    specs += [modspec(0), modspec(1), full(lw['norm1_g']), full(lw['w_in']), full(lw['q_a_g']),
              full(lw['w_q']), full(lw['kv_a_g']), full(lw['w_k']), full(lw['w_v']),
              tabspec, tabspec, tabspec, full(lw['v_one'])]
    hp = HEADS * HEAD_PAD
    out_shape, out_specs = [], []
    if fuse_prev:
        out_shape.append(jax.ShapeDtypeStruct((bsz, n_tok, d), F32))
        out_specs.append(tok(d))
    out_shape += [jax.ShapeDtypeStruct((bsz, n_tok, hp), BF16)] * 2
    out_specs += [tok(hp)] * 2
    out_shape += [jax.ShapeDtypeStruct((bsz, hp, n_tok), BF16),
                  jax.ShapeDtypeStruct((bsz, n_tok, 2 * CONV_CH), F32),
                  jax.ShapeDtypeStruct((bsz, n_tok, 3 * HY_CH), F32)]
    out_specs += [pl.BlockSpec((1, hp, tm), lambda b, i: (b, 0, i)),
                  tok(2 * CONV_CH), tok(3 * HY_CH)]
    scale = float((NOPE + ROPE) ** -0.5 * math.log2(math.e))
    res = pl.pallas_call(
        functools.partial(_inproj_kernel, fuse_prev=fuse_prev, scale=scale),
        grid=grid, in_specs=specs, out_specs=out_specs, out_shape=out_shape,
        compiler_params=_cparams(("parallel", "parallel")),
        name="inproj",
    )(*args)
    if not fuse_prev:
        res = [x] + list(res)
    return res


def _attn_kernel(*refs, n_sets, tk, ahead):
    q_ref = refs[0]
    kv = refs[1:1 + 2 * n_sets]
    o_ref = refs[1 + 2 * n_sets]
    nt = (((1,), (1,)), ((), ()))
    sls = [slice(hh * HEAD_PAD, (hh + 1) * HEAD_PAD) for hh in range(2)]
    qs = [q_ref[0, :, sl] for sl in sls]
    items = [(hh, i, c) for i in range(n_sets) for c in range(kv[2 * i].shape[1] // tk)
             for hh in range(2)]

    def scores(item):
        hh, i, c = item
        return lax.dot_general(kv[2 * i][0, c * tk:(c + 1) * tk, sls[hh]], qs[hh], nt,
                               preferred_element_type=F32)

    m, o = [None, None], [None, None]
    pending = [scores(it) for it in items[:ahead]]
    for n, (hh, i, c) in enumerate(items):
        s = pending.pop(0)
        if n + ahead < len(items):
            pending.append(scores(items[n + ahead]))
        mt = jnp.max(s, axis=0, keepdims=True)
        m_new = mt if m[hh] is None else jnp.maximum(m[hh], mt)
        pv = _dot(kv[2 * i + 1][0, sls[hh], c * tk:(c + 1) * tk],
                  jnp.exp2(s - m_new).astype(BF16))
        o[hh] = pv if o[hh] is None else o[hh] * jnp.exp2(m[hh] - m_new) + pv
        m[hh] = m_new
    outs = [(oh * (1.0 / oh[VDIM:VDIM + 1, :])).T for oh in o]
    lane = lax.broadcasted_iota(jnp.int32, outs[0].shape, 1)
    o_ref[0] = jnp.where(lane < VDIM, outs[0], pltpu.roll(outs[1], VDIM, 1))


def _attention(q, key_sets, tq):
    bsz, n_q, hp = q.shape
    grid = (bsz, HEADS // 2, n_q // tq)
    args = [q]
    specs = [pl.BlockSpec((1, tq, 2 * HEAD_PAD), lambda b, h, i: (b, i, h))]
    for k, v in key_sets:
        n_k = k.shape[1]
        args += [k, v]
        specs += [pl.BlockSpec((1, n_k, 2 * HEAD_PAD), lambda b, h, i: (b, 0, h)),
                  pl.BlockSpec((1, 2 * HEAD_PAD, n_k), lambda b, h, i: (b, h, 0))]
    return pl.pallas_call(
        functools.partial(_attn_kernel, n_sets=len(key_sets),
                          tk=min([MXU_DIM] + [k.shape[1] for k, _ in key_sets]), ahead=8),
        grid=grid, in_specs=specs,
        out_specs=pl.BlockSpec((1, tq, 2 * VDIM), lambda b, h, i: (b, i, h)),
        out_shape=jax.ShapeDtypeStruct((bsz, n_q, HEADS * VDIM), F32),
        compiler_params=_cparams(("parallel", "parallel", "arbitrary")),
        name="attention",
    )(*args)


def _conv_kernel(ucp_ref, uc_ref, ucn_ref, uhp_ref, uh_ref, uhn_ref,
                 cw_ref, cb_ref, lg_ref, lb_ref, gn_ref, hw_ref, hb_ref,
                 cn_out, v_out, x1_out, x2_out, ypad, hpad, *, tl, row_tile):
    i = pl.program_id(1)
    has_prev = (i > 0).astype(F32)
    has_next = (i < pl.num_programs(1) - 1).astype(F32)

    def glu(u):
        return u[:, :CONV_CH] * jax.nn.sigmoid(u[:, CONV_CH:])

    ypad[0:HALO, :] = glu(ucp_ref[0]) * has_prev
    ypad[HALO:HALO + tl, :] = glu(uc_ref[0])
    ypad[HALO + tl:, :] = glu(ucn_ref[0]) * has_next
    hpad[0:HALO, :] = uhp_ref[0] * has_prev
    hpad[HALO:HALO + tl, :] = uh_ref[0]
    hpad[HALO + tl:, :] = uhn_ref[0] * has_next

    half = CONV_K // 2
    for r in range(tl // row_tile):
        base = r * row_tile
        acc = jnp.zeros((row_tile, CONV_CH), F32) + cb_ref[...]
        for k in range(CONV_K):
            off = HALO + base + k - half
            acc = acc + cw_ref[k:k + 1, :] * ypad[off:off + row_tile, :]
        mu = jnp.mean(acc, axis=-1, keepdims=True)
        cen = acc - mu
        var = jnp.mean(cen * cen, axis=-1, keepdims=True)
        y = cen * lax.rsqrt(var + EPS) * lg_ref[...] + lb_ref[...]
        y = y * jax.nn.sigmoid(y)
        cn_out[0, base:base + row_tile, :] = _rms(y, gn_ref[...])

        z = jnp.zeros((row_tile, 3 * HY_CH), F32) + hb_ref[...]
        for k in range(3):
            off = HALO + base + k - 1
            z = z + hw_ref[k:k + 1, :] * hpad[off:off + row_tile, :]
        v_out[0, base:base + row_tile, :] = z[:, :HY_CH]
        x1_out[0, base:base + row_tile, :] = z[:, HY_CH:2 * HY_CH]
        x2_out[0, base:base + row_tile, :] = z[:, 2 * HY_CH:]


def _conv(uc, uh, lw, tl):
    bsz, n_tok, _ = uc.shape
    nh = tl // HALO
    last = n_tok // HALO - 1
    cur = lambda w: pl.BlockSpec((1, tl, w), lambda b, i: (b, i, 0))
    prv = lambda w: pl.BlockSpec((1, HALO, w), lambda b, i: (b, jnp.maximum(i * nh - 1, 0), 0))
    nxt = lambda w: pl.BlockSpec((1, HALO, w), lambda b, i: (b, jnp.minimum((i + 1) * nh, last), 0))
    full = lambda a: pl.BlockSpec(a.shape, lambda b, i: (0,) * a.ndim)
    wts = [lw['conv_dw_w'], lw['conv_dw_b'], lw['conv_ln_g'], lw['conv_ln_b'], lw['gn_conv'],
           lw['hy_short_w'], lw['hy_short_b']]
    return pl.pallas_call(
        functools.partial(_conv_kernel, tl=tl, row_tile=min(64, tl)),
        grid=(bsz, n_tok // tl),
        in_specs=[prv(2 * CONV_CH), cur(2 * CONV_CH), nxt(2 * CONV_CH),
                  prv(3 * HY_CH), cur(3 * HY_CH), nxt(3 * HY_CH)] + [full(w) for w in wts],
        out_specs=[cur(CONV_CH)] + [cur(HY_CH)] * 3,
        out_shape=[jax.ShapeDtypeStruct((bsz, n_tok, CONV_CH), F32)]
        + [jax.ShapeDtypeStruct((bsz, n_tok, HY_CH), F32)] * 3,
        scratch_shapes=[pltpu.VMEM((tl + 2 * HALO, CONV_CH), F32),
                        pltpu.VMEM((tl + 2 * HALO, 3 * HY_CH), F32)],
        compiler_params=_cparams(("parallel", "parallel")),
        name="conv",
    )(uc, uc, uc, uh, uh, uh, *wts)


def _outproj_kernel(att_ref, cn_ref, hy_ref, x_ref, g1_ref, sh_ref, sc_ref, gna_ref, gnh_ref,
                    wo_ref, n2g_ref, rw_ref, x1_out, h2p_out, aff_out):
    a = _rms(att_ref[0], gna_ref[...]).astype(BF16)
    c = cn_ref[0].astype(BF16)
    h = _rms(hy_ref[0], gnh_ref[...]).astype(BF16)
    na, nc = a.shape[1], c.shape[1]
    y = (_dot(a, wo_ref[0:na, :]) + _dot(c, wo_ref[na:na + nc, :]) + _dot(h, wo_ref[na + nc:, :]))
    x1 = x_ref[0] + g1_ref[...] * y
    x1_out[0] = x1
    h2 = _rms(x1, n2g_ref[...]) * (1.0 + sc_ref[...]) + sh_ref[...]
    d = h2.shape[1]
    lo = lax.bitcast_convert_type(h2[:, :d // 2].astype(BF16).astype(F32), jnp.uint32)
    hi = lax.bitcast_convert_type(h2[:, d // 2:].astype(BF16).astype(F32), jnp.uint32)
    packed = hi | (lo >> 16)
    n_rows, pc = packed.shape[0], packed.shape[1] // LANE
    for c in range(pc):
        h2p_out[0, pl.ds(c, n_rows, stride=pc), :] = packed[:, c * LANE:(c + 1) * LANE]
    h_hi, h_lo = _split(h2)
    t = _dot(h_hi, rw_ref[...])
    logits = t[:, :LANE] + t[:, LANE:] + _dot(h_lo, rw_ref[:, :LANE])
    lane = lax.broadcasted_iota(jnp.int32, logits.shape, 1)
    logits = jnp.where(lane < N_EXPERTS, logits, -1e30)
    e = jnp.exp(logits - jnp.max(logits, axis=1, keepdims=True))
    aff = e / jnp.sum(e, axis=1, keepdims=True)
    aff_out[0] = aff.T[:N_EXPERTS, :]


def _outproj(att, cn, hy, x, mod, row_of_b, lw, tm):
    bsz, n_tok, d = x.shape
    tok = lambda w: pl.BlockSpec((1, tm, w), lambda b, i: (b, i, 0))
    modspec = lambda k: pl.BlockSpec((None, 1, d), lambda b, i: (row_of_b(b), 0, k))
    full = lambda a: pl.BlockSpec(a.shape, lambda b, i: (0,) * a.ndim)
    wts = [lw['gn_att'], lw['gn_hy'], lw['w_out'], lw['norm2_g'], lw['router_w']]
    pc = d // 2 // LANE
    return pl.pallas_call(
        _outproj_kernel,
        grid=(bsz, n_tok // tm),
        in_specs=[tok(att.shape[2]), tok(cn.shape[2]), tok(hy.shape[2]), tok(d),
                  modspec(2), modspec(3), modspec(4)] + [full(w) for w in wts],
        out_specs=[tok(d), pl.BlockSpec((1, tm * pc, LANE), lambda b, i: (b, i, 0)),
                   pl.BlockSpec((1, N_EXPERTS, tm), lambda b, i: (b, 0, i))],
        out_shape=[jax.ShapeDtypeStruct((bsz, n_tok, d), F32),
                   jax.ShapeDtypeStruct((bsz, n_tok * pc, LANE), jnp.uint32),
                   jax.ShapeDtypeStruct((bsz, N_EXPERTS, n_tok), F32)],
        compiler_params=_cparams(("parallel", "parallel")),
        name="outproj",
    )(att, cn, hy, x, mod, mod, mod, *wts)


def _moe_kernel(idx_ref, gate_ref, h2p_ref, wg_ref, wu_ref, wd_ref, out_hbm,
                acc_ref, xg_ref, y_ref, sem, *, cap, pc, fc):
    b = pl.program_id(0)
    e = pl.program_id(1)

    @pl.when(e == 0)
    def _():
        acc_ref[...] = jnp.zeros_like(acc_ref)

    group = math.gcd(cap, SUBLANE)

    def gather(jg, carry):
        js = [jg * group + u for u in range(group)]
        rows = [h2p_ref[0, pl.ds(pl.multiple_of(idx_ref[0, 0, j] * pc, pc), pc), :] for j in js]
        for j, row in zip(js, rows):
            xg_ref[pl.ds(pl.multiple_of(j * pc, pc), pc), :] = row
        return carry
    lax.fori_loop(0, cap // group, gather, 0)

    lo, hi = [], []
    for c in range(pc):
        w = xg_ref[pl.ds(c, cap, stride=pc), :]
        lo.append(lax.bitcast_convert_type(w << 16, F32).astype(BF16))
        hi.append(lax.bitcast_convert_type(w & jnp.uint32(0xFFFF0000), F32).astype(BF16))
    x = jnp.concatenate(lo + hi, axis=1)
    a = _dot(x, wg_ref[0])
    u = _dot(x, wu_ref[0])
    hmid = (a * jax.nn.sigmoid(a) * u).astype(BF16)
    y = _dot(hmid, wd_ref[0])
    for c in range(fc):
        y_ref[pl.ds(c, cap, stride=fc), :] = y[:, c * LANE:(c + 1) * LANE]

    def scatter(jg, carry):
        js = [jg * group + u for u in range(group)]
        dsts = [pl.ds(pl.multiple_of(idx_ref[0, 0, j] * fc, fc), fc) for j in js]
        new = [acc_ref[dst, :] + gate_ref[0, 0, j] * y_ref[pl.ds(pl.multiple_of(j * fc, fc), fc), :]
               for j, dst in zip(js, dsts)]
        for dst, val in zip(dsts, new):
            acc_ref[dst, :] = val
        return carry
    lax.fori_loop(0, cap // group, scatter, 0)

    @pl.when(e == pl.num_programs(1) - 1)
    def _():
        cp = pltpu.make_async_copy(acc_ref, out_hbm.at[b], sem)
        cp.start()
        cp.wait()


def _moe(h2p, idx, gates, lw):
    d = lw['w_gate'].shape[1]
    pc, fc = d // 2 // LANE, d // LANE
    bsz, n_tok = h2p.shape[0], h2p.shape[1] // pc
    n_e, cap = idx.shape[1], idx.shape[2]
    ff = lw['w_gate'].shape[2]
    smem = lambda: pl.BlockSpec((1, 1, cap), lambda b, e: (b * n_e + e, 0, 0),
                                memory_space=pltpu.SMEM)
    return pl.pallas_call(
        functools.partial(_moe_kernel, cap=cap, pc=pc, fc=fc),
        grid=(bsz, n_e),
        in_specs=[smem(), smem(),
                  pl.BlockSpec((1, n_tok * pc, LANE), lambda b, e: (b, 0, 0)),
                  pl.BlockSpec((1, d, ff), lambda b, e: (e, 0, 0)),
                  pl.BlockSpec((1, d, ff), lambda b, e: (e, 0, 0)),
                  pl.BlockSpec((1, ff, d), lambda b, e: (e, 0, 0))],
        out_specs=pl.BlockSpec(memory_space=pl.ANY),
        out_shape=jax.ShapeDtypeStruct((bsz, n_tok * fc, LANE), F32),
        scratch_shapes=[pltpu.VMEM((n_tok * fc, LANE), F32),
                        pltpu.VMEM((cap * pc, LANE), jnp.uint32),
                        pltpu.VMEM((cap * fc, LANE), F32),
                        pltpu.SemaphoreType.DMA(())],
        compiler_params=_cparams(("arbitrary", "arbitrary")),
        name="moe",
    )(idx.reshape(bsz * n_e, 1, cap), gates.reshape(bsz * n_e, 1, cap),
      h2p, lw['w_gate'], lw['w_up'], lw['w_down'])


def _final_kernel(x_ref, m_ref, g2_ref, fg_ref, o_ref):
    o_ref[0] = _rms(x_ref[0] + g2_ref[...] * _from_token_major(m_ref, x_ref.shape[1]), fg_ref[...])


def _final(x1, moe, mod, fg, tm):
    bsz, n_tok, d = x1.shape
    tok = pl.BlockSpec((1, tm, d), lambda b, i: (b, i, 0))
    return pl.pallas_call(
        _final_kernel,
        grid=(bsz, n_tok // tm),
        in_specs=[tok, pl.BlockSpec((1, tm * (d // LANE), LANE), lambda b, i: (b, i, 0)),
                  pl.BlockSpec((None, 1, d), lambda b, i: (b, 0, 5)),
                  pl.BlockSpec((1, d), lambda b, i: (0, 0))],
        out_specs=tok,
        out_shape=jax.ShapeDtypeStruct((bsz, n_tok, d), F32),
        compiler_params=_cparams(("parallel", "parallel")),
        name="final_norm",
    )(x1, moe, mod, fg)


def _rope_tables(n_tok):
    rows = n_tok // GRID_W
    row = jnp.repeat(jnp.arange(rows, dtype=F32), GRID_W)
    col = jnp.tile(jnp.arange(GRID_W, dtype=F32), rows)
    half = ROPE // 2
    inv = ROPE_BASE ** (-jnp.arange(0, half, 2, dtype=F32) / half)
    cr, sr = jnp.cos(row[:, None] * inv), jnp.sin(row[:, None] * inv)
    cc, sc = jnp.cos(col[:, None] * inv), jnp.sin(col[:, None] * inv)
    z8 = jnp.zeros_like(cr)
    ones = jnp.ones((n_tok, NOPE), F32)
    pad = jnp.zeros((n_tok, HEAD_PAD - NOPE - ROPE), F32)
    c = jnp.concatenate([ones, cr, cr, cc, cc, pad + 1.0], axis=1)
    s1 = jnp.concatenate([ones * 0.0, -sr, z8, -sc, z8, pad], axis=1)
    s2 = jnp.concatenate([ones * 0.0, z8, sr, z8, sc, pad], axis=1)
    return c, s1, s2


def _identity_tables(n_tok):
    return (jnp.ones((n_tok, HEAD_PAD), F32), jnp.zeros((n_tok, HEAD_PAD), F32),
            jnp.zeros((n_tok, HEAD_PAD), F32))


def _pad_heads(w, per_head, take):
    k = w.shape[0]
    w = w.reshape(k, HEADS, per_head)[:, :, take]
    w = jnp.pad(w, ((0, 0), (0, 0), (0, HEAD_PAD - w.shape[2])))
    return w.reshape(k, HEADS * HEAD_PAD).astype(BF16)


def _layer_weights(i, p):
    d = p['w_in'].shape[1]
    w_in = p['w_in'][i]
    off_kv = Q_LORA
    off_conv = off_kv + KV_LORA + ROPE
    off_hy = off_conv + 2 * CONV_CH
    kr = jnp.zeros((d, LANE), F32).at[:, NOPE:NOPE + ROPE].set(w_in[:, off_kv + KV_LORA:off_conv])
    w_in_r = jnp.concatenate([w_in[:, :off_kv], w_in[:, off_kv:off_kv + KV_LORA], kr,
                              w_in[:, off_conv:off_hy], w_in[:, off_hy:]], axis=1).astype(BF16)
    gn = p['group_norm_g'][i]
    n_att = HEADS * VDIM
    rw_hi, rw_lo = _split(jnp.pad(p['router_w'][i], ((0, 0), (0, LANE - N_EXPERTS))))
    v_one = jnp.zeros((HEADS, HEAD_PAD), F32).at[:, VDIM].set(1.0).reshape(HEADS * HEAD_PAD, 1)
    row = lambda a: a.reshape(1, -1)
    return {
        'norm1_g': row(p['norm1_g'][i]), 'w_in': w_in_r,
        'q_a_g': row(p['q_a_g'][i]), 'w_q': _pad_heads(p['w_q_b'][i], NOPE + ROPE, slice(None)),
        'kv_a_g': row(p['kv_a_g'][i]),
        'w_k': _pad_heads(p['w_kv_b'][i], NOPE + VDIM, slice(0, NOPE)),
        'w_v': _pad_heads(p['w_kv_b'][i], NOPE + VDIM, slice(NOPE, NOPE + VDIM)).T,
        'v_one': v_one,
        'conv_dw_w': p['conv_dw_w'][i], 'conv_dw_b': row(p['conv_dw_b'][i]),
        'conv_ln_g': row(p['conv_ln_g'][i]), 'conv_ln_b': row(p['conv_ln_b'][i]),
        'gn_att': row(gn[:n_att]), 'gn_conv': row(gn[n_att:n_att + CONV_CH]),
        'gn_hy': row(gn[n_att + CONV_CH:]),
        'hy_short_w': p['hy_short_w'][i], 'hy_short_b': row(p['hy_short_b'][i]),
        'w_out': p['w_out'][i].astype(BF16), 'norm2_g': row(p['norm2_g'][i]),
        'router_w': jnp.concatenate([rw_hi, rw_lo], axis=1),
        'w_gate': p['w_gate'][i].astype(BF16), 'w_up': p['w_up'][i].astype(BF16),
        'w_down': p['w_down'][i].astype(BF16),
    }


def _tile(n, pref):
    return pref if n % pref == 0 else n


def _dot3(a, b):
    a_hi, a_lo = _split(a)
    b_hi, b_lo = _split(b)
    return _dot(a_hi, b_hi) + _dot(a_lo, b_hi) + _dot(a_hi, b_lo)


class _FftPlan(NamedTuple):
    n2: int
    nh: int
    k1p: int
    f1: np.ndarray
    g1: np.ndarray
    mf: np.ndarray
    mi: np.ndarray


@functools.lru_cache(maxsize=None)
def _fft_plan(n_tok):
    n = 2 * n_tok
    n2 = 64 if n_tok >= 2048 else 16
    n1 = n // n2
    nh = n1 // 2
    k1 = nh + 1
    k1p = -(-k1 // SUBLANE) * SUBLANE
    two_pi = 2.0 * np.pi
    r = np.arange(k1)
    ang1 = two_pi * ((np.arange(nh)[None, :] * r[:, None]) % n1) / n1
    f1 = np.zeros((2 * k1p, nh))
    f1[:k1], f1[k1p:k1p + k1] = np.cos(ang1), -np.sin(ang1)
    w = np.where((r == 0) | (r == nh), 1.0, 2.0)[None, :] / n
    g1 = np.zeros((nh, 2 * k1p))
    g1[:, :k1], g1[:, k1p:k1p + k1] = np.cos(ang1.T) * w, -np.sin(ang1.T) * w
    k = r[:, None, None] + n1 * np.arange(n2)[None, :, None]
    th = two_pi * ((k * np.arange(n2)[None, None, :]) % n) / n
    tc, ts = np.cos(th), -np.sin(th)
    mf = np.zeros((k1p, 2 * n2, 2 * n2))
    mi = np.zeros((k1p, 2 * n2, 2 * n2))
    mf[:k1, :n2, :n2], mf[:k1, :n2, n2:], mf[:k1, n2:, :n2], mf[:k1, n2:, n2:] = tc, -ts, ts, tc
    tct, tst = tc.transpose(0, 2, 1), ts.transpose(0, 2, 1)
    mi[:k1, :n2, :n2], mi[:k1, :n2, n2:], mi[:k1, n2:, :n2], mi[:k1, n2:, n2:] = tct, tst, -tst, tct
    return _FftPlan(n2, nh, k1p, f1, g1, mf, mi)


def _stage1(x_ref, lead, f_ref, a_ref, nh, n2, rows, mm):
    def body(j, carry):
        x = x_ref[lead, pl.ds(j, nh, stride=n2), :]
        a_ref[pl.ds(pl.multiple_of(j * rows, SUBLANE), rows), :] = mm(f_ref[...], x)
        return carry
    lax.fori_loop(0, n2, body, 0, unroll=8)


def _spectrum_rows(a_ref, k, k1p, n2, rows):
    return jnp.concatenate([a_ref[pl.ds(k, n2, stride=rows), :],
                            a_ref[pl.ds(k1p + k, n2, stride=rows), :]], axis=0)


def _hyena_conv_kernel(u_ref, gate_ref, skip_ref, f1_ref, g1_ref, mf_ref, mi_ref, g_ref, y_out, a_ref,
                       *, nh, n2, k1p):
    rows = 2 * k1p
    bdot = lambda f, x: _dot(f, x.astype(BF16))
    _stage1(u_ref, 0, f1_ref, a_ref, nh, n2, rows, bdot)

    def stage2(k, carry):
        y = _dot(mf_ref[k], _spectrum_rows(a_ref, k, k1p, n2, rows).astype(BF16))
        yr, yi = y[:n2], y[n2:]
        gr, gi = g_ref[0, 0, k], g_ref[0, 1, k]
        z = jnp.concatenate([yr * gr - yi * gi, yr * gi + yi * gr], axis=0).astype(BF16)
        v = _dot(mi_ref[k], z)
        a_ref[pl.ds(k, n2, stride=rows), :] = v[:n2]
        a_ref[pl.ds(k1p + k, n2, stride=rows), :] = v[n2:]
        return carry
    lax.fori_loop(0, k1p, stage2, 0, unroll=8)

    def inverse1(j, carry):
        v = a_ref[pl.ds(pl.multiple_of(j * rows, SUBLANE), rows), :].astype(BF16)
        sl = pl.ds(j, nh, stride=n2)
        y_out[0, sl, :] = gate_ref[0, sl, :] * (_dot(g1_ref[...], v) + u_ref[0, sl, :] * skip_ref[...])
        return carry
    lax.fori_loop(0, n2, inverse1, 0, unroll=8)


def _hyena_conv(u, gate, skip, g, order, plan, mats):
    bsz, n_tok, ch = u.shape
    nh, n2, k1p = plan.nh, plan.n2, plan.k1p
    rows = 2 * k1p
    tok = pl.BlockSpec((1, n_tok, LANE), lambda ci, b: (b, 0, ci))
    const = lambda a: pl.BlockSpec(a.shape, lambda ci, b: (0,) * a.ndim)
    return pl.pallas_call(
        functools.partial(_hyena_conv_kernel, nh=nh, n2=n2, k1p=k1p),
        grid=(ch // LANE, bsz),
        in_specs=[tok, tok, pl.BlockSpec((None, 1, LANE), lambda ci, b: (order, 0, ci)),
                  const(mats['f1']), const(mats['g1']), const(mats['mf']), const(mats['mi']),
                  pl.BlockSpec((1, 2, k1p, n2, LANE), lambda ci, b: (order, 0, 0, 0, ci))],
        out_specs=tok,
        out_shape=jax.ShapeDtypeStruct((bsz, n_tok, ch), F32),
        scratch_shapes=[pltpu.VMEM((n2 * rows, LANE), F32)],
        compiler_params=_cparams(("parallel", "parallel")),
        name="hyena_conv",
    )(u, gate, skip.reshape(skip.shape[0], 1, ch), mats['f1'], mats['g1'], mats['mf'], mats['mi'], g)


def _filter_spectrum_kernel(h_ref, ss_ref, f1_ref, mf_ref, g_out, a_ref, *, nh, n2, k1p):
    rows = 2 * k1p
    s = lax.rsqrt(ss_ref[0] + ss_ref[1] + EPS)
    for direction in range(2):
        _stage1(h_ref, direction, f1_ref, a_ref, nh, n2, rows, _dot3)

        def stage2(k, carry):
            y = _dot3(mf_ref[k], _spectrum_rows(a_ref, k, k1p, n2, rows))
            if direction == 0:
                g_out[0, 0, k] = y[:n2] * s
                g_out[0, 1, k] = y[n2:] * s
            else:
                g_out[0, 0, k] = g_out[0, 0, k] + y[:n2] * s
                g_out[0, 1, k] = g_out[0, 1, k] - y[n2:] * s
            return carry
        lax.fori_loop(0, k1p, stage2, 0)


def _filter_spectrum(h, ss, plan, mats):
    groups, n_tok, ch = h.shape
    nh, n2, k1p = plan.nh, plan.n2, plan.k1p
    full = lambda a: pl.BlockSpec(a.shape, lambda o, ci: (0,) * a.ndim)
    return pl.pallas_call(
        functools.partial(_filter_spectrum_kernel, nh=nh, n2=n2, k1p=k1p),
        grid=(groups // 2, ch // LANE),
        in_specs=[pl.BlockSpec((2, n_tok, LANE), lambda o, ci: (o, 0, ci)),
                  pl.BlockSpec((2, 1, LANE), lambda o, ci: (o, 0, ci)),
                  full(mats['f1_32']), full(mats['mf32'])],
        out_specs=pl.BlockSpec((1, 2, k1p, n2, LANE), lambda o, ci: (o, 0, 0, 0, ci)),
        out_shape=jax.ShapeDtypeStruct((groups // 2, 2, k1p, n2, ch), F32),
        scratch_shapes=[pltpu.VMEM((n2 * 2 * k1p, LANE), F32)],
        compiler_params=_cparams(("arbitrary", "arbitrary")),
        name="hyena_filter_spectrum",
    )(h, ss, mats['f1_32'], mats['mf32'])


def _filt_kernel(z_ref, w1_ref, b1_ref, w2_ref, b2_ref, fr_ref, w3_ref, dec_ref, h_out, ss_out, *, tl):
    i = pl.program_id(0)
    z = z_ref[...]
    h = jnp.sin(fr_ref[0:1, :] * (_dot3(z, w1_ref[...]) + b1_ref[...]))
    h = jnp.sin(fr_ref[1:2, :] * (_dot3(h, w2_ref[...]) + b2_ref[...]))
    h = _dot3(h, w3_ref[...])
    decay = jnp.exp(-z[:, 0:1] * jnp.abs(dec_ref[...]))
    row = i * tl + lax.broadcasted_iota(jnp.int32, (tl, 1), 0)

    @pl.when(i == 0)
    def _():
        ss_out[...] = jnp.zeros_like(ss_out)

    for g in range(2 * HY_ORDER):
        hg = h[:, g * HY_CH:(g + 1) * HY_CH] * decay
        if g % 2 == 1:
            hg = jnp.where(row > 0, hg, 0.0)
        h_out[g] = hg
        ss_out[g] += jnp.sum(hg * hg, axis=0, keepdims=True)


def _hyena_filter_spectrum(n_tok, i, p, plan, mats):
    t = jnp.linspace(0.0, 1.0, n_tok, dtype=F32)[:, None]
    w = 2.0 * math.pi * jnp.arange(n_tok, dtype=F32) / n_tok
    f = jnp.linspace(1e-4, HY_BANDS - 1, HY_BANDS, dtype=F32)
    fw = w[:, None] * f[None, :]
    feat = jnp.concatenate([t, jnp.cos(fw), -jnp.sin(fw)], axis=-1)
    n_feat = feat.shape[1]
    hid = p['hy_w1'].shape[2]
    feat = jnp.pad(feat, ((0, 0), (0, hid - n_feat)))
    w1 = jnp.pad(p['hy_w1'][i], ((0, hid - n_feat), (0, 0)))
    tl = _tile(n_tok, 512)
    groups = 2 * HY_ORDER
    full = lambda a: pl.BlockSpec(a.shape, lambda s: (0,) * a.ndim)
    wts = [w1, p['hy_b1'][i][None], p['hy_w2'][i], p['hy_b2'][i][None], p['hy_sin_freq'][i],
           p['hy_w3'][i], p['hy_decay'][i][None]]
    h, ss = pl.pallas_call(
        functools.partial(_filt_kernel, tl=tl),
        grid=(n_tok // tl,),
        in_specs=[pl.BlockSpec((tl, hid), lambda s: (s, 0))] + [full(a) for a in wts],
        out_specs=[pl.BlockSpec((groups, tl, HY_CH), lambda s: (0, s, 0)),
                   pl.BlockSpec((groups, 1, HY_CH), lambda s: (0, 0, 0))],
        out_shape=[jax.ShapeDtypeStruct((groups, n_tok, HY_CH), F32),
                   jax.ShapeDtypeStruct((groups, 1, HY_CH), F32)],
        compiler_params=_cparams(("arbitrary",)),
        name="hyena_filter_mlp",
    )(feat, *wts)
    return _filter_spectrum(h, ss, plan, mats)


def _hyena(v, x1, x2, skip, plan, mats, g):
    y = v
    for o, gate in enumerate((x1, x2)):
        y = _hyena_conv(y, gate, skip, g, o, plan, mats)
    return y


def _fft_mats(plan):
    return {'f1': jnp.asarray(plan.f1, BF16), 'f1_32': jnp.asarray(plan.f1, F32),
            'g1': jnp.asarray(plan.g1, BF16), 'mf': jnp.asarray(plan.mf, BF16),
            'mf32': jnp.asarray(plan.mf, F32), 'mi': jnp.asarray(plan.mi, BF16)}


def _cumsum_lanes(x):
    lane = lax.broadcasted_iota(jnp.int32, x.shape, 1)
    s = 1
    while s < x.shape[1]:
        x = x + jnp.where(lane >= s, pltpu.roll(x, s, 1), 0.0)
        s *= 2
    return x


def _topk_kernel(aff_ref, idx_out, gate_out, pos_ref, *, cap, chunk):
    a = aff_ref[0]
    n_e, n_tok = a.shape
    bits = lax.bitcast_convert_type(a, jnp.int32)

    def count(mask):
        return jnp.sum(jnp.where(mask, 1.0, 0.0), axis=1, keepdims=True)

    def bisect(_, lohi):
        lo, hi = lohi
        mid = lo + ((hi - lo) >> 1)
        ok = count(bits >= mid) >= cap
        return jnp.where(ok, mid, lo), jnp.where(ok, hi, mid)

    lo0 = jnp.zeros((n_e, 1), jnp.int32)
    hi0 = jnp.full((n_e, 1), 0x7F800000, jnp.int32)
    thr, _ = lax.fori_loop(0, 31, bisect, (lo0, hi0))
    gt = bits > thr
    eqf = jnp.where(bits == thr, 1.0, 0.0)
    need = cap - count(gt)
    rank = _cumsum_lanes(eqf) - eqf
    self_ = jnp.where(gt, 1.0, jnp.where(rank < need, eqf, 0.0))
    pos_ref[...] = _cumsum_lanes(self_) * self_

    tok = lax.broadcasted_iota(jnp.int32, (2 * SUBLANE, n_tok), 1)
    r = lax.broadcasted_iota(jnp.int32, (2 * SUBLANE, n_tok), 0)
    t_hi = (tok >> 6).astype(F32)
    t_lo = (tok & 63).astype(F32)
    slot = (lax.broadcasted_iota(jnp.int32, (cap, 1), 0) + 1).astype(F32)
    nt = (((1,), (1,)), ((), ()))

    def per_expert(e, carry):
        row = aff_ref[0, pl.ds(e, 1), :]
        a_hi = row.astype(BF16).astype(F32)
        a_mid = (row - a_hi).astype(BF16).astype(F32)
        a_lo = row - a_hi - a_mid
        pay = jnp.where(r == 0, t_hi, jnp.where(r == 1, t_lo, jnp.where(
            r == 2, a_hi, jnp.where(r == 3, a_mid, jnp.where(r == 4, a_lo, 0.0))))).astype(BF16)
        acc = jnp.zeros((2 * SUBLANE, cap), F32)
        for c in range(n_tok // chunk):
            sl = slice(c * chunk, (c + 1) * chunk)
            onehot = jnp.where(pos_ref[pl.ds(e, 1), sl] == slot, 1.0, 0.0).astype(BF16)
            acc = acc + lax.dot_general(pay[:, sl], onehot, nt, preferred_element_type=F32)
        idx_out[0, pl.ds(e, 1), :] = (acc[0:1] * 64.0 + acc[1:2]).astype(jnp.int32)
        gate_out[0, pl.ds(e, 1), :] = acc[2:3] + acc[3:4] + acc[4:5]
        return carry

    lax.fori_loop(0, n_e, per_expert, 0)


def _route(aff_t, cap):
    bsz, n_e, n_tok = aff_t.shape
    blk = lambda w: pl.BlockSpec((1, n_e, w), lambda b: (b, 0, 0))
    return pl.pallas_call(
        functools.partial(_topk_kernel, cap=cap, chunk=_tile(n_tok, 1024)),
        grid=(bsz,), in_specs=[blk(n_tok)], out_specs=[blk(cap), blk(cap)],
        out_shape=[jax.ShapeDtypeStruct((bsz, n_e, cap), jnp.int32),
                   jax.ShapeDtypeStruct((bsz, n_e, cap), F32)],
        scratch_shapes=[pltpu.VMEM((n_e, n_tok), F32)],
        compiler_params=_cparams(("parallel",)),
        name="topk_route",
    )(aff_t)


def kernel(x, c, ctx, c_ctx, mod_w, mod_b, norm1_g, w_in, q_a_g, w_q_b, kv_a_g, w_kv_b, conv_dw_w, conv_dw_b, conv_ln_g, conv_ln_b, hy_short_w, hy_short_b, hy_w1, hy_b1, hy_w2, hy_b2, hy_w3, hy_sin_freq, hy_decay, hy_skip, group_norm_g, w_out, norm2_g, router_w, w_gate, w_up, w_down, final_norm_g):
    p = dict(mod_w=mod_w, mod_b=mod_b, norm1_g=norm1_g, w_in=w_in, q_a_g=q_a_g, w_q_b=w_q_b,
             kv_a_g=kv_a_g, w_kv_b=w_kv_b, conv_dw_w=conv_dw_w, conv_dw_b=conv_dw_b,
             conv_ln_g=conv_ln_g, conv_ln_b=conv_ln_b, hy_short_w=hy_short_w,
             hy_short_b=hy_short_b, hy_w1=hy_w1, hy_b1=hy_b1, hy_w2=hy_w2, hy_b2=hy_b2,
             hy_w3=hy_w3, hy_sin_freq=hy_sin_freq, hy_decay=hy_decay, hy_skip=hy_skip,
             group_norm_g=group_norm_g, w_out=w_out, norm2_g=norm2_g, router_w=router_w,
             w_gate=w_gate, w_up=w_up, w_down=w_down)
    depth = mod_w.shape[0]
    bsz, n_lat, d = x.shape
    n_ctx = ctx.shape[1]

    rows = -(-(bsz + 1) // 8) * 8
    cc = jnp.concatenate([c, c_ctx[None, :], jnp.zeros((rows - bsz - 1, d), F32)], axis=0)
    mod_all = _modulation(cc, mod_w, mod_b)
    lat_row = lambda b: b
    ctx_row = lambda b: bsz

    tabs_l = _rope_tables(n_lat)
    tabs_c = _identity_tables(n_ctx)
    tm_l, tm_c = _tile(n_lat, 512), _tile(n_ctx, 256)
    tq_l, tq_c = _tile(n_lat, 256), _tile(n_ctx, 256)
    tl_l, tl_c = _tile(n_lat, 256), _tile(n_ctx, 256)

    xl, xc = x, ctx
    prev_l = prev_c = None
    for i in range(depth):
        last = i == depth - 1
        lw = _layer_weights(i, p)
        mod = mod_all[i].reshape(rows, 1, N_MOD * d)

        def side(xs, prev, row_of_b, tabs, tm, tq, tl, extra_keys, need_mix):
            xs, q, k, v, uc, uh = _inproj(xs, prev, mod, row_of_b, lw, tabs, tm)
            if not need_mix:
                return xs, (k, v), None
            att = _attention(q, [(k, v)] + extra_keys, tq)
            cn, hv, hx1, hx2 = _conv(uc, uh, lw, tl)
            plan = _fft_plan(xs.shape[1])
            mats = _fft_mats(plan)
            filt = _hyena_filter_spectrum(xs.shape[1], i, p, plan, mats)
            hy = _hyena(hv, hx1, hx2, p['hy_skip'][i], plan, mats, filt)
            x1, h2p, aff_t = _outproj(att, cn, hy, xs, mod, row_of_b, lw, tm)
            cap = CAPACITY_FACTOR * xs.shape[1] // N_EXPERTS
            idx, gates = _route(aff_t, cap)
            moe = _moe(h2p, idx, gates, lw)
            return x1, (k, v), moe

        xc, kv_c, moe_c = side(xc, prev_c, ctx_row, tabs_c, tm_c, tq_c, tl_c, [], not last)
        xl, _, moe_l = side(xl, prev_l, lat_row, tabs_l, tm_l, tq_l, tl_l, [kv_c], True)
        prev_l = (moe_l, mod)
        prev_c = None if moe_c is None else (moe_c, mod)

    mod = mod_all[depth - 1].reshape(rows, 1, N_MOD * d)
    return _final(xl, prev_l[0], mod, final_norm_g.reshape(1, d), tm_l)
```

```python
import functools
import math
from typing import NamedTuple

import numpy as np
import jax
import jax.numpy as jnp
from jax import lax
from jax.experimental import pallas as pl
from jax.experimental.pallas import tpu as pltpu

F32 = jnp.float32
BF16 = jnp.bfloat16
EPS = 1e-6

GRID_W = 64
N_MOD = 6
HEADS = 8
NOPE = 64
ROPE = 32
VDIM = 64
Q_LORA = 256
KV_LORA = 128
ROPE_BASE = 10000.0
CONV_CH = 256
CONV_K = 31
HY_CH = 256
HY_ORDER = 2
HY_BANDS = 16
N_EXPERTS = 16
CAPACITY_FACTOR = 2

LANE = 128
SUBLANE = 8
MXU_DIM = 256
HEAD_PAD = LANE
BF16_ROWS = 16
V_ROWS = -(-(VDIM + 1) // BF16_ROWS) * BF16_ROWS
HALO = 16
VMEM_LIMIT = 56 * 1024 * 1024

C_Q = 0
C_KVN = C_Q + Q_LORA
C_KR = C_KVN + KV_LORA
C_CONV = C_KR + LANE
C_HY = C_CONV + 2 * CONV_CH
C_END = C_HY + 3 * HY_CH


def _cparams(sem):
    return pltpu.CompilerParams(dimension_semantics=sem, vmem_limit_bytes=VMEM_LIMIT)


def _rms(x, g):
    return x * lax.rsqrt(jnp.mean(x * x, axis=-1, keepdims=True) + EPS) * g


def _split(a):
    hi = a.astype(BF16)
    lo = (a - hi.astype(F32)).astype(BF16)
    return hi, lo


def _dot(a, b):
    return jnp.dot(a, b, preferred_element_type=F32)


def _from_token_major(ref, n_rows):
    fc = ref.shape[1] // n_rows
    return jnp.concatenate([ref[0, pl.ds(c, n_rows, stride=fc), :] for c in range(fc)], axis=1)


def _mod_kernel(c_ref, w_ref, b_ref, o_ref):
    c = c_ref[...]
    a = c * jax.nn.sigmoid(c)
    a_hi, a_lo = _split(a)
    w_hi, w_lo = _split(w_ref[0])
    o_ref[0] = _dot(a_hi, w_hi) + _dot(a_lo, w_hi) + _dot(a_hi, w_lo) + b_ref[0]


def _modulation(cc, mod_w, mod_b):
    depth, d, n = mod_w.shape
    rows = cc.shape[0]
    tn = 1536
    return pl.pallas_call(
        _mod_kernel,
        grid=(depth, n // tn),
        in_specs=[
            pl.BlockSpec((rows, d), lambda l, j: (0, 0)),
            pl.BlockSpec((1, d, tn), lambda l, j: (l, 0, j)),
            pl.BlockSpec((1, 1, tn), lambda l, j: (l, 0, j)),
        ],
        out_specs=pl.BlockSpec((1, rows, tn), lambda l, j: (l, 0, j)),
        out_shape=jax.ShapeDtypeStruct((depth, rows, n), F32),
        compiler_params=_cparams(("arbitrary", "arbitrary")),
        name="modulation",
    )(cc, mod_w, mod_b.reshape(depth, 1, n))


def _rope(x, c, s1, s2):
    return x * c + pltpu.roll(x, LANE - 8, 1) * s1 + pltpu.roll(x, 8, 1) * s2


def _inproj_kernel(*refs, fuse_prev, scale):
    if fuse_prev:
        xa_ref, xb_ref, g2_ref = refs[:3]
        refs = refs[3:]
    else:
        xa_ref = refs[0]
        refs = refs[1:]
    (sh_ref, sc_ref, n1g_ref, win_ref, qag_ref, wq_ref, kvg_ref, wk_ref, wv_ref,
     c_ref, s1_ref, s2_ref, vone_ref) = refs[:13]
    outs = refs[13:]
    if fuse_prev:
        x_out, q_out, k_out, v_out, uc_out, uh_out = outs
        x = xa_ref[0] + g2_ref[...] * _from_token_major(xb_ref, xa_ref.shape[1])
        x_out[0] = x
    else:
        q_out, k_out, v_out, uc_out, uh_out = outs
        x = xa_ref[0]
    hn = _rms(x, n1g_ref[...]) * (1.0 + sc_ref[...]) + sh_ref[...]
    u = _dot(hn.astype(BF16), win_ref[...])
    c, s1, s2 = c_ref[...], s1_ref[...], s2_ref[...]

    nq = _rms(u[:, C_Q:C_KVN], qag_ref[...]).astype(BF16)
    q = _dot(nq, wq_ref[...])
    for h in range(HEADS):
        sl = slice(h * HEAD_PAD, (h + 1) * HEAD_PAD)
        q_out[0, :, sl] = (_rope(q[:, sl], c, s1, s2) * scale).astype(BF16)

    kvn = _rms(u[:, C_KVN:C_KR], kvg_ref[...]).astype(BF16)
    kr = _rope(u[:, C_KR:C_CONV], c, s1, s2)
    k = _dot(kvn, wk_ref[...])
    for h in range(HEADS):
        sl = slice(h * HEAD_PAD, (h + 1) * HEAD_PAD)
        k_out[0, :, sl] = (k[:, sl] + kr).astype(BF16)
    vt = lax.dot_general(wv_ref[...], kvn, (((1,), (1,)), ((), ())), preferred_element_type=F32)
    v_out[0] = (vt + vone_ref[...]).astype(BF16)
    uc_out[0] = u[:, C_CONV:C_HY]
    uh_out[0] = u[:, C_HY:C_END]


def _inproj(x, prev, mod, row_of_b, lw, tabs, tm):
    bsz, n_tok, d = x.shape
    fuse_prev = prev is not None
    grid = (bsz, n_tok // tm)
    tok = lambda w: pl.BlockSpec((1, tm, w), lambda b, i: (b, i, 0))
    modspec = lambda k: pl.BlockSpec((None, 1, d), lambda b, i: (row_of_b(b), 0, k))
    full = lambda a: pl.BlockSpec(a.shape, lambda b, i: (0,) * a.ndim)
    tabspec = pl.BlockSpec((tm, LANE), lambda b, i: (i, 0))

    args, specs = [x], [tok(d)]
    if fuse_prev:
        args += [prev[0], prev[1]]
        specs += [pl.BlockSpec((1, tm * (d // LANE), LANE), lambda b, i: (b, i, 0)), modspec(5)]
    args += [mod, mod, lw['norm1_g'], lw['w_in'], lw['q_a_g'], lw['w_q'], lw['kv_a_g'],
             lw['w_k'], lw['w_v'], tabs[0], tabs[1], tabs[2], lw['v_one']]
    specs += [modspec(0), modspec(1), full(lw['norm1_g']), full(lw['w_in']), full(lw['q_a_g']),
              full(lw['w_q']), full(lw['kv_a_g']), full(lw['w_k']), full(lw['w_v']),
              tabspec, tabspec, tabspec, full(lw['v_one'])]
    hp = HEADS * HEAD_PAD
    out_shape, out_specs = [], []
    if fuse_prev:
        out_shape.append(jax.ShapeDtypeStruct((bsz, n_tok, d), F32))
        out_specs.append(tok(d))
    out_shape += [jax.ShapeDtypeStruct((bsz, n_tok, hp), BF16)] * 2
    out_specs += [tok(hp)] * 2
    out_shape += [jax.ShapeDtypeStruct((bsz, hp, n_tok), BF16),
                  jax.ShapeDtypeStruct((bsz, n_tok, 2 * CONV_CH), F32),
                  jax.ShapeDtypeStruct((bsz, n_tok, 3 * HY_CH), F32)]
    out_specs += [pl.BlockSpec((1, hp, tm), lambda b, i: (b, 0, i)),
                  tok(2 * CONV_CH), tok(3 * HY_CH)]
    scale = float((NOPE + ROPE) ** -0.5 * math.log2(math.e))
    res = pl.pallas_call(
        functools.partial(_inproj_kernel, fuse_prev=fuse_prev, scale=scale),
        grid=grid, in_specs=specs, out_specs=out_specs, out_shape=out_shape,
        compiler_params=_cparams(("parallel", "parallel")),
        name="inproj",
    )(*args)
    if not fuse_prev:
        res = [x] + list(res)
    return res


def _attn_kernel(*refs, n_sets, tk, ahead):
    q_ref = refs[0]
    kv = refs[1:1 + 2 * n_sets]
    o_ref = refs[1 + 2 * n_sets]
    nt = (((1,), (1,)), ((), ()))
    sls = [slice(hh * HEAD_PAD, (hh + 1) * HEAD_PAD) for hh in range(2)]
    qs = [q_ref[0, :, sl] for sl in sls]
    items = [(hh, i, c) for i in range(n_sets) for c in range(kv[2 * i].shape[1] // tk)
             for hh in range(2)]

    def scores(item):
        hh, i, c = item
        return lax.dot_general(kv[2 * i][0, c * tk:(c + 1) * tk, sls[hh]], qs[hh], nt,
                               preferred_element_type=F32)

    m, o = [None, None], [None, None]
    pending = [scores(it) for it in items[:ahead]]
    for n, (hh, i, c) in enumerate(items):
        s = pending.pop(0)
        if n + ahead < len(items):
            pending.append(scores(items[n + ahead]))
        mt = jnp.max(s, axis=0, keepdims=True)
        m_new = mt if m[hh] is None else jnp.maximum(m[hh], mt)
        pv = _dot(kv[2 * i + 1][0, hh * HEAD_PAD:hh * HEAD_PAD + V_ROWS, c * tk:(c + 1) * tk],
                  jnp.exp2(s - m_new).astype(BF16))
        o[hh] = pv if o[hh] is None else o[hh] * jnp.exp2(m[hh] - m_new) + pv
        m[hh] = m_new
    pad = jnp.zeros((HEAD_PAD - V_ROWS, o[0].shape[1]), F32)
    outs = [jnp.concatenate([oh * (1.0 / oh[VDIM:VDIM + 1, :]), pad], axis=0).T for oh in o]
    lane = lax.broadcasted_iota(jnp.int32, outs[0].shape, 1)
    o_ref[0] = jnp.where(lane < VDIM, outs[0], pltpu.roll(outs[1], VDIM, 1))


def _attention(q, key_sets, tq):
    bsz, n_q, hp = q.shape
    grid = (bsz, HEADS // 2, n_q // tq)
    args = [q]
    specs = [pl.BlockSpec((1, tq, 2 * HEAD_PAD), lambda b, h, i: (b, i, h))]
    for k, v in key_sets:
        n_k = k.shape[1]
        args += [k, v]
        specs += [pl.BlockSpec((1, n_k, 2 * HEAD_PAD), lambda b, h, i: (b, 0, h)),
                  pl.BlockSpec((1, 2 * HEAD_PAD, n_k), lambda b, h, i: (b, h, 0))]
    return pl.pallas_call(
        functools.partial(_attn_kernel, n_sets=len(key_sets),
                          tk=min([MXU_DIM] + [k.shape[1] for k, _ in key_sets]), ahead=8),
        grid=grid, in_specs=specs,
        out_specs=pl.BlockSpec((1, tq, 2 * VDIM), lambda b, h, i: (b, i, h)),
        out_shape=jax.ShapeDtypeStruct((bsz, n_q, HEADS * VDIM), F32),
        compiler_params=_cparams(("parallel", "parallel", "arbitrary")),
        name="attention",
    )(*args)


def _conv_kernel(ucp_ref, uc_ref, ucn_ref, uhp_ref, uh_ref, uhn_ref,
                 cw_ref, cb_ref, lg_ref, lb_ref, gn_ref, hw_ref, hb_ref,
                 cn_out, v_out, x1_out, x2_out, ypad, hpad, yph, *, tl, row_tile):
    i = pl.program_id(1)
    has_prev = (i > 0).astype(F32)
    has_next = (i < pl.num_programs(1) - 1).astype(F32)

    def glu(u):
        return u[:, :CONV_CH] * jax.nn.sigmoid(u[:, CONV_CH:])

    ypad[0:HALO, :] = glu(ucp_ref[0]) * has_prev
    ypad[HALO:HALO + tl, :] = glu(uc_ref[0])
    ypad[HALO + tl:, :] = glu(ucn_ref[0]) * has_next
    hpad[0:HALO, :] = uhp_ref[0] * has_prev
    hpad[HALO:HALO + tl, :] = uh_ref[0]
    hpad[HALO + tl:, :] = uhn_ref[0] * has_next

    span = tl + 2 * HALO - SUBLANE
    for ph in range(SUBLANE):
        yph[ph] = ypad[ph:ph + span, :]

    half = CONV_K // 2
    for r in range(tl // row_tile):
        base = r * row_tile
        acc = jnp.zeros((row_tile, CONV_CH), F32) + cb_ref[...]
        for k in range(CONV_K):
            off = HALO + base + k - half
            ph = off % SUBLANE
            acc = acc + cw_ref[k:k + 1, :] * yph[ph, off - ph:off - ph + row_tile, :]
        mu = jnp.mean(acc, axis=-1, keepdims=True)
        cen = acc - mu
        var = jnp.mean(cen * cen, axis=-1, keepdims=True)
        y = cen * lax.rsqrt(var + EPS) * lg_ref[...] + lb_ref[...]
        y = y * jax.nn.sigmoid(y)
        cn_out[0, base:base + row_tile, :] = _rms(y, gn_ref[...])

        z = jnp.zeros((row_tile, 3 * HY_CH), F32) + hb_ref[...]
        for k in range(3):
            off = HALO + base + k - 1
            z = z + hw_ref[k:k + 1, :] * hpad[off:off + row_tile, :]
        v_out[0, base:base + row_tile, :] = z[:, :HY_CH]
        x1_out[0, base:base + row_tile, :] = z[:, HY_CH:2 * HY_CH]
        x2_out[0, base:base + row_tile, :] = z[:, 2 * HY_CH:]


def _conv(uc, uh, lw, tl):
    bsz, n_tok, _ = uc.shape
    nh = tl // HALO
    last = n_tok // HALO - 1
    cur = lambda w: pl.BlockSpec((1, tl, w), lambda b, i: (b, i, 0))
    prv = lambda w: pl.BlockSpec((1, HALO, w), lambda b, i: (b, jnp.maximum(i * nh - 1, 0), 0))
    nxt = lambda w: pl.BlockSpec((1, HALO, w), lambda b, i: (b, jnp.minimum((i + 1) * nh, last), 0))
    full = lambda a: pl.BlockSpec(a.shape, lambda b, i: (0,) * a.ndim)
    wts = [lw['conv_dw_w'], lw['conv_dw_b'], lw['conv_ln_g'], lw['conv_ln_b'], lw['gn_conv'],
           lw['hy_short_w'], lw['hy_short_b']]
    return pl.pallas_call(
        functools.partial(_conv_kernel, tl=tl, row_tile=min(64, tl)),
        grid=(bsz, n_tok // tl),
        in_specs=[prv(2 * CONV_CH), cur(2 * CONV_CH), nxt(2 * CONV_CH),
                  prv(3 * HY_CH), cur(3 * HY_CH), nxt(3 * HY_CH)] + [full(w) for w in wts],
        out_specs=[cur(CONV_CH)] + [cur(HY_CH)] * 3,
        out_shape=[jax.ShapeDtypeStruct((bsz, n_tok, CONV_CH), F32)]
        + [jax.ShapeDtypeStruct((bsz, n_tok, HY_CH), F32)] * 3,
        scratch_shapes=[pltpu.VMEM((tl + 2 * HALO, CONV_CH), F32),
                        pltpu.VMEM((tl + 2 * HALO, 3 * HY_CH), F32),
                        pltpu.VMEM((SUBLANE, tl + 2 * HALO - SUBLANE, CONV_CH), F32)],
        compiler_params=_cparams(("parallel", "parallel")),
        name="conv",
    )(uc, uc, uc, uh, uh, uh, *wts)


def _outproj_kernel(att_ref, cn_ref, hy_ref, x_ref, g1_ref, sh_ref, sc_ref, gna_ref, gnh_ref,
                    wo_ref, n2g_ref, rw_ref, x1_out, h2p_out, aff_out):
    a = _rms(att_ref[0], gna_ref[...]).astype(BF16)
    c = cn_ref[0].astype(BF16)
    h = _rms(hy_ref[0], gnh_ref[...]).astype(BF16)
    na, nc = a.shape[1], c.shape[1]
    y = (_dot(a, wo_ref[0:na, :]) + _dot(c, wo_ref[na:na + nc, :]) + _dot(h, wo_ref[na + nc:, :]))
    x1 = x_ref[0] + g1_ref[...] * y
    x1_out[0] = x1
    h2 = _rms(x1, n2g_ref[...]) * (1.0 + sc_ref[...]) + sh_ref[...]
    d = h2.shape[1]
    lo = lax.bitcast_convert_type(h2[:, :d // 2].astype(BF16).astype(F32), jnp.uint32)
    hi = lax.bitcast_convert_type(h2[:, d // 2:].astype(BF16).astype(F32), jnp.uint32)
    packed = hi | (lo >> 16)
    n_rows, pc = packed.shape[0], packed.shape[1] // LANE
    for c in range(pc):
        h2p_out[0, pl.ds(c, n_rows, stride=pc), :] = packed[:, c * LANE:(c + 1) * LANE]
    h_hi, h_lo = _split(h2)
    t = _dot(h_hi, rw_ref[...])
    logits = t[:, :LANE] + t[:, LANE:] + _dot(h_lo, rw_ref[:, :LANE])
    lane = lax.broadcasted_iota(jnp.int32, logits.shape, 1)
    logits = jnp.where(lane < N_EXPERTS, logits, -1e30)
    e = jnp.exp(logits - jnp.max(logits, axis=1, keepdims=True))
    aff = e / jnp.sum(e, axis=1, keepdims=True)
    aff_out[0] = aff.T[:N_EXPERTS, :]


def _outproj(att, cn, hy, x, mod, row_of_b, lw, tm):
    bsz, n_tok, d = x.shape
    tok = lambda w: pl.BlockSpec((1, tm, w), lambda b, i: (b, i, 0))
    modspec = lambda k: pl.BlockSpec((None, 1, d), lambda b, i: (row_of_b(b), 0, k))
    full = lambda a: pl.BlockSpec(a.shape, lambda b, i: (0,) * a.ndim)
    wts = [lw['gn_att'], lw['gn_hy'], lw['w_out'], lw['norm2_g'], lw['router_w']]
    pc = d // 2 // LANE
    return pl.pallas_call(
        _outproj_kernel,
        grid=(bsz, n_tok // tm),
        in_specs=[tok(att.shape[2]), tok(cn.shape[2]), tok(hy.shape[2]), tok(d),
                  modspec(2), modspec(3), modspec(4)] + [full(w) for w in wts],
        out_specs=[tok(d), pl.BlockSpec((1, tm * pc, LANE), lambda b, i: (b, i, 0)),
                   pl.BlockSpec((1, N_EXPERTS, tm), lambda b, i: (b, 0, i))],
        out_shape=[jax.ShapeDtypeStruct((bsz, n_tok, d), F32),
                   jax.ShapeDtypeStruct((bsz, n_tok * pc, LANE), jnp.uint32),
                   jax.ShapeDtypeStruct((bsz, N_EXPERTS, n_tok), F32)],
        compiler_params=_cparams(("parallel", "parallel")),
        name="outproj",
    )(att, cn, hy, x, mod, mod, mod, *wts)


def _moe_kernel(idx_ref, gate_ref, h2p_ref, wg_ref, wu_ref, wd_ref, out_hbm,
                acc_ref, xg_ref, y_ref, sem, *, cap, pc, fc):
    b = pl.program_id(0)
    e = pl.program_id(1)

    @pl.when(e == 0)
    def _():
        acc_ref[...] = jnp.zeros_like(acc_ref)

    group = math.gcd(cap, SUBLANE)

    def gather(jg, carry):
        js = [jg * group + u for u in range(group)]
        rows = [h2p_ref[0, pl.ds(pl.multiple_of(idx_ref[0, 0, j] * pc, pc), pc), :] for j in js]
        for j, row in zip(js, rows):
            xg_ref[pl.ds(pl.multiple_of(j * pc, pc), pc), :] = row
        return carry
    lax.fori_loop(0, cap // group, gather, 0)

    lo, hi = [], []
    for c in range(pc):
        w = xg_ref[pl.ds(c, cap, stride=pc), :]
        lo.append(lax.bitcast_convert_type(w << 16, F32).astype(BF16))
        hi.append(lax.bitcast_convert_type(w & jnp.uint32(0xFFFF0000), F32).astype(BF16))
    x = jnp.concatenate(lo + hi, axis=1)
    a = _dot(x, wg_ref[0])
    u = _dot(x, wu_ref[0])
    hmid = (a * jax.nn.sigmoid(a) * u).astype(BF16)
    y = _dot(hmid, wd_ref[0])
    for c in range(fc):
        y_ref[pl.ds(c, cap, stride=fc), :] = y[:, c * LANE:(c + 1) * LANE]

    def scatter(jg, carry):
        js = [jg * group + u for u in range(group)]
        dsts = [pl.ds(pl.multiple_of(idx_ref[0, 0, j] * fc, fc), fc) for j in js]
        new = [acc_ref[dst, :] + gate_ref[0, 0, j] * y_ref[pl.ds(pl.multiple_of(j * fc, fc), fc), :]
               for j, dst in zip(js, dsts)]
        for dst, val in zip(dsts, new):
            acc_ref[dst, :] = val
        return carry
    lax.fori_loop(0, cap // group, scatter, 0)

    @pl.when(e == pl.num_programs(1) - 1)
    def _():
        cp = pltpu.make_async_copy(acc_ref, out_hbm.at[b], sem)
        cp.start()
        cp.wait()


def _moe(h2p, idx, gates, lw):
    d = lw['w_gate'].shape[1]
    pc, fc = d // 2 // LANE, d // LANE
    bsz, n_tok = h2p.shape[0], h2p.shape[1] // pc
    n_e, cap = idx.shape[1], idx.shape[2]
    ff = lw['w_gate'].shape[2]
    smem = lambda: pl.BlockSpec((1, 1, cap), lambda b, e: (b * n_e + e, 0, 0),
                                memory_space=pltpu.SMEM)
    return pl.pallas_call(
        functools.partial(_moe_kernel, cap=cap, pc=pc, fc=fc),
        grid=(bsz, n_e),
        in_specs=[smem(), smem(),
                  pl.BlockSpec((1, n_tok * pc, LANE), lambda b, e: (b, 0, 0)),
                  pl.BlockSpec((1, d, ff), lambda b, e: (e, 0, 0)),
                  pl.BlockSpec((1, d, ff), lambda b, e: (e, 0, 0)),
                  pl.BlockSpec((1, ff, d), lambda b, e: (e, 0, 0))],
        out_specs=pl.BlockSpec(memory_space=pl.ANY),
        out_shape=jax.ShapeDtypeStruct((bsz, n_tok * fc, LANE), F32),
        scratch_shapes=[pltpu.VMEM((n_tok * fc, LANE), F32),
                        pltpu.VMEM((cap * pc, LANE), jnp.uint32),
                        pltpu.VMEM((cap * fc, LANE), F32),
                        pltpu.SemaphoreType.DMA(())],
        compiler_params=_cparams(("arbitrary", "arbitrary")),
        name="moe",
    )(idx.reshape(bsz * n_e, 1, cap), gates.reshape(bsz * n_e, 1, cap),
      h2p, lw['w_gate'], lw['w_up'], lw['w_down'])


def _final_kernel(x_ref, m_ref, g2_ref, fg_ref, o_ref):
    o_ref[0] = _rms(x_ref[0] + g2_ref[...] * _from_token_major(m_ref, x_ref.shape[1]), fg_ref[...])


def _final(x1, moe, mod, fg, tm):
    bsz, n_tok, d = x1.shape
    tok = pl.BlockSpec((1, tm, d), lambda b, i: (b, i, 0))
    return pl.pallas_call(
        _final_kernel,
        grid=(bsz, n_tok // tm),
        in_specs=[tok, pl.BlockSpec((1, tm * (d // LANE), LANE), lambda b, i: (b, i, 0)),
                  pl.BlockSpec((None, 1, d), lambda b, i: (b, 0, 5)),
                  pl.BlockSpec((1, d), lambda b, i: (0, 0))],
        out_specs=tok,
        out_shape=jax.ShapeDtypeStruct((bsz, n_tok, d), F32),
        compiler_params=_cparams(("parallel", "parallel")),
        name="final_norm",
    )(x1, moe, mod, fg)


def _rope_tables(n_tok):
    rows = n_tok // GRID_W
    row = jnp.repeat(jnp.arange(rows, dtype=F32), GRID_W)
    col = jnp.tile(jnp.arange(GRID_W, dtype=F32), rows)
    half = ROPE // 2
    inv = ROPE_BASE ** (-jnp.arange(0, half, 2, dtype=F32) / half)
    cr, sr = jnp.cos(row[:, None] * inv), jnp.sin(row[:, None] * inv)
    cc, sc = jnp.cos(col[:, None] * inv), jnp.sin(col[:, None] * inv)
    z8 = jnp.zeros_like(cr)
    ones = jnp.ones((n_tok, NOPE), F32)
    pad = jnp.zeros((n_tok, HEAD_PAD - NOPE - ROPE), F32)
    c = jnp.concatenate([ones, cr, cr, cc, cc, pad + 1.0], axis=1)
    s1 = jnp.concatenate([ones * 0.0, -sr, z8, -sc, z8, pad], axis=1)
    s2 = jnp.concatenate([ones * 0.0, z8, sr, z8, sc, pad], axis=1)
    return c, s1, s2


def _identity_tables(n_tok):
    return (jnp.ones((n_tok, HEAD_PAD), F32), jnp.zeros((n_tok, HEAD_PAD), F32),
            jnp.zeros((n_tok, HEAD_PAD), F32))


def _pad_heads(w, per_head, take):
    k = w.shape[0]
    w = w.reshape(k, HEADS, per_head)[:, :, take]
    w = jnp.pad(w, ((0, 0), (0, 0), (0, HEAD_PAD - w.shape[2])))
    return w.reshape(k, HEADS * HEAD_PAD).astype(BF16)


def _layer_weights(i, p):
    d = p['w_in'].shape[1]
    w_in = p['w_in'][i]
    off_kv = Q_LORA
    off_conv = off_kv + KV_LORA + ROPE
    off_hy = off_conv + 2 * CONV_CH
    kr = jnp.zeros((d, LANE), F32).at[:, NOPE:NOPE + ROPE].set(w_in[:, off_kv + KV_LORA:off_conv])
    w_in_r = jnp.concatenate([w_in[:, :off_kv], w_in[:, off_kv:off_kv + KV_LORA], kr,
                              w_in[:, off_conv:off_hy], w_in[:, off_hy:]], axis=1).astype(BF16)
    gn = p['group_norm_g'][i]
    n_att = HEADS * VDIM
    rw_hi, rw_lo = _split(jnp.pad(p['router_w'][i], ((0, 0), (0, LANE - N_EXPERTS))))
    v_one = jnp.zeros((HEADS, HEAD_PAD), F32).at[:, VDIM].set(1.0).reshape(HEADS * HEAD_PAD, 1)
    row = lambda a: a.reshape(1, -1)
    return {
        'norm1_g': row(p['norm1_g'][i]), 'w_in': w_in_r,
        'q_a_g': row(p['q_a_g'][i]), 'w_q': _pad_heads(p['w_q_b'][i], NOPE + ROPE, slice(None)),
        'kv_a_g': row(p['kv_a_g'][i]),
        'w_k': _pad_heads(p['w_kv_b'][i], NOPE + VDIM, slice(0, NOPE)),
        'w_v': _pad_heads(p['w_kv_b'][i], NOPE + VDIM, slice(NOPE, NOPE + VDIM)).T,
        'v_one': v_one,
        'conv_dw_w': p['conv_dw_w'][i], 'conv_dw_b': row(p['conv_dw_b'][i]),
        'conv_ln_g': row(p['conv_ln_g'][i]), 'conv_ln_b': row(p['conv_ln_b'][i]),
        'gn_att': row(gn[:n_att]), 'gn_conv': row(gn[n_att:n_att + CONV_CH]),
        'gn_hy': row(gn[n_att + CONV_CH:]),
        'hy_short_w': p['hy_short_w'][i], 'hy_short_b': row(p['hy_short_b'][i]),
        'w_out': p['w_out'][i].astype(BF16), 'norm2_g': row(p['norm2_g'][i]),
        'router_w': jnp.concatenate([rw_hi, rw_lo], axis=1),
        'w_gate': p['w_gate'][i].astype(BF16), 'w_up': p['w_up'][i].astype(BF16),
        'w_down': p['w_down'][i].astype(BF16),
    }


def _tile(n, pref):
    return pref if n % pref == 0 else n


def _dot3(a, b):
    a_hi, a_lo = _split(a)
    b_hi, b_lo = _split(b)
    return _dot(a_hi, b_hi) + _dot(a_lo, b_hi) + _dot(a_hi, b_lo)


class _FftPlan(NamedTuple):
    n2: int
    nh: int
    k1p: int
    f1: np.ndarray
    g1: np.ndarray
    mf: np.ndarray
    mi: np.ndarray


@functools.lru_cache(maxsize=None)
def _fft_plan(n_tok):
    n = 2 * n_tok
    n2 = 64 if n_tok >= 2048 else 16
    n1 = n // n2
    nh = n1 // 2
    k1 = nh + 1
    k1p = -(-k1 // SUBLANE) * SUBLANE
    two_pi = 2.0 * np.pi
    r = np.arange(k1)
    ang1 = two_pi * ((np.arange(nh)[None, :] * r[:, None]) % n1) / n1
    f1 = np.zeros((2 * k1p, nh))
    f1[:k1], f1[k1p:k1p + k1] = np.cos(ang1), -np.sin(ang1)
    w = np.where((r == 0) | (r == nh), 1.0, 2.0)[None, :] / n
    g1 = np.zeros((nh, 2 * k1p))
    g1[:, :k1], g1[:, k1p:k1p + k1] = np.cos(ang1.T) * w, -np.sin(ang1.T) * w
    k = r[:, None, None] + n1 * np.arange(n2)[None, :, None]
    th = two_pi * ((k * np.arange(n2)[None, None, :]) % n) / n
    tc, ts = np.cos(th), -np.sin(th)
    mf = np.zeros((k1p, 2 * n2, 2 * n2))
    mi = np.zeros((k1p, 2 * n2, 2 * n2))
    mf[:k1, :n2, :n2], mf[:k1, :n2, n2:], mf[:k1, n2:, :n2], mf[:k1, n2:, n2:] = tc, -ts, ts, tc
    tct, tst = tc.transpose(0, 2, 1), ts.transpose(0, 2, 1)
    mi[:k1, :n2, :n2], mi[:k1, :n2, n2:], mi[:k1, n2:, :n2], mi[:k1, n2:, n2:] = tct, tst, -tst, tct
    return _FftPlan(n2, nh, k1p, f1, g1, mf, mi)


def _stage1(x_ref, lead, f_ref, a_ref, nh, n2, rows, mm):
    def body(j, carry):
        x = x_ref[lead, pl.ds(j, nh, stride=n2), :]
        a_ref[pl.ds(pl.multiple_of(j * rows, SUBLANE), rows), :] = mm(f_ref[...], x)
        return carry
    lax.fori_loop(0, n2, body, 0, unroll=8)


def _spectrum_rows(a_ref, k, k1p, n2, rows):
    return jnp.concatenate([a_ref[pl.ds(k, n2, stride=rows), :],
                            a_ref[pl.ds(k1p + k, n2, stride=rows), :]], axis=0)


def _hyena_conv_kernel(u_ref, gate_ref, skip_ref, f1_ref, g1_ref, mf_ref, mi_ref, g_ref, y_out, a_ref,
                       *, nh, n2, k1p):
    rows = 2 * k1p
    bdot = lambda f, x: _dot(f, x.astype(BF16))
    _stage1(u_ref, 0, f1_ref, a_ref, nh, n2, rows, bdot)

    def stage2(k, carry):
        y = _dot(mf_ref[k], _spectrum_rows(a_ref, k, k1p, n2, rows).astype(BF16))
        yr, yi = y[:n2], y[n2:]
        gr, gi = g_ref[0, 0, k], g_ref[0, 1, k]
        z = jnp.concatenate([yr * gr - yi * gi, yr * gi + yi * gr], axis=0).astype(BF16)
        v = _dot(mi_ref[k], z)
        a_ref[pl.ds(k, n2, stride=rows), :] = v[:n2]
        a_ref[pl.ds(k1p + k, n2, stride=rows), :] = v[n2:]
        return carry
    lax.fori_loop(0, k1p, stage2, 0, unroll=8)

    def inverse1(j, carry):
        v = a_ref[pl.ds(pl.multiple_of(j * rows, SUBLANE), rows), :].astype(BF16)
        sl = pl.ds(j, nh, stride=n2)
        y_out[0, sl, :] = gate_ref[0, sl, :] * (_dot(g1_ref[...], v) + u_ref[0, sl, :] * skip_ref[...])
        return carry
    lax.fori_loop(0, n2, inverse1, 0, unroll=8)


def _hyena_conv(u, gate, skip, g, order, plan, mats):
    bsz, n_tok, ch = u.shape
    nh, n2, k1p = plan.nh, plan.n2, plan.k1p
    rows = 2 * k1p
    tok = pl.BlockSpec((1, n_tok, LANE), lambda ci, b: (b, 0, ci))
    const = lambda a: pl.BlockSpec(a.shape, lambda ci, b: (0,) * a.ndim)
    return pl.pallas_call(
        functools.partial(_hyena_conv_kernel, nh=nh, n2=n2, k1p=k1p),
        grid=(ch // LANE, bsz),
        in_specs=[tok, tok, pl.BlockSpec((None, 1, LANE), lambda ci, b: (order, 0, ci)),
                  const(mats['f1']), const(mats['g1']), const(mats['mf']), const(mats['mi']),
                  pl.BlockSpec((1, 2, k1p, n2, LANE), lambda ci, b: (order, 0, 0, 0, ci))],
        out_specs=tok,
        out_shape=jax.ShapeDtypeStruct((bsz, n_tok, ch), F32),
        scratch_shapes=[pltpu.VMEM((n2 * rows, LANE), F32)],
        compiler_params=_cparams(("parallel", "parallel")),
        name="hyena_conv",
    )(u, gate, skip.reshape(skip.shape[0], 1, ch), mats['f1'], mats['g1'], mats['mf'], mats['mi'], g)


def _filter_spectrum_kernel(h_ref, ss_ref, f1_ref, mf_ref, g_out, a_ref, *, nh, n2, k1p):
    rows = 2 * k1p
    s = lax.rsqrt(ss_ref[0] + ss_ref[1] + EPS)
    for direction in range(2):
        _stage1(h_ref, direction, f1_ref, a_ref, nh, n2, rows, _dot3)

        def stage2(k, carry):
            y = _dot3(mf_ref[k], _spectrum_rows(a_ref, k, k1p, n2, rows))
            if direction == 0:
                g_out[0, 0, k] = y[:n2] * s
                g_out[0, 1, k] = y[n2:] * s
            else:
                g_out[0, 0, k] = g_out[0, 0, k] + y[:n2] * s
                g_out[0, 1, k] = g_out[0, 1, k] - y[n2:] * s
            return carry
        lax.fori_loop(0, k1p, stage2, 0)


def _filter_spectrum(h, ss, plan, mats):
    groups, n_tok, ch = h.shape
    nh, n2, k1p = plan.nh, plan.n2, plan.k1p
    full = lambda a: pl.BlockSpec(a.shape, lambda o, ci: (0,) * a.ndim)
    return pl.pallas_call(
        functools.partial(_filter_spectrum_kernel, nh=nh, n2=n2, k1p=k1p),
        grid=(groups // 2, ch // LANE),
        in_specs=[pl.BlockSpec((2, n_tok, LANE), lambda o, ci: (o, 0, ci)),
                  pl.BlockSpec((2, 1, LANE), lambda o, ci: (o, 0, ci)),
                  full(mats['f1_32']), full(mats['mf32'])],
        out_specs=pl.BlockSpec((1, 2, k1p, n2, LANE), lambda o, ci: (o, 0, 0, 0, ci)),
        out_shape=jax.ShapeDtypeStruct((groups // 2, 2, k1p, n2, ch), F32),
        scratch_shapes=[pltpu.VMEM((n2 * 2 * k1p, LANE), F32)],
        compiler_params=_cparams(("arbitrary", "arbitrary")),
        name="hyena_filter_spectrum",
    )(h, ss, mats['f1_32'], mats['mf32'])


def _filt_kernel(z_ref, w1_ref, b1_ref, w2_ref, b2_ref, fr_ref, w3_ref, dec_ref, h_out, ss_out, *, tl):
    i = pl.program_id(0)
    z = z_ref[...]
    h = jnp.sin(fr_ref[0:1, :] * (_dot3(z, w1_ref[...]) + b1_ref[...]))
    h = jnp.sin(fr_ref[1:2, :] * (_dot3(h, w2_ref[...]) + b2_ref[...]))
    h = _dot3(h, w3_ref[...])
    decay = jnp.exp(-z[:, 0:1] * jnp.abs(dec_ref[...]))
    row = i * tl + lax.broadcasted_iota(jnp.int32, (tl, 1), 0)

    @pl.when(i == 0)
    def _():
        ss_out[...] = jnp.zeros_like(ss_out)

    for g in range(2 * HY_ORDER):
        hg = h[:, g * HY_CH:(g + 1) * HY_CH] * decay
        if g % 2 == 1:
            hg = jnp.where(row > 0, hg, 0.0)
        h_out[g] = hg
        ss_out[g] += jnp.sum(hg * hg, axis=0, keepdims=True)


def _hyena_filter_spectrum(n_tok, i, p, plan, mats):
    t = jnp.linspace(0.0, 1.0, n_tok, dtype=F32)[:, None]
    w = 2.0 * math.pi * jnp.arange(n_tok, dtype=F32) / n_tok
    f = jnp.linspace(1e-4, HY_BANDS - 1, HY_BANDS, dtype=F32)
    fw = w[:, None] * f[None, :]
    feat = jnp.concatenate([t, jnp.cos(fw), -jnp.sin(fw)], axis=-1)
    n_feat = feat.shape[1]
    hid = p['hy_w1'].shape[2]
    feat = jnp.pad(feat, ((0, 0), (0, hid - n_feat)))
    w1 = jnp.pad(p['hy_w1'][i], ((0, hid - n_feat), (0, 0)))
    tl = _tile(n_tok, 512)
    groups = 2 * HY_ORDER
    full = lambda a: pl.BlockSpec(a.shape, lambda s: (0,) * a.ndim)
    wts = [w1, p['hy_b1'][i][None], p['hy_w2'][i], p['hy_b2'][i][None], p['hy_sin_freq'][i],
           p['hy_w3'][i], p['hy_decay'][i][None]]
    h, ss = pl.pallas_call(
        functools.partial(_filt_kernel, tl=tl),
        grid=(n_tok // tl,),
        in_specs=[pl.BlockSpec((tl, hid), lambda s: (s, 0))] + [full(a) for a in wts],
        out_specs=[pl.BlockSpec((groups, tl, HY_CH), lambda s: (0, s, 0)),
                   pl.BlockSpec((groups, 1, HY_CH), lambda s: (0, 0, 0))],
        out_shape=[jax.ShapeDtypeStruct((groups, n_tok, HY_CH), F32),
                   jax.ShapeDtypeStruct((groups, 1, HY_CH), F32)],
        compiler_params=_cparams(("arbitrary",)),
        name="hyena_filter_mlp",
    )(feat, *wts)
    return _filter_spectrum(h, ss, plan, mats)


def _hyena(v, x1, x2, skip, plan, mats, g):
    y = v
    for o, gate in enumerate((x1, x2)):
        y = _hyena_conv(y, gate, skip, g, o, plan, mats)
    return y


def _fft_mats(plan):
    return {'f1': jnp.asarray(plan.f1, BF16), 'f1_32': jnp.asarray(plan.f1, F32),
            'g1': jnp.asarray(plan.g1, BF16), 'mf': jnp.asarray(plan.mf, BF16),
            'mf32': jnp.asarray(plan.mf, F32), 'mi': jnp.asarray(plan.mi, BF16)}


def _cumsum_lanes(x):
    lane = lax.broadcasted_iota(jnp.int32, x.shape, 1)
    s = 1
    while s < x.shape[1]:
        x = x + jnp.where(lane >= s, pltpu.roll(x, s, 1), 0.0)
        s *= 2
    return x


def _topk_kernel(aff_ref, idx_out, gate_out, pos_ref, *, cap, chunk, nb):
    a = aff_ref[0]
    n_e, n_tok = a.shape
    bits = lax.bitcast_convert_type(a, jnp.int32)

    def count(mask):
        return jnp.sum(jnp.where(mask, 1.0, 0.0), axis=1, keepdims=True)

    def bisect(_, lohi):
        lo, hi = lohi
        mid = lo + ((hi - lo) >> 1)
        ok = count(bits >= mid) >= cap
        return jnp.where(ok, mid, lo), jnp.where(ok, hi, mid)

    lo0 = jnp.zeros((n_e, 1), jnp.int32)
    hi0 = jnp.full((n_e, 1), 0x7F800000, jnp.int32)
    thr, _ = lax.fori_loop(0, 31, bisect, (lo0, hi0))
    gt = bits > thr
    eqf = jnp.where(bits == thr, 1.0, 0.0)
    need = cap - count(gt)
    rank = _cumsum_lanes(eqf) - eqf
    self_ = jnp.where(gt, 1.0, jnp.where(rank < need, eqf, 0.0))
    pos_ref[...] = _cumsum_lanes(self_) * self_

    na = cap // nb
    tok = lax.broadcasted_iota(jnp.int32, (1, n_tok), 1)
    t_hi = (tok >> 6).astype(F32)
    t_lo = (tok & 63).astype(F32)
    row_a = lax.broadcasted_iota(jnp.int32, (na, 1), 0)
    row_b = lax.broadcasted_iota(jnp.int32, (nb, 1), 0)
    nt = (((1,), (1,)), ((), ()))

    def per_expert(e, carry):
        aff = aff_ref[0, pl.ds(e, 1), :]
        a_hi = aff.astype(BF16).astype(F32)
        a_mid = (aff - a_hi).astype(BF16).astype(F32)
        a_lo = aff - a_hi - a_mid
        slot = pos_ref[pl.ds(e, 1), :].astype(jnp.int32) - 1
        hi, lo = slot >> (nb.bit_length() - 1), slot & (nb - 1)
        acc = jnp.zeros((na, 5 * nb), F32)
        for c in range(n_tok // chunk):
            sl = slice(c * chunk, (c + 1) * chunk)
            one_a = jnp.where(hi[:, sl] == row_a, 1.0, 0.0).astype(BF16)
            in_b = lo[:, sl] == row_b
            pay = jnp.concatenate([jnp.where(in_b, v[:, sl], 0.0) for v in (t_hi, t_lo, a_hi, a_mid, a_lo)],
                                  axis=0).astype(BF16)
            acc = acc + lax.dot_general(one_a, pay, nt, preferred_element_type=F32)
        rows = pl.ds(pl.multiple_of(e * na, na), na) if isinstance(e, jax.Array) else slice(e * na, (e + 1) * na)
        idx_out[0, rows, :] = (acc[:, :nb] * 64.0 + acc[:, nb:2 * nb]).astype(jnp.int32)
        gate_out[0, rows, :] = acc[:, 2 * nb:3 * nb] + acc[:, 3 * nb:4 * nb] + acc[:, 4 * nb:]
        return carry

    if na % SUBLANE == 0:
        lax.fori_loop(0, n_e, per_expert, 0)
    else:
        for e in range(n_e):
            per_expert(e, 0)


def _route(aff_t, cap):
    bsz, n_e, n_tok = aff_t.shape
    nb = 16
    out = pl.BlockSpec((1, n_e * cap // nb, nb), lambda b: (b, 0, 0))
    idx, gates = pl.pallas_call(
        functools.partial(_topk_kernel, cap=cap, chunk=_tile(n_tok, 1024), nb=nb),
        grid=(bsz,), in_specs=[pl.BlockSpec((1, n_e, n_tok), lambda b: (b, 0, 0))],
        out_specs=[out, out],
        out_shape=[jax.ShapeDtypeStruct((bsz, n_e * cap // nb, nb), jnp.int32),
                   jax.ShapeDtypeStruct((bsz, n_e * cap // nb, nb), F32)],
        scratch_shapes=[pltpu.VMEM((n_e, n_tok), F32)],
        compiler_params=_cparams(("parallel",)),
        name="topk_route",
    )(aff_t)
    return idx.reshape(bsz, n_e, cap), gates.reshape(bsz, n_e, cap)


def kernel(x, c, ctx, c_ctx, mod_w, mod_b, norm1_g, w_in, q_a_g, w_q_b, kv_a_g, w_kv_b, conv_dw_w, conv_dw_b, conv_ln_g, conv_ln_b, hy_short_w, hy_short_b, hy_w1, hy_b1, hy_w2, hy_b2, hy_w3, hy_sin_freq, hy_decay, hy_skip, group_norm_g, w_out, norm2_g, router_w, w_gate, w_up, w_down, final_norm_g):
    p = dict(mod_w=mod_w, mod_b=mod_b, norm1_g=norm1_g, w_in=w_in, q_a_g=q_a_g, w_q_b=w_q_b,
             kv_a_g=kv_a_g, w_kv_b=w_kv_b, conv_dw_w=conv_dw_w, conv_dw_b=conv_dw_b,
             conv_ln_g=conv_ln_g, conv_ln_b=conv_ln_b, hy_short_w=hy_short_w,
             hy_short_b=hy_short_b, hy_w1=hy_w1, hy_b1=hy_b1, hy_w2=hy_w2, hy_b2=hy_b2,
             hy_w3=hy_w3, hy_sin_freq=hy_sin_freq, hy_decay=hy_decay, hy_skip=hy_skip,
             group_norm_g=group_norm_g, w_out=w_out, norm2_g=norm2_g, router_w=router_w,
             w_gate=w_gate, w_up=w_up, w_down=w_down)
    depth = mod_w.shape[0]
    bsz, n_lat, d = x.shape
    n_ctx = ctx.shape[1]

    rows = -(-(bsz + 1) // 8) * 8
    cc = jnp.concatenate([c, c_ctx[None, :], jnp.zeros((rows - bsz - 1, d), F32)], axis=0)
    mod_all = _modulation(cc, mod_w, mod_b)
    lat_row = lambda b: b
    ctx_row = lambda b: bsz

    tabs_l = _rope_tables(n_lat)
    tabs_c = _identity_tables(n_ctx)
    tm_l, tm_c = _tile(n_lat, 512), _tile(n_ctx, 256)
    tq_l, tq_c = _tile(n_lat, 256), _tile(n_ctx, 256)
    tl_l, tl_c = _tile(n_lat, 256), _tile(n_ctx, 256)

    xl, xc = x, ctx
    prev_l = prev_c = None
    for i in range(depth):
        last = i == depth - 1
        lw = _layer_weights(i, p)
        mod = mod_all[i].reshape(rows, 1, N_MOD * d)

        def side(xs, prev, row_of_b, tabs, tm, tq, tl, extra_keys, need_mix):
            xs, q, k, v, uc, uh = _inproj(xs, prev, mod, row_of_b, lw, tabs, tm)
            if not need_mix:
                return xs, (k, v), None
            att = _attention(q, [(k, v)] + extra_keys, tq)
            cn, hv, hx1, hx2 = _conv(uc, uh, lw, tl)
            plan = _fft_plan(xs.shape[1])
            mats = _fft_mats(plan)
            filt = _hyena_filter_spectrum(xs.shape[1], i, p, plan, mats)
            hy = _hyena(hv, hx1, hx2, p['hy_skip'][i], plan, mats, filt)
            x1, h2p, aff_t = _outproj(att, cn, hy, xs, mod, row_of_b, lw, tm)
            cap = CAPACITY_FACTOR * xs.shape[1] // N_EXPERTS
            idx, gates = _route(aff_t, cap)
            moe = _moe(h2p, idx, gates, lw)
            return x1, (k, v), moe

        xc, kv_c, moe_c = side(xc, prev_c, ctx_row, tabs_c, tm_c, tq_c, tl_c, [], not last)
        xl, _, moe_l = side(xl, prev_l, lat_row, tabs_l, tm_l, tq_l, tl_l, [kv_c], True)
        prev_l = (moe_l, mod)
        prev_c = None if moe_c is None else (moe_c, mod)

    mod = mod_all[depth - 1].reshape(rows, 1, N_MOD * d)
    return _final(xl, prev_l[0], mod, final_norm_g.reshape(1, d), tm_l)
```

```python
import functools
import math
from typing import NamedTuple

import numpy as np
import jax
import jax.numpy as jnp
from jax import lax
from jax.experimental import pallas as pl
from jax.experimental.pallas import tpu as pltpu

F32 = jnp.float32
BF16 = jnp.bfloat16
EPS = 1e-6

GRID_W = 64
N_MOD = 6
HEADS = 8
NOPE = 64
ROPE = 32
VDIM = 64
Q_LORA = 256
KV_LORA = 128
ROPE_BASE = 10000.0
CONV_CH = 256
CONV_K = 31
HY_CH = 256
HY_ORDER = 2
HY_BANDS = 16
N_EXPERTS = 16
CAPACITY_FACTOR = 2

LANE = 128
SUBLANE = 8
MXU_DIM = 256
HEAD_PAD = LANE
BF16_ROWS = 16
V_ROWS = -(-(VDIM + 1) // BF16_ROWS) * BF16_ROWS
HALO = 16
VMEM_LIMIT = 56 * 1024 * 1024

C_Q = 0
C_KVN = C_Q + Q_LORA
C_KR = C_KVN + KV_LORA
C_CONV = C_KR + LANE
C_HY = C_CONV + 2 * CONV_CH
C_END = C_HY + 3 * HY_CH


def _cparams(sem):
    return pltpu.CompilerParams(dimension_semantics=sem, vmem_limit_bytes=VMEM_LIMIT)


def _rms(x, g):
    return x * lax.rsqrt(jnp.mean(x * x, axis=-1, keepdims=True) + EPS) * g


def _split(a):
    hi = a.astype(BF16)
    lo = (a - hi.astype(F32)).astype(BF16)
    return hi, lo


def _dot(a, b):
    return jnp.dot(a, b, preferred_element_type=F32)


def _from_token_major(ref, n_rows):
    fc = ref.shape[1] // n_rows
    return jnp.concatenate([ref[0, pl.ds(c, n_rows, stride=fc), :] for c in range(fc)], axis=1)


def _mod_kernel(c_ref, w_ref, b_ref, o_ref):
    c = c_ref[...]
    a = c * jax.nn.sigmoid(c)
    a_hi, a_lo = _split(a)
    w_hi, w_lo = _split(w_ref[0])
    o_ref[0] = _dot(a_hi, w_hi) + _dot(a_lo, w_hi) + _dot(a_hi, w_lo) + b_ref[0]


def _modulation(cc, mod_w, mod_b):
    depth, d, n = mod_w.shape
    rows = cc.shape[0]
    tn = 1536
    return pl.pallas_call(
        _mod_kernel,
        grid=(depth, n // tn),
        in_specs=[
            pl.BlockSpec((rows, d), lambda l, j: (0, 0)),
            pl.BlockSpec((1, d, tn), lambda l, j: (l, 0, j)),
            pl.BlockSpec((1, 1, tn), lambda l, j: (l, 0, j)),
        ],
        out_specs=pl.BlockSpec((1, rows, tn), lambda l, j: (l, 0, j)),
        out_shape=jax.ShapeDtypeStruct((depth, rows, n), F32),
        compiler_params=_cparams(("arbitrary", "arbitrary")),
        name="modulation",
    )(cc, mod_w, mod_b.reshape(depth, 1, n))


def _rope(x, c, s1, s2):
    return x * c + pltpu.roll(x, LANE - 8, 1) * s1 + pltpu.roll(x, 8, 1) * s2


def _inproj_kernel(*refs, fuse_prev, scale):
    if fuse_prev:
        xa_ref, xb_ref, g2_ref = refs[:3]
        refs = refs[3:]
    else:
        xa_ref = refs[0]
        refs = refs[1:]
    (sh_ref, sc_ref, n1g_ref, win_ref, qag_ref, wq_ref, kvg_ref, wk_ref, wv_ref,
     c_ref, s1_ref, s2_ref, vone_ref) = refs[:13]
    outs = refs[13:]
    if fuse_prev:
        x_out, q_out, k_out, v_out, uc_out, uh_out = outs
        x = xa_ref[0] + g2_ref[...] * _from_token_major(xb_ref, xa_ref.shape[1])
        x_out[0] = x
    else:
        q_out, k_out, v_out, uc_out, uh_out = outs
        x = xa_ref[0]
    hn = _rms(x, n1g_ref[...]) * (1.0 + sc_ref[...]) + sh_ref[...]
    u = _dot(hn.astype(BF16), win_ref[...])
    c, s1, s2 = c_ref[...], s1_ref[...], s2_ref[...]

    nq = _rms(u[:, C_Q:C_KVN], qag_ref[...]).astype(BF16)
    q = _dot(nq, wq_ref[...])
    for h in range(HEADS):
        sl = slice(h * HEAD_PAD, (h + 1) * HEAD_PAD)
        q_out[0, :, sl] = (_rope(q[:, sl], c, s1, s2) * scale).astype(BF16)

    kvn = _rms(u[:, C_KVN:C_KR], kvg_ref[...]).astype(BF16)
    kr = _rope(u[:, C_KR:C_CONV], c, s1, s2)
    k = _dot(kvn, wk_ref[...])
    for h in range(HEADS):
        sl = slice(h * HEAD_PAD, (h + 1) * HEAD_PAD)
        k_out[0, :, sl] = (k[:, sl] + kr).astype(BF16)
    vt = lax.dot_general(wv_ref[...], kvn, (((1,), (1,)), ((), ())), preferred_element_type=F32)
    v_out[0] = (vt + vone_ref[...]).astype(BF16)
    uc_out[0] = u[:, C_CONV:C_HY]
    uh_out[0] = u[:, C_HY:C_END]


def _inproj(x, prev, mod, row_of_b, lw, tabs, tm):
    bsz, n_tok, d = x.shape
    fuse_prev = prev is not None
    grid = (bsz, n_tok // tm)
    tok = lambda w: pl.BlockSpec((1, tm, w), lambda b, i: (b, i, 0))
    modspec = lambda k: pl.BlockSpec((None, 1, d), lambda b, i: (row_of_b(b), 0, k))
    full = lambda a: pl.BlockSpec(a.shape, lambda b, i: (0,) * a.ndim)
    tabspec = pl.BlockSpec((tm, LANE), lambda b, i: (i, 0))

    args, specs = [x], [tok(d)]
    if fuse_prev:
        args += [prev[0], prev[1]]
        specs += [pl.BlockSpec((1, tm * (d // LANE), LANE), lambda b, i: (b, i, 0)), modspec(5)]
    args += [mod, mod, lw['norm1_g'], lw['w_in'], lw['q_a_g'], lw['w_q'], lw['kv_a_g'],
             lw['w_k'], lw['w_v'], tabs[0], tabs[1], tabs[2], lw['v_one']]
    specs += [modspec(0), modspec(1), full(lw['norm1_g']), full(lw['w_in']), full(lw['q_a_g']),
              full(lw['w_q']), full(lw['kv_a_g']), full(lw['w_k']), full(lw['w_v']),
              tabspec, tabspec, tabspec, full(lw['v_one'])]
    hp = HEADS * HEAD_PAD
    out_shape, out_specs = [], []
    if fuse_prev:
        out_shape.append(jax.ShapeDtypeStruct((bsz, n_tok, d), F32))
        out_specs.append(tok(d))
    out_shape += [jax.ShapeDtypeStruct((bsz, n_tok, hp), BF16)] * 2
    out_specs += [tok(hp)] * 2
    out_shape += [jax.ShapeDtypeStruct((bsz, hp, n_tok), BF16),
                  jax.ShapeDtypeStruct((bsz, n_tok, 2 * CONV_CH), F32),
                  jax.ShapeDtypeStruct((bsz, n_tok, 3 * HY_CH), F32)]
    out_specs += [pl.BlockSpec((1, hp, tm), lambda b, i: (b, 0, i)),
                  tok(2 * CONV_CH), tok(3 * HY_CH)]
    scale = float((NOPE + ROPE) ** -0.5 * math.log2(math.e))
    res = pl.pallas_call(
        functools.partial(_inproj_kernel, fuse_prev=fuse_prev, scale=scale),
        grid=grid, in_specs=specs, out_specs=out_specs, out_shape=out_shape,
        compiler_params=_cparams(("parallel", "parallel")),
        name="inproj",
    )(*args)
    if not fuse_prev:
        res = [x] + list(res)
    return res


def _attn_kernel(*refs, n_sets, hps, tk, ahead):
    q_ref = refs[0]
    kv = refs[1:1 + 2 * n_sets]
    o_ref = refs[1 + 2 * n_sets]
    nt = (((1,), (1,)), ((), ()))
    sls = [slice(hh * HEAD_PAD, (hh + 1) * HEAD_PAD) for hh in range(hps)]
    qs = [q_ref[0, :, sl] for sl in sls]
    items = [(hh, i, c) for i in range(n_sets) for c in range(kv[2 * i].shape[1] // tk)
             for hh in range(hps)]

    def scores(item):
        hh, i, c = item
        return lax.dot_general(kv[2 * i][0, c * tk:(c + 1) * tk, sls[hh]], qs[hh], nt,
                               preferred_element_type=F32)

    m, o = [None] * hps, [None] * hps
    pending = [scores(it) for it in items[:ahead]]
    for n, (hh, i, c) in enumerate(items):
        s = pending.pop(0)
        if n + ahead < len(items):
            pending.append(scores(items[n + ahead]))
        mt = jnp.max(s, axis=0, keepdims=True)
        m_new = mt if m[hh] is None else jnp.maximum(m[hh], mt)
        pv = _dot(kv[2 * i + 1][0, hh * HEAD_PAD:hh * HEAD_PAD + V_ROWS, c * tk:(c + 1) * tk],
                  jnp.exp2(s - m_new).astype(BF16))
        o[hh] = pv if o[hh] is None else o[hh] * jnp.exp2(m[hh] - m_new) + pv
        m[hh] = m_new
    pad = jnp.zeros((HEAD_PAD - V_ROWS, o[0].shape[1]), F32)
    outs = [jnp.concatenate([oh * (1.0 / oh[VDIM:VDIM + 1, :]), pad], axis=0).T for oh in o]
    lane = lax.broadcasted_iota(jnp.int32, outs[0].shape, 1)
    for pr in range(hps // 2):
        o_ref[0, :, pr * LANE:(pr + 1) * LANE] = jnp.where(
            lane < VDIM, outs[2 * pr], pltpu.roll(outs[2 * pr + 1], VDIM, 1))


def _attention(q, key_sets, tq):
    bsz, n_q, hp = q.shape
    hps = 8
    grid = (bsz, HEADS // hps, n_q // tq)
    args = [q]
    specs = [pl.BlockSpec((1, tq, hps * HEAD_PAD), lambda b, h, i: (b, i, h))]
    for k, v in key_sets:
        n_k = k.shape[1]
        args += [k, v]
        specs += [pl.BlockSpec((1, n_k, hps * HEAD_PAD), lambda b, h, i: (b, 0, h)),
                  pl.BlockSpec((1, hps * HEAD_PAD, n_k), lambda b, h, i: (b, h, 0))]
    return pl.pallas_call(
        functools.partial(_attn_kernel, n_sets=len(key_sets), hps=hps,
                          tk=min([MXU_DIM] + [k.shape[1] for k, _ in key_sets]), ahead=8),
        grid=grid, in_specs=specs,
        out_specs=pl.BlockSpec((1, tq, hps * VDIM), lambda b, h, i: (b, i, h)),
        out_shape=jax.ShapeDtypeStruct((bsz, n_q, HEADS * VDIM), F32),
        compiler_params=_cparams(("parallel", "parallel", "arbitrary")),
        name="attention",
    )(*args)


def _conv_kernel(ucp_ref, uc_ref, ucn_ref, uhp_ref, uh_ref, uhn_ref,
                 cw_ref, cb_ref, lg_ref, lb_ref, gn_ref, hw_ref, hb_ref,
                 cn_out, v_out, x1_out, x2_out, ypad, hpad, yph, *, tl, row_tile):
    i = pl.program_id(1)
    has_prev = (i > 0).astype(F32)
    has_next = (i < pl.num_programs(1) - 1).astype(F32)

    def glu(u):
        return u[:, :CONV_CH] * jax.nn.sigmoid(u[:, CONV_CH:])

    ypad[0:HALO, :] = glu(ucp_ref[0]) * has_prev
    ypad[HALO:HALO + tl, :] = glu(uc_ref[0])
    ypad[HALO + tl:, :] = glu(ucn_ref[0]) * has_next
    hpad[0:HALO, :] = uhp_ref[0] * has_prev
    hpad[HALO:HALO + tl, :] = uh_ref[0]
    hpad[HALO + tl:, :] = uhn_ref[0] * has_next

    span = tl + 2 * HALO - SUBLANE
    for ph in range(SUBLANE):
        yph[ph] = ypad[ph:ph + span, :]

    half = CONV_K // 2
    for r in range(tl // row_tile):
        base = r * row_tile
        acc = jnp.zeros((row_tile, CONV_CH), F32) + cb_ref[...]
        for k in range(CONV_K):
            off = HALO + base + k - half
            ph = off % SUBLANE
            acc = acc + cw_ref[k:k + 1, :] * yph[ph, off - ph:off - ph + row_tile, :]
        mu = jnp.mean(acc, axis=-1, keepdims=True)
        cen = acc - mu
        var = jnp.mean(cen * cen, axis=-1, keepdims=True)
        y = cen * lax.rsqrt(var + EPS) * lg_ref[...] + lb_ref[...]
        y = y * jax.nn.sigmoid(y)
        cn_out[0, base:base + row_tile, :] = _rms(y, gn_ref[...])

        z = jnp.zeros((row_tile, 3 * HY_CH), F32) + hb_ref[...]
        for k in range(3):
            off = HALO + base + k - 1
            z = z + hw_ref[k:k + 1, :] * hpad[off:off + row_tile, :]
        v_out[0, base:base + row_tile, :] = z[:, :HY_CH]
        x1_out[0, base:base + row_tile, :] = z[:, HY_CH:2 * HY_CH]
        x2_out[0, base:base + row_tile, :] = z[:, 2 * HY_CH:]


def _conv(uc, uh, lw, tl):
    bsz, n_tok, _ = uc.shape
    nh = tl // HALO
    last = n_tok // HALO - 1
    cur = lambda w: pl.BlockSpec((1, tl, w), lambda b, i: (b, i, 0))
    prv = lambda w: pl.BlockSpec((1, HALO, w), lambda b, i: (b, jnp.maximum(i * nh - 1, 0), 0))
    nxt = lambda w: pl.BlockSpec((1, HALO, w), lambda b, i: (b, jnp.minimum((i + 1) * nh, last), 0))
    full = lambda a: pl.BlockSpec(a.shape, lambda b, i: (0,) * a.ndim)
    wts = [lw['conv_dw_w'], lw['conv_dw_b'], lw['conv_ln_g'], lw['conv_ln_b'], lw['gn_conv'],
           lw['hy_short_w'], lw['hy_short_b']]
    return pl.pallas_call(
        functools.partial(_conv_kernel, tl=tl, row_tile=min(64, tl)),
        grid=(bsz, n_tok // tl),
        in_specs=[prv(2 * CONV_CH), cur(2 * CONV_CH), nxt(2 * CONV_CH),
                  prv(3 * HY_CH), cur(3 * HY_CH), nxt(3 * HY_CH)] + [full(w) for w in wts],
        out_specs=[cur(CONV_CH)] + [cur(HY_CH)] * 3,
        out_shape=[jax.ShapeDtypeStruct((bsz, n_tok, CONV_CH), F32)]
        + [jax.ShapeDtypeStruct((bsz, n_tok, HY_CH), F32)] * 3,
        scratch_shapes=[pltpu.VMEM((tl + 2 * HALO, CONV_CH), F32),
                        pltpu.VMEM((tl + 2 * HALO, 3 * HY_CH), F32),
                        pltpu.VMEM((SUBLANE, tl + 2 * HALO - SUBLANE, CONV_CH), F32)],
        compiler_params=_cparams(("parallel", "parallel")),
        name="conv",
    )(uc, uc, uc, uh, uh, uh, *wts)


def _outproj_kernel(att_ref, cn_ref, hy_ref, x_ref, g1_ref, sh_ref, sc_ref, gna_ref, gnh_ref,
                    wo_ref, n2g_ref, rw_ref, x1_out, h2p_out, aff_out):
    a = _rms(att_ref[0], gna_ref[...]).astype(BF16)
    c = cn_ref[0].astype(BF16)
    h = _rms(hy_ref[0], gnh_ref[...]).astype(BF16)
    na, nc = a.shape[1], c.shape[1]
    y = (_dot(a, wo_ref[0:na, :]) + _dot(c, wo_ref[na:na + nc, :]) + _dot(h, wo_ref[na + nc:, :]))
    x1 = x_ref[0] + g1_ref[...] * y
    x1_out[0] = x1
    h2 = _rms(x1, n2g_ref[...]) * (1.0 + sc_ref[...]) + sh_ref[...]
    d = h2.shape[1]
    lo = lax.bitcast_convert_type(h2[:, :d // 2].astype(BF16).astype(F32), jnp.uint32)
    hi = lax.bitcast_convert_type(h2[:, d // 2:].astype(BF16).astype(F32), jnp.uint32)
    packed = hi | (lo >> 16)
    n_rows, pc = packed.shape[0], packed.shape[1] // LANE
    for c in range(pc):
        h2p_out[0, pl.ds(c, n_rows, stride=pc), :] = packed[:, c * LANE:(c + 1) * LANE]
    h_hi, h_lo = _split(h2)
    t = _dot(h_hi, rw_ref[...])
    logits = t[:, :LANE] + t[:, LANE:] + _dot(h_lo, rw_ref[:, :LANE])
    lane = lax.broadcasted_iota(jnp.int32, logits.shape, 1)
    logits = jnp.where(lane < N_EXPERTS, logits, -1e30)
    e = jnp.exp(logits - jnp.max(logits, axis=1, keepdims=True))
    aff = e / jnp.sum(e, axis=1, keepdims=True)
    aff_out[0] = aff.T[:N_EXPERTS, :]


def _outproj(att, cn, hy, x, mod, row_of_b, lw, tm):
    bsz, n_tok, d = x.shape
    tok = lambda w: pl.BlockSpec((1, tm, w), lambda b, i: (b, i, 0))
    modspec = lambda k: pl.BlockSpec((None, 1, d), lambda b, i: (row_of_b(b), 0, k))
    full = lambda a: pl.BlockSpec(a.shape, lambda b, i: (0,) * a.ndim)
    wts = [lw['gn_att'], lw['gn_hy'], lw['w_out'], lw['norm2_g'], lw['router_w']]
    pc = d // 2 // LANE
    return pl.pallas_call(
        _outproj_kernel,
        grid=(bsz, n_tok // tm),
        in_specs=[tok(att.shape[2]), tok(cn.shape[2]), tok(hy.shape[2]), tok(d),
                  modspec(2), modspec(3), modspec(4)] + [full(w) for w in wts],
        out_specs=[tok(d), pl.BlockSpec((1, tm * pc, LANE), lambda b, i: (b, i, 0)),
                   pl.BlockSpec((1, N_EXPERTS, tm), lambda b, i: (b, 0, i))],
        out_shape=[jax.ShapeDtypeStruct((bsz, n_tok, d), F32),
                   jax.ShapeDtypeStruct((bsz, n_tok * pc, LANE), jnp.uint32),
                   jax.ShapeDtypeStruct((bsz, N_EXPERTS, n_tok), F32)],
        compiler_params=_cparams(("parallel", "parallel")),
        name="outproj",
    )(att, cn, hy, x, mod, mod, mod, *wts)


def _moe_kernel(idx_ref, gate_ref, h2p_ref, wg_ref, wu_ref, wd_ref, out_hbm,
                acc_ref, xg_ref, y_ref, sem, *, cap, pc, fc):
    b = pl.program_id(0)
    e = pl.program_id(1)

    @pl.when(e == 0)
    def _():
        acc_ref[...] = jnp.zeros_like(acc_ref)

    group = math.gcd(cap, SUBLANE)

    def gather(jg, carry):
        js = [jg * group + u for u in range(group)]
        rows = [h2p_ref[0, pl.ds(pl.multiple_of(idx_ref[0, 0, j] * pc, pc), pc), :] for j in js]
        for j, row in zip(js, rows):
            xg_ref[pl.ds(pl.multiple_of(j * pc, pc), pc), :] = row
        return carry
    lax.fori_loop(0, cap // group, gather, 0)

    lo, hi = [], []
    for c in range(pc):
        w = xg_ref[pl.ds(c, cap, stride=pc), :]
        lo.append(lax.bitcast_convert_type(w << 16, F32).astype(BF16))
        hi.append(lax.bitcast_convert_type(w & jnp.uint32(0xFFFF0000), F32).astype(BF16))
    x = jnp.concatenate(lo + hi, axis=1)
    a = _dot(x, wg_ref[0])
    u = _dot(x, wu_ref[0])
    hmid = (a * jax.nn.sigmoid(a) * u).astype(BF16)
    y = _dot(hmid, wd_ref[0])
    for c in range(fc):
        y_ref[pl.ds(c, cap, stride=fc), :] = y[:, c * LANE:(c + 1) * LANE]

    def scatter(jg, carry):
        js = [jg * group + u for u in range(group)]
        dsts = [pl.ds(pl.multiple_of(idx_ref[0, 0, j] * fc, fc), fc) for j in js]
        new = [acc_ref[dst, :] + gate_ref[0, 0, j] * y_ref[pl.ds(pl.multiple_of(j * fc, fc), fc), :]
               for j, dst in zip(js, dsts)]
        for dst, val in zip(dsts, new):
            acc_ref[dst, :] = val
        return carry
    lax.fori_loop(0, cap // group, scatter, 0)

    @pl.when(e == pl.num_programs(1) - 1)
    def _():
        cp = pltpu.make_async_copy(acc_ref, out_hbm.at[b], sem)
        cp.start()
        cp.wait()


def _moe(h2p, idx, gates, lw):
    d = lw['w_gate'].shape[1]
    pc, fc = d // 2 // LANE, d // LANE
    bsz, n_tok = h2p.shape[0], h2p.shape[1] // pc
    n_e, cap = idx.shape[1], idx.shape[2]
    ff = lw['w_gate'].shape[2]
    smem = lambda: pl.BlockSpec((1, 1, cap), lambda b, e: (b * n_e + e, 0, 0),
                                memory_space=pltpu.SMEM)
    return pl.pallas_call(
        functools.partial(_moe_kernel, cap=cap, pc=pc, fc=fc),
        grid=(bsz, n_e),
        in_specs=[smem(), smem(),
                  pl.BlockSpec((1, n_tok * pc, LANE), lambda b, e: (b, 0, 0)),
                  pl.BlockSpec((1, d, ff), lambda b, e: (e, 0, 0)),
                  pl.BlockSpec((1, d, ff), lambda b, e: (e, 0, 0)),
                  pl.BlockSpec((1, ff, d), lambda b, e: (e, 0, 0))],
        out_specs=pl.BlockSpec(memory_space=pl.ANY),
        out_shape=jax.ShapeDtypeStruct((bsz, n_tok * fc, LANE), F32),
        scratch_shapes=[pltpu.VMEM((n_tok * fc, LANE), F32),
                        pltpu.VMEM((cap * pc, LANE), jnp.uint32),
                        pltpu.VMEM((cap * fc, LANE), F32),
                        pltpu.SemaphoreType.DMA(())],
        compiler_params=_cparams(("arbitrary", "arbitrary")),
        name="moe",
    )(idx.reshape(bsz * n_e, 1, cap), gates.reshape(bsz * n_e, 1, cap),
      h2p, lw['w_gate'], lw['w_up'], lw['w_down'])


def _final_kernel(x_ref, m_ref, g2_ref, fg_ref, o_ref):
    o_ref[0] = _rms(x_ref[0] + g2_ref[...] * _from_token_major(m_ref, x_ref.shape[1]), fg_ref[...])


def _final(x1, moe, mod, fg, tm):
    bsz, n_tok, d = x1.shape
    tok = pl.BlockSpec((1, tm, d), lambda b, i: (b, i, 0))
    return pl.pallas_call(
        _final_kernel,
        grid=(bsz, n_tok // tm),
        in_specs=[tok, pl.BlockSpec((1, tm * (d // LANE), LANE), lambda b, i: (b, i, 0)),
                  pl.BlockSpec((None, 1, d), lambda b, i: (b, 0, 5)),
                  pl.BlockSpec((1, d), lambda b, i: (0, 0))],
        out_specs=tok,
        out_shape=jax.ShapeDtypeStruct((bsz, n_tok, d), F32),
        compiler_params=_cparams(("parallel", "parallel")),
        name="final_norm",
    )(x1, moe, mod, fg)


def _rope_tables(n_tok):
    rows = n_tok // GRID_W
    row = jnp.repeat(jnp.arange(rows, dtype=F32), GRID_W)
    col = jnp.tile(jnp.arange(GRID_W, dtype=F32), rows)
    half = ROPE // 2
    inv = ROPE_BASE ** (-jnp.arange(0, half, 2, dtype=F32) / half)
    cr, sr = jnp.cos(row[:, None] * inv), jnp.sin(row[:, None] * inv)
    cc, sc = jnp.cos(col[:, None] * inv), jnp.sin(col[:, None] * inv)
    z8 = jnp.zeros_like(cr)
    ones = jnp.ones((n_tok, NOPE), F32)
    pad = jnp.zeros((n_tok, HEAD_PAD - NOPE - ROPE), F32)
    c = jnp.concatenate([ones, cr, cr, cc, cc, pad + 1.0], axis=1)
    s1 = jnp.concatenate([ones * 0.0, -sr, z8, -sc, z8, pad], axis=1)
    s2 = jnp.concatenate([ones * 0.0, z8, sr, z8, sc, pad], axis=1)
    return c, s1, s2


def _identity_tables(n_tok):
    return (jnp.ones((n_tok, HEAD_PAD), F32), jnp.zeros((n_tok, HEAD_PAD), F32),
            jnp.zeros((n_tok, HEAD_PAD), F32))


def _pad_heads(w, per_head, take):
    k = w.shape[0]
    w = w.reshape(k, HEADS, per_head)[:, :, take]
    w = jnp.pad(w, ((0, 0), (0, 0), (0, HEAD_PAD - w.shape[2])))
    return w.reshape(k, HEADS * HEAD_PAD).astype(BF16)


def _layer_weights(i, p):
    d = p['w_in'].shape[1]
    w_in = p['w_in'][i]
    off_kv = Q_LORA
    off_conv = off_kv + KV_LORA + ROPE
    off_hy = off_conv + 2 * CONV_CH
    kr = jnp.zeros((d, LANE), F32).at[:, NOPE:NOPE + ROPE].set(w_in[:, off_kv + KV_LORA:off_conv])
    w_in_r = jnp.concatenate([w_in[:, :off_kv], w_in[:, off_kv:off_kv + KV_LORA], kr,
                              w_in[:, off_conv:off_hy], w_in[:, off_hy:]], axis=1).astype(BF16)
    gn = p['group_norm_g'][i]
    n_att = HEADS * VDIM
    rw_hi, rw_lo = _split(jnp.pad(p['router_w'][i], ((0, 0), (0, LANE - N_EXPERTS))))
    v_one = jnp.zeros((HEADS, HEAD_PAD), F32).at[:, VDIM].set(1.0).reshape(HEADS * HEAD_PAD, 1)
    row = lambda a: a.reshape(1, -1)
    return {
        'norm1_g': row(p['norm1_g'][i]), 'w_in': w_in_r,
        'q_a_g': row(p['q_a_g'][i]), 'w_q': _pad_heads(p['w_q_b'][i], NOPE + ROPE, slice(None)),
        'kv_a_g': row(p['kv_a_g'][i]),
        'w_k': _pad_heads(p['w_kv_b'][i], NOPE + VDIM, slice(0, NOPE)),
        'w_v': _pad_heads(p['w_kv_b'][i], NOPE + VDIM, slice(NOPE, NOPE + VDIM)).T,
        'v_one': v_one,
        'conv_dw_w': p['conv_dw_w'][i], 'conv_dw_b': row(p['conv_dw_b'][i]),
        'conv_ln_g': row(p['conv_ln_g'][i]), 'conv_ln_b': row(p['conv_ln_b'][i]),
        'gn_att': row(gn[:n_att]), 'gn_conv': row(gn[n_att:n_att + CONV_CH]),
        'gn_hy': row(gn[n_att + CONV_CH:]),
        'hy_short_w': p['hy_short_w'][i], 'hy_short_b': row(p['hy_short_b'][i]),
        'w_out': p['w_out'][i].astype(BF16), 'norm2_g': row(p['norm2_g'][i]),
        'router_w': jnp.concatenate([rw_hi, rw_lo], axis=1),
        'w_gate': p['w_gate'][i].astype(BF16), 'w_up': p['w_up'][i].astype(BF16),
        'w_down': p['w_down'][i].astype(BF16),
    }


def _tile(n, pref):
    return pref if n % pref == 0 else n


def _dot3(a, b):
    a_hi, a_lo = _split(a)
    b_hi, b_lo = _split(b)
    return _dot(a_hi, b_hi) + _dot(a_lo, b_hi) + _dot(a_hi, b_lo)


class _FftPlan(NamedTuple):
    n2: int
    nh: int
    k1p: int
    f1: np.ndarray
    g1: np.ndarray
    mf: np.ndarray
    mi: np.ndarray


@functools.lru_cache(maxsize=None)
def _fft_plan(n_tok):
    n = 2 * n_tok
    n2 = 64 if n_tok >= 2048 else 16
    n1 = n // n2
    nh = n1 // 2
    k1 = nh + 1
    k1p = -(-k1 // SUBLANE) * SUBLANE
    two_pi = 2.0 * np.pi
    r = np.arange(k1)
    ang1 = two_pi * ((np.arange(nh)[None, :] * r[:, None]) % n1) / n1
    f1 = np.zeros((2 * k1p, nh))
    f1[:k1], f1[k1p:k1p + k1] = np.cos(ang1), -np.sin(ang1)
    w = np.where((r == 0) | (r == nh), 1.0, 2.0)[None, :] / n
    g1 = np.zeros((nh, 2 * k1p))
    g1[:, :k1], g1[:, k1p:k1p + k1] = np.cos(ang1.T) * w, -np.sin(ang1.T) * w
    k = r[:, None, None] + n1 * np.arange(n2)[None, :, None]
    th = two_pi * ((k * np.arange(n2)[None, None, :]) % n) / n
    tc, ts = np.cos(th), -np.sin(th)
    mf = np.zeros((k1p, 2 * n2, 2 * n2))
    mi = np.zeros((k1p, 2 * n2, 2 * n2))
    mf[:k1, :n2, :n2], mf[:k1, :n2, n2:], mf[:k1, n2:, :n2], mf[:k1, n2:, n2:] = tc, -ts, ts, tc
    tct, tst = tc.transpose(0, 2, 1), ts.transpose(0, 2, 1)
    mi[:k1, :n2, :n2], mi[:k1, :n2, n2:], mi[:k1, n2:, :n2], mi[:k1, n2:, n2:] = tct, tst, -tst, tct
    return _FftPlan(n2, nh, k1p, f1, g1, mf, mi)


def _stage1(x_ref, lead, f_ref, a_ref, nh, n2, rows, mm):
    def body(j, carry):
        x = x_ref[lead, pl.ds(j, nh, stride=n2), :]
        a_ref[pl.ds(pl.multiple_of(j * rows, SUBLANE), rows), :] = mm(f_ref[...], x)
        return carry
    lax.fori_loop(0, n2, body, 0, unroll=8)


def _spectrum_rows(a_ref, k, k1p, n2, rows):
    return jnp.concatenate([a_ref[pl.ds(k, n2, stride=rows), :],
                            a_ref[pl.ds(k1p + k, n2, stride=rows), :]], axis=0)


def _hyena_conv_kernel(u_ref, gate_ref, skip_ref, f1_ref, g1_ref, mf_ref, mi_ref, g_ref, y_out, a_ref,
                       *, nh, n2, k1p):
    rows = 2 * k1p
    bdot = lambda f, x: _dot(f, x.astype(BF16))
    _stage1(u_ref, 0, f1_ref, a_ref, nh, n2, rows, bdot)

    def stage2(k, carry):
        y = _dot(mf_ref[k], _spectrum_rows(a_ref, k, k1p, n2, rows).astype(BF16))
        yr, yi = y[:n2], y[n2:]
        gr, gi = g_ref[0, 0, k], g_ref[0, 1, k]
        z = jnp.concatenate([yr * gr - yi * gi, yr * gi + yi * gr], axis=0).astype(BF16)
        v = _dot(mi_ref[k], z)
        a_ref[pl.ds(k, n2, stride=rows), :] = v[:n2]
        a_ref[pl.ds(k1p + k, n2, stride=rows), :] = v[n2:]
        return carry
    lax.fori_loop(0, k1p, stage2, 0, unroll=8)

    def inverse1(j, carry):
        v = a_ref[pl.ds(pl.multiple_of(j * rows, SUBLANE), rows), :].astype(BF16)
        sl = pl.ds(j, nh, stride=n2)
        y_out[0, sl, :] = gate_ref[0, sl, :] * (_dot(g1_ref[...], v) + u_ref[0, sl, :] * skip_ref[...])
        return carry
    lax.fori_loop(0, n2, inverse1, 0, unroll=8)


def _hyena_conv(u, gate, skip, g, order, plan, mats):
    bsz, n_tok, ch = u.shape
    nh, n2, k1p = plan.nh, plan.n2, plan.k1p
    rows = 2 * k1p
    tok = pl.BlockSpec((1, n_tok, LANE), lambda ci, b: (b, 0, ci))
    const = lambda a: pl.BlockSpec(a.shape, lambda ci, b: (0,) * a.ndim)
    return pl.pallas_call(
        functools.partial(_hyena_conv_kernel, nh=nh, n2=n2, k1p=k1p),
        grid=(ch // LANE, bsz),
        in_specs=[tok, tok, pl.BlockSpec((None, 1, LANE), lambda ci, b: (order, 0, ci)),
                  const(mats['f1']), const(mats['g1']), const(mats['mf']), const(mats['mi']),
                  pl.BlockSpec((1, 2, k1p, n2, LANE), lambda ci, b: (order, 0, 0, 0, ci))],
        out_specs=tok,
        out_shape=jax.ShapeDtypeStruct((bsz, n_tok, ch), F32),
        scratch_shapes=[pltpu.VMEM((n2 * rows, LANE), F32)],
        compiler_params=_cparams(("parallel", "parallel")),
        name="hyena_conv",
    )(u, gate, skip.reshape(skip.shape[0], 1, ch), mats['f1'], mats['g1'], mats['mf'], mats['mi'], g)


def _filter_spectrum_kernel(h_ref, ss_ref, f1_ref, mf_ref, g_out, a_ref, *, nh, n2, k1p):
    rows = 2 * k1p
    s = lax.rsqrt(ss_ref[0] + ss_ref[1] + EPS)
    for direction in range(2):
        _stage1(h_ref, direction, f1_ref, a_ref, nh, n2, rows, _dot3)

        def stage2(k, carry):
            y = _dot3(mf_ref[k], _spectrum_rows(a_ref, k, k1p, n2, rows))
            if direction == 0:
                g_out[0, 0, k] = y[:n2] * s
                g_out[0, 1, k] = y[n2:] * s
            else:
                g_out[0, 0, k] = g_out[0, 0, k] + y[:n2] * s
                g_out[0, 1, k] = g_out[0, 1, k] - y[n2:] * s
            return carry
        lax.fori_loop(0, k1p, stage2, 0)


def _filter_spectrum(h, ss, plan, mats):
    groups, n_tok, ch = h.shape
    nh, n2, k1p = plan.nh, plan.n2, plan.k1p
    full = lambda a: pl.BlockSpec(a.shape, lambda o, ci: (0,) * a.ndim)
    return pl.pallas_call(
        functools.partial(_filter_spectrum_kernel, nh=nh, n2=n2, k1p=k1p),
        grid=(groups // 2, ch // LANE),
        in_specs=[pl.BlockSpec((2, n_tok, LANE), lambda o, ci: (o, 0, ci)),
                  pl.BlockSpec((2, 1, LANE), lambda o, ci: (o, 0, ci)),
                  full(mats['f1_32']), full(mats['mf32'])],
        out_specs=pl.BlockSpec((1, 2, k1p, n2, LANE), lambda o, ci: (o, 0, 0, 0, ci)),
        out_shape=jax.ShapeDtypeStruct((groups // 2, 2, k1p, n2, ch), F32),
        scratch_shapes=[pltpu.VMEM((n2 * 2 * k1p, LANE), F32)],
        compiler_params=_cparams(("arbitrary", "arbitrary")),
        name="hyena_filter_spectrum",
    )(h, ss, mats['f1_32'], mats['mf32'])


def _filt_kernel(z_ref, w1_ref, b1_ref, w2_ref, b2_ref, fr_ref, w3_ref, dec_ref, h_out, ss_out, *, tl):
    i = pl.program_id(0)
    z = z_ref[...]
    h = jnp.sin(fr_ref[0:1, :] * (_dot3(z, w1_ref[...]) + b1_ref[...]))
    h = jnp.sin(fr_ref[1:2, :] * (_dot3(h, w2_ref[...]) + b2_ref[...]))
    h = _dot3(h, w3_ref[...])
    decay = jnp.exp(-z[:, 0:1] * jnp.abs(dec_ref[...]))
    row = i * tl + lax.broadcasted_iota(jnp.int32, (tl, 1), 0)

    @pl.when(i == 0)
    def _():
        ss_out[...] = jnp.zeros_like(ss_out)

    for g in range(2 * HY_ORDER):
        hg = h[:, g * HY_CH:(g + 1) * HY_CH] * decay
        if g % 2 == 1:
            hg = jnp.where(row > 0, hg, 0.0)
        h_out[g] = hg
        ss_out[g] += jnp.sum(hg * hg, axis=0, keepdims=True)


def _hyena_filter_spectrum(n_tok, i, p, plan, mats):
    t = jnp.linspace(0.0, 1.0, n_tok, dtype=F32)[:, None]
    w = 2.0 * math.pi * jnp.arange(n_tok, dtype=F32) / n_tok
    f = jnp.linspace(1e-4, HY_BANDS - 1, HY_BANDS, dtype=F32)
    fw = w[:, None] * f[None, :]
    feat = jnp.concatenate([t, jnp.cos(fw), -jnp.sin(fw)], axis=-1)
    n_feat = feat.shape[1]
    hid = p['hy_w1'].shape[2]
    feat = jnp.pad(feat, ((0, 0), (0, hid - n_feat)))
    w1 = jnp.pad(p['hy_w1'][i], ((0, hid - n_feat), (0, 0)))
    tl = _tile(n_tok, 512)
    groups = 2 * HY_ORDER
    full = lambda a: pl.BlockSpec(a.shape, lambda s: (0,) * a.ndim)
    wts = [w1, p['hy_b1'][i][None], p['hy_w2'][i], p['hy_b2'][i][None], p['hy_sin_freq'][i],
           p['hy_w3'][i], p['hy_decay'][i][None]]
    h, ss = pl.pallas_call(
        functools.partial(_filt_kernel, tl=tl),
        grid=(n_tok // tl,),
        in_specs=[pl.BlockSpec((tl, hid), lambda s: (s, 0))] + [full(a) for a in wts],
        out_specs=[pl.BlockSpec((groups, tl, HY_CH), lambda s: (0, s, 0)),
                   pl.BlockSpec((groups, 1, HY_CH), lambda s: (0, 0, 0))],
        out_shape=[jax.ShapeDtypeStruct((groups, n_tok, HY_CH), F32),
                   jax.ShapeDtypeStruct((groups, 1, HY_CH), F32)],
        compiler_params=_cparams(("arbitrary",)),
        name="hyena_filter_mlp",
    )(feat, *wts)
    return _filter_spectrum(h, ss, plan, mats)


def _hyena(v, x1, x2, skip, plan, mats, g):
    y = v
    for o, gate in enumerate((x1, x2)):
        y = _hyena_conv(y, gate, skip, g, o, plan, mats)
    return y


def _fft_mats(plan):
    return {'f1': jnp.asarray(plan.f1, BF16), 'f1_32': jnp.asarray(plan.f1, F32),
            'g1': jnp.asarray(plan.g1, BF16), 'mf': jnp.asarray(plan.mf, BF16),
            'mf32': jnp.asarray(plan.mf, F32), 'mi': jnp.asarray(plan.mi, BF16)}


def _cumsum_lanes(x):
    lane = lax.broadcasted_iota(jnp.int32, x.shape, 1)
    s = 1
    while s < x.shape[1]:
        x = x + jnp.where(lane >= s, pltpu.roll(x, s, 1), 0.0)
        s *= 2
    return x


def _topk_kernel(aff_ref, idx_out, gate_out, pos_ref, *, cap, chunk, nb):
    a = aff_ref[0]
    n_e, n_tok = a.shape
    bits = lax.bitcast_convert_type(a, jnp.int32)

    def count(mask):
        return jnp.sum(jnp.where(mask, 1.0, 0.0), axis=1, keepdims=True)

    def bisect(_, lohi):
        lo, hi = lohi
        mid = lo + ((hi - lo) >> 1)
        ok = count(bits >= mid) >= cap
        return jnp.where(ok, mid, lo), jnp.where(ok, hi, mid)

    lo0 = jnp.zeros((n_e, 1), jnp.int32)
    hi0 = jnp.full((n_e, 1), 0x7F800000, jnp.int32)
    thr, _ = lax.fori_loop(0, 31, bisect, (lo0, hi0))
    gt = bits > thr
    eqf = jnp.where(bits == thr, 1.0, 0.0)
    need = cap - count(gt)
    rank = _cumsum_lanes(eqf) - eqf
    self_ = jnp.where(gt, 1.0, jnp.where(rank < need, eqf, 0.0))
    pos_ref[...] = _cumsum_lanes(self_) * self_

    na = cap // nb
    tok = lax.broadcasted_iota(jnp.int32, (1, n_tok), 1)
    t_hi = (tok >> 6).astype(F32)
    t_lo = (tok & 63).astype(F32)
    row_a = lax.broadcasted_iota(jnp.int32, (na, 1), 0)
    row_b = lax.broadcasted_iota(jnp.int32, (nb, 1), 0)
    nt = (((1,), (1,)), ((), ()))

    def per_expert(e, carry):
        aff = aff_ref[0, pl.ds(e, 1), :]
        a_hi = aff.astype(BF16).astype(F32)
        a_mid = (aff - a_hi).astype(BF16).astype(F32)
        a_lo = aff - a_hi - a_mid
        slot = pos_ref[pl.ds(e, 1), :].astype(jnp.int32) - 1
        hi, lo = slot >> (nb.bit_length() - 1), slot & (nb - 1)
        acc = jnp.zeros((na, 5 * nb), F32)
        for c in range(n_tok // chunk):
            sl = slice(c * chunk, (c + 1) * chunk)
            one_a = jnp.where(hi[:, sl] == row_a, 1.0, 0.0).astype(BF16)
            in_b = lo[:, sl] == row_b
            pay = jnp.concatenate([jnp.where(in_b, v[:, sl], 0.0) for v in (t_hi, t_lo, a_hi, a_mid, a_lo)],
                                  axis=0).astype(BF16)
            acc = acc + lax.dot_general(one_a, pay, nt, preferred_element_type=F32)
        rows = pl.ds(pl.multiple_of(e * na, na), na) if isinstance(e, jax.Array) else slice(e * na, (e + 1) * na)
        idx_out[0, rows, :] = (acc[:, :nb] * 64.0 + acc[:, nb:2 * nb]).astype(jnp.int32)
        gate_out[0, rows, :] = acc[:, 2 * nb:3 * nb] + acc[:, 3 * nb:4 * nb] + acc[:, 4 * nb:]
        return carry

    if na % SUBLANE == 0:
        lax.fori_loop(0, n_e, per_expert, 0)
    else:
        for e in range(n_e):
            per_expert(e, 0)


def _route(aff_t, cap):
    bsz, n_e, n_tok = aff_t.shape
    nb = 16
    out = pl.BlockSpec((1, n_e * cap // nb, nb), lambda b: (b, 0, 0))
    idx, gates = pl.pallas_call(
        functools.partial(_topk_kernel, cap=cap, chunk=_tile(n_tok, 1024), nb=nb),
        grid=(bsz,), in_specs=[pl.BlockSpec((1, n_e, n_tok), lambda b: (b, 0, 0))],
        out_specs=[out, out],
        out_shape=[jax.ShapeDtypeStruct((bsz, n_e * cap // nb, nb), jnp.int32),
                   jax.ShapeDtypeStruct((bsz, n_e * cap // nb, nb), F32)],
        scratch_shapes=[pltpu.VMEM((n_e, n_tok), F32)],
        compiler_params=_cparams(("parallel",)),
        name="topk_route",
    )(aff_t)
    return idx.reshape(bsz, n_e, cap), gates.reshape(bsz, n_e, cap)


def kernel(x, c, ctx, c_ctx, mod_w, mod_b, norm1_g, w_in, q_a_g, w_q_b, kv_a_g, w_kv_b, conv_dw_w, conv_dw_b, conv_ln_g, conv_ln_b, hy_short_w, hy_short_b, hy_w1, hy_b1, hy_w2, hy_b2, hy_w3, hy_sin_freq, hy_decay, hy_skip, group_norm_g, w_out, norm2_g, router_w, w_gate, w_up, w_down, final_norm_g):
    p = dict(mod_w=mod_w, mod_b=mod_b, norm1_g=norm1_g, w_in=w_in, q_a_g=q_a_g, w_q_b=w_q_b,
             kv_a_g=kv_a_g, w_kv_b=w_kv_b, conv_dw_w=conv_dw_w, conv_dw_b=conv_dw_b,
             conv_ln_g=conv_ln_g, conv_ln_b=conv_ln_b, hy_short_w=hy_short_w,
             hy_short_b=hy_short_b, hy_w1=hy_w1, hy_b1=hy_b1, hy_w2=hy_w2, hy_b2=hy_b2,
             hy_w3=hy_w3, hy_sin_freq=hy_sin_freq, hy_decay=hy_decay, hy_skip=hy_skip,
             group_norm_g=group_norm_g, w_out=w_out, norm2_g=norm2_g, router_w=router_w,
             w_gate=w_gate, w_up=w_up, w_down=w_down)
    depth = mod_w.shape[0]
    bsz, n_lat, d = x.shape
    n_ctx = ctx.shape[1]

    rows = -(-(bsz + 1) // 8) * 8
    cc = jnp.concatenate([c, c_ctx[None, :], jnp.zeros((rows - bsz - 1, d), F32)], axis=0)
    mod_all = _modulation(cc, mod_w, mod_b)
    lat_row = lambda b: b
    ctx_row = lambda b: bsz

    tabs_l = _rope_tables(n_lat)
    tabs_c = _identity_tables(n_ctx)
    tm_l, tm_c = _tile(n_lat, 512), _tile(n_ctx, 256)
    tq_l, tq_c = _tile(n_lat, 256), _tile(n_ctx, 256)
    tl_l, tl_c = _tile(n_lat, 512), _tile(n_ctx, 256)

    xl, xc = x, ctx
    prev_l = prev_c = None
    for i in range(depth):
        last = i == depth - 1
        lw = _layer_weights(i, p)
        mod = mod_all[i].reshape(rows, 1, N_MOD * d)

        def side(xs, prev, row_of_b, tabs, tm, tq, tl, extra_keys, need_mix):
            xs, q, k, v, uc, uh = _inproj(xs, prev, mod, row_of_b, lw, tabs, tm)
            if not need_mix:
                return xs, (k, v), None
            att = _attention(q, [(k, v)] + extra_keys, tq)
            cn, hv, hx1, hx2 = _conv(uc, uh, lw, tl)
            plan = _fft_plan(xs.shape[1])
            mats = _fft_mats(plan)
            filt = _hyena_filter_spectrum(xs.shape[1], i, p, plan, mats)
            hy = _hyena(hv, hx1, hx2, p['hy_skip'][i], plan, mats, filt)
            x1, h2p, aff_t = _outproj(att, cn, hy, xs, mod, row_of_b, lw, tm)
            cap = CAPACITY_FACTOR * xs.shape[1] // N_EXPERTS
            idx, gates = _route(aff_t, cap)
            moe = _moe(h2p, idx, gates, lw)
            return x1, (k, v), moe

        xc, kv_c, moe_c = side(xc, prev_c, ctx_row, tabs_c, tm_c, tq_c, tl_c, [], not last)
        xl, _, moe_l = side(xl, prev_l, lat_row, tabs_l, tm_l, tq_l, tl_l, [kv_c], True)
        prev_l = (moe_l, mod)
        prev_c = None if moe_c is None else (moe_c, mod)

    mod = mod_all[depth - 1].reshape(rows, 1, N_MOD * d)
    return _final(xl, prev_l[0], mod, final_norm_g.reshape(1, d), tm_l)
```

```python
import functools
import math
from typing import NamedTuple

import numpy as np
import jax
import jax.numpy as jnp
from jax import lax
from jax.experimental import pallas as pl
from jax.experimental.pallas import tpu as pltpu

F32 = jnp.float32
BF16 = jnp.bfloat16
EPS = 1e-6

GRID_W = 64
N_MOD = 6
HEADS = 8
NOPE = 64
ROPE = 32
VDIM = 64
Q_LORA = 256
KV_LORA = 128
ROPE_BASE = 10000.0
CONV_CH = 256
CONV_K = 31
HY_CH = 256
HY_ORDER = 2
HY_BANDS = 16
N_EXPERTS = 16
CAPACITY_FACTOR = 2

LANE = 128
SUBLANE = 8
MXU_DIM = 256
HEAD_PAD = LANE
BF16_ROWS = 16
V_ROWS = -(-(VDIM + 1) // BF16_ROWS) * BF16_ROWS
HALO = 16
VMEM_LIMIT = 56 * 1024 * 1024
MOE_ACC_BUDGET = 16 * 1024 * 1024

C_Q = 0
C_KVN = C_Q + Q_LORA
C_KR = C_KVN + KV_LORA
C_CONV = C_KR + LANE
C_HY = C_CONV + 2 * CONV_CH
C_END = C_HY + 3 * HY_CH


def _cparams(sem):
    return pltpu.CompilerParams(dimension_semantics=sem, vmem_limit_bytes=VMEM_LIMIT)


def _rms(x, g):
    return x * lax.rsqrt(jnp.mean(x * x, axis=-1, keepdims=True) + EPS) * g


def _split(a):
    hi = a.astype(BF16)
    lo = (a - hi.astype(F32)).astype(BF16)
    return hi, lo


def _dot(a, b):
    return jnp.dot(a, b, preferred_element_type=F32)


def _from_token_major(ref, n_rows):
    fc = ref.shape[1] // n_rows
    return jnp.concatenate([ref[0, pl.ds(c, n_rows, stride=fc), :] for c in range(fc)], axis=1)


def _mod_kernel(c_ref, w_ref, b_ref, o_ref):
    c = c_ref[...]
    a = c * jax.nn.sigmoid(c)
    a_hi, a_lo = _split(a)
    w_hi, w_lo = _split(w_ref[0])
    o_ref[0] = _dot(a_hi, w_hi) + _dot(a_lo, w_hi) + _dot(a_hi, w_lo) + b_ref[0]


def _modulation(cc, mod_w, mod_b):
    depth, d, n = mod_w.shape
    rows = cc.shape[0]
    tn = 1536
    return pl.pallas_call(
        _mod_kernel,
        grid=(depth, n // tn),
        in_specs=[
            pl.BlockSpec((rows, d), lambda l, j: (0, 0)),
            pl.BlockSpec((1, d, tn), lambda l, j: (l, 0, j)),
            pl.BlockSpec((1, 1, tn), lambda l, j: (l, 0, j)),
        ],
        out_specs=pl.BlockSpec((1, rows, tn), lambda l, j: (l, 0, j)),
        out_shape=jax.ShapeDtypeStruct((depth, rows, n), F32),
        compiler_params=_cparams(("arbitrary", "arbitrary")),
        name="modulation",
    )(cc, mod_w, mod_b.reshape(depth, 1, n))


def _rope(x, c, s1, s2):
    return x * c + pltpu.roll(x, LANE - 8, 1) * s1 + pltpu.roll(x, 8, 1) * s2


def _inproj_kernel(*refs, fuse_prev, scale):
    if fuse_prev:
        xa_ref, xb_ref, g2_ref = refs[:3]
        refs = refs[3:]
    else:
        xa_ref = refs[0]
        refs = refs[1:]
    (sh_ref, sc_ref, n1g_ref, win_ref, qag_ref, wq_ref, kvg_ref, wk_ref, wv_ref,
     c_ref, s1_ref, s2_ref, vone_ref) = refs[:13]
    outs = refs[13:]
    if fuse_prev:
        x_out, q_out, k_out, v_out, uc_out, uh_out = outs
        x = xa_ref[0] + g2_ref[...] * _from_token_major(xb_ref, xa_ref.shape[1])
        x_out[0] = x
    else:
        q_out, k_out, v_out, uc_out, uh_out = outs
        x = xa_ref[0]
    hn = _rms(x, n1g_ref[...]) * (1.0 + sc_ref[...]) + sh_ref[...]
    u = _dot(hn.astype(BF16), win_ref[...])
    c, s1, s2 = c_ref[...], s1_ref[...], s2_ref[...]

    nq = _rms(u[:, C_Q:C_KVN], qag_ref[...]).astype(BF16)
    q = _dot(nq, wq_ref[...])
    for h in range(HEADS):
        sl = slice(h * HEAD_PAD, (h + 1) * HEAD_PAD)
        q_out[0, :, sl] = (_rope(q[:, sl], c, s1, s2) * scale).astype(BF16)

    kvn = _rms(u[:, C_KVN:C_KR], kvg_ref[...]).astype(BF16)
    kr = _rope(u[:, C_KR:C_CONV], c, s1, s2)
    k = _dot(kvn, wk_ref[...])
    for h in range(HEADS):
        sl = slice(h * HEAD_PAD, (h + 1) * HEAD_PAD)
        k_out[0, :, sl] = (k[:, sl] + kr).astype(BF16)
    vt = lax.dot_general(wv_ref[...], kvn, (((1,), (1,)), ((), ())), preferred_element_type=F32)
    v_out[0] = (vt + vone_ref[...]).astype(BF16)
    uc_out[0] = u[:, C_CONV:C_HY]
    uh_out[0] = u[:, C_HY:C_END]


def _inproj(x, prev, mod, row_of_b, lw, tabs, tm):
    bsz, n_tok, d = x.shape
    fuse_prev = prev is not None
    grid = (bsz, n_tok // tm)
    tok = lambda w: pl.BlockSpec((1, tm, w), lambda b, i: (b, i, 0))
    modspec = lambda k: pl.BlockSpec((None, 1, d), lambda b, i: (row_of_b(b), 0, k))
    full = lambda a: pl.BlockSpec(a.shape, lambda b, i: (0,) * a.ndim)
    tabspec = pl.BlockSpec((tm, LANE), lambda b, i: (i, 0))

    args, specs = [x], [tok(d)]
    if fuse_prev:
        args += [prev[0], prev[1]]
        specs += [pl.BlockSpec((1, tm * (d // LANE), LANE), lambda b, i: (b, i, 0)), modspec(5)]
    args += [mod, mod, lw['norm1_g'], lw['w_in'], lw['q_a_g'], lw['w_q'], lw['kv_a_g'],
             lw['w_k'], lw['w_v'], tabs[0], tabs[1], tabs[2], lw['v_one']]
    specs += [modspec(0), modspec(1), full(lw['norm1_g']), full(lw['w_in']), full(lw['q_a_g']),
              full(lw['w_q']), full(lw['kv_a_g']), full(lw['w_k']), full(lw['w_v']),
              tabspec, tabspec, tabspec, full(lw['v_one'])]
    hp = HEADS * HEAD_PAD
    out_shape, out_specs = [], []
    if fuse_prev:
        out_shape.append(jax.ShapeDtypeStruct((bsz, n_tok, d), F32))
        out_specs.append(tok(d))
    out_shape += [jax.ShapeDtypeStruct((bsz, n_tok, hp), BF16)] * 2
    out_specs += [tok(hp)] * 2
    out_shape += [jax.ShapeDtypeStruct((bsz, hp, n_tok), BF16),
                  jax.ShapeDtypeStruct((bsz, n_tok, 2 * CONV_CH), F32),
                  jax.ShapeDtypeStruct((bsz, n_tok, 3 * HY_CH), F32)]
    out_specs += [pl.BlockSpec((1, hp, tm), lambda b, i: (b, 0, i)),
                  tok(2 * CONV_CH), tok(3 * HY_CH)]
    scale = float((NOPE + ROPE) ** -0.5 * math.log2(math.e))
    res = pl.pallas_call(
        functools.partial(_inproj_kernel, fuse_prev=fuse_prev, scale=scale),
        grid=grid, in_specs=specs, out_specs=out_specs, out_shape=out_shape,
        compiler_params=_cparams(("parallel", "parallel")),
        name="inproj",
    )(*args)
    if not fuse_prev:
        res = [x] + list(res)
    return res


def _attn_kernel(*refs, n_sets, hps, tk, ahead):
    q_ref = refs[0]
    kv = refs[1:1 + 2 * n_sets]
    o_ref = refs[1 + 2 * n_sets]
    nt = (((1,), (1,)), ((), ()))
    sls = [slice(hh * HEAD_PAD, (hh + 1) * HEAD_PAD) for hh in range(hps)]
    qs = [q_ref[0, :, sl] for sl in sls]
    items = [(hh, i, c) for i in range(n_sets) for c in range(kv[2 * i].shape[1] // tk)
             for hh in range(hps)]

    def scores(item):
        hh, i, c = item
        return lax.dot_general(kv[2 * i][0, c * tk:(c + 1) * tk, sls[hh]], qs[hh], nt,
                               preferred_element_type=F32)

    m, o = [None] * hps, [None] * hps
    pending = [scores(it) for it in items[:ahead]]
    for n, (hh, i, c) in enumerate(items):
        s = pending.pop(0)
        if n + ahead < len(items):
            pending.append(scores(items[n + ahead]))
        mt = jnp.max(s, axis=0, keepdims=True)
        m_new = mt if m[hh] is None else jnp.maximum(m[hh], mt)
        pv = _dot(kv[2 * i + 1][0, hh * HEAD_PAD:hh * HEAD_PAD + V_ROWS, c * tk:(c + 1) * tk],
                  jnp.exp2(s - m_new).astype(BF16))
        o[hh] = pv if o[hh] is None else o[hh] * jnp.exp2(m[hh] - m_new) + pv
        m[hh] = m_new
    pad = jnp.zeros((HEAD_PAD - V_ROWS, o[0].shape[1]), F32)
    outs = [jnp.concatenate([oh * (1.0 / oh[VDIM:VDIM + 1, :]), pad], axis=0).T for oh in o]
    lane = lax.broadcasted_iota(jnp.int32, outs[0].shape, 1)
    for pr in range(hps // 2):
        o_ref[0, :, pr * LANE:(pr + 1) * LANE] = jnp.where(
            lane < VDIM, outs[2 * pr], pltpu.roll(outs[2 * pr + 1], VDIM, 1))


def _attention(q, key_sets, tq):
    bsz, n_q, hp = q.shape
    hps = 8
    grid = (bsz, HEADS // hps, n_q // tq)
    args = [q]
    specs = [pl.BlockSpec((1, tq, hps * HEAD_PAD), lambda b, h, i: (b, i, h))]
    for k, v in key_sets:
        n_k = k.shape[1]
        args += [k, v]
        specs += [pl.BlockSpec((1, n_k, hps * HEAD_PAD), lambda b, h, i: (b, 0, h)),
                  pl.BlockSpec((1, hps * HEAD_PAD, n_k), lambda b, h, i: (b, h, 0))]
    return pl.pallas_call(
        functools.partial(_attn_kernel, n_sets=len(key_sets), hps=hps,
                          tk=min([MXU_DIM] + [k.shape[1] for k, _ in key_sets]), ahead=8),
        grid=grid, in_specs=specs,
        out_specs=pl.BlockSpec((1, tq, hps * VDIM), lambda b, h, i: (b, i, h)),
        out_shape=jax.ShapeDtypeStruct((bsz, n_q, HEADS * VDIM), F32),
        compiler_params=_cparams(("parallel", "parallel", "arbitrary")),
        name="attention",
    )(*args)


def _conv_kernel(ucp_ref, uc_ref, ucn_ref, uhp_ref, uh_ref, uhn_ref,
                 cw_ref, cb_ref, lg_ref, lb_ref, gn_ref, hw_ref, hb_ref,
                 cn_out, v_out, x1_out, x2_out, ypad, hpad, yph, *, tl, row_tile):
    i = pl.program_id(1)
    has_prev = (i > 0).astype(F32)
    has_next = (i < pl.num_programs(1) - 1).astype(F32)

    def glu(u):
        return u[:, :CONV_CH] * jax.nn.sigmoid(u[:, CONV_CH:])

    ypad[0:HALO, :] = glu(ucp_ref[0]) * has_prev
    ypad[HALO:HALO + tl, :] = glu(uc_ref[0])
    ypad[HALO + tl:, :] = glu(ucn_ref[0]) * has_next
    hpad[0:HALO, :] = uhp_ref[0] * has_prev
    hpad[HALO:HALO + tl, :] = uh_ref[0]
    hpad[HALO + tl:, :] = uhn_ref[0] * has_next

    span = tl + 2 * HALO - SUBLANE
    for ph in range(SUBLANE):
        yph[ph] = ypad[ph:ph + span, :]

    half = CONV_K // 2
    for r in range(tl // row_tile):
        base = r * row_tile
        acc = jnp.zeros((row_tile, CONV_CH), F32) + cb_ref[...]
        for k in range(CONV_K):
            off = HALO + base + k - half
            ph = off % SUBLANE
            acc = acc + cw_ref[k:k + 1, :] * yph[ph, off - ph:off - ph + row_tile, :]
        mu = jnp.mean(acc, axis=-1, keepdims=True)
        cen = acc - mu
        var = jnp.mean(cen * cen, axis=-1, keepdims=True)
        y = cen * lax.rsqrt(var + EPS) * lg_ref[...] + lb_ref[...]
        y = y * jax.nn.sigmoid(y)
        cn_out[0, base:base + row_tile, :] = _rms(y, gn_ref[...])

        z = jnp.zeros((row_tile, 3 * HY_CH), F32) + hb_ref[...]
        for k in range(3):
            off = HALO + base + k - 1
            z = z + hw_ref[k:k + 1, :] * hpad[off:off + row_tile, :]
        v_out[0, base:base + row_tile, :] = z[:, :HY_CH]
        x1_out[0, base:base + row_tile, :] = z[:, HY_CH:2 * HY_CH]
        x2_out[0, base:base + row_tile, :] = z[:, 2 * HY_CH:]


def _conv(uc, uh, lw, tl):
    bsz, n_tok, _ = uc.shape
    nh = tl // HALO
    last = n_tok // HALO - 1
    cur = lambda w: pl.BlockSpec((1, tl, w), lambda b, i: (b, i, 0))
    prv = lambda w: pl.BlockSpec((1, HALO, w), lambda b, i: (b, jnp.maximum(i * nh - 1, 0), 0))
    nxt = lambda w: pl.BlockSpec((1, HALO, w), lambda b, i: (b, jnp.minimum((i + 1) * nh, last), 0))
    full = lambda a: pl.BlockSpec(a.shape, lambda b, i: (0,) * a.ndim)
    wts = [lw['conv_dw_w'], lw['conv_dw_b'], lw['conv_ln_g'], lw['conv_ln_b'], lw['gn_conv'],
           lw['hy_short_w'], lw['hy_short_b']]
    return pl.pallas_call(
        functools.partial(_conv_kernel, tl=tl, row_tile=min(64, tl)),
        grid=(bsz, n_tok // tl),
        in_specs=[prv(2 * CONV_CH), cur(2 * CONV_CH), nxt(2 * CONV_CH),
                  prv(3 * HY_CH), cur(3 * HY_CH), nxt(3 * HY_CH)] + [full(w) for w in wts],
        out_specs=[cur(CONV_CH)] + [cur(HY_CH)] * 3,
        out_shape=[jax.ShapeDtypeStruct((bsz, n_tok, CONV_CH), F32)]
        + [jax.ShapeDtypeStruct((bsz, n_tok, HY_CH), F32)] * 3,
        scratch_shapes=[pltpu.VMEM((tl + 2 * HALO, CONV_CH), F32),
                        pltpu.VMEM((tl + 2 * HALO, 3 * HY_CH), F32),
                        pltpu.VMEM((SUBLANE, tl + 2 * HALO - SUBLANE, CONV_CH), F32)],
        compiler_params=_cparams(("parallel", "parallel")),
        name="conv",
    )(uc, uc, uc, uh, uh, uh, *wts)


def _outproj_kernel(att_ref, cn_ref, *refs, n_hy):
    hy_refs = refs[:n_hy]
    (x_ref, g1_ref, sh_ref, sc_ref, gna_ref, gnh_ref, wo_ref, n2g_ref, rw_ref,
     x1_out, h2p_out, aff_out) = refs[n_hy:]
    a = _rms(att_ref[0], gna_ref[...]).astype(BF16)
    c = cn_ref[0].astype(BF16)
    h = _rms(_lanes(hy_refs, 0), gnh_ref[...]).astype(BF16)
    na, nc = a.shape[1], c.shape[1]
    y = (_dot(a, wo_ref[0:na, :]) + _dot(c, wo_ref[na:na + nc, :]) + _dot(h, wo_ref[na + nc:, :]))
    x1 = x_ref[0] + g1_ref[...] * y
    x1_out[0] = x1
    h2 = _rms(x1, n2g_ref[...]) * (1.0 + sc_ref[...]) + sh_ref[...]
    d = h2.shape[1]
    lo = lax.bitcast_convert_type(h2[:, :d // 2].astype(BF16).astype(F32), jnp.uint32)
    hi = lax.bitcast_convert_type(h2[:, d // 2:].astype(BF16).astype(F32), jnp.uint32)
    packed = hi | (lo >> 16)
    n_rows, pc = packed.shape[0], packed.shape[1] // LANE
    for c in range(pc):
        h2p_out[0, pl.ds(c, n_rows, stride=pc), :] = packed[:, c * LANE:(c + 1) * LANE]
    h_hi, h_lo = _split(h2)
    t = _dot(h_hi, rw_ref[...])
    logits = t[:, :LANE] + t[:, LANE:] + _dot(h_lo, rw_ref[:, :LANE])
    lane = lax.broadcasted_iota(jnp.int32, logits.shape, 1)
    logits = jnp.where(lane < N_EXPERTS, logits, -1e30)
    e = jnp.exp(logits - jnp.max(logits, axis=1, keepdims=True))
    aff = e / jnp.sum(e, axis=1, keepdims=True)
    aff_out[0] = aff.T[:N_EXPERTS, :]


def _outproj(att, cn, hy, x, mod, row_of_b, lw, tm):
    bsz, n_tok, d = x.shape
    tok = lambda w: pl.BlockSpec((1, tm, w), lambda b, i: (b, i, 0))
    modspec = lambda k: pl.BlockSpec((None, 1, d), lambda b, i: (row_of_b(b), 0, k))
    full = lambda a: pl.BlockSpec(a.shape, lambda b, i: (0,) * a.ndim)
    wts = [lw['gn_att'], lw['gn_hy'], lw['w_out'], lw['norm2_g'], lw['router_w']]
    pc = d // 2 // LANE
    return pl.pallas_call(
        functools.partial(_outproj_kernel, n_hy=len(hy)),
        grid=(bsz, n_tok // tm),
        in_specs=[tok(att.shape[2]), tok(cn.shape[2])] + [tok(LANE)] * len(hy) + [tok(d),
                  modspec(2), modspec(3), modspec(4)] + [full(w) for w in wts],
        out_specs=[tok(d), pl.BlockSpec((1, tm * pc, LANE), lambda b, i: (b, i, 0)),
                   pl.BlockSpec((1, N_EXPERTS, tm), lambda b, i: (b, 0, i))],
        out_shape=[jax.ShapeDtypeStruct((bsz, n_tok, d), F32),
                   jax.ShapeDtypeStruct((bsz, n_tok * pc, LANE), jnp.uint32),
                   jax.ShapeDtypeStruct((bsz, N_EXPERTS, n_tok), F32)],
        compiler_params=_cparams(("parallel", "parallel")),
        name="outproj",
    )(att, cn, *hy, x, mod, mod, mod, *wts)


def _moe_kernel(idx_ref, gate_ref, h2p_ref, wg_ref, wu_ref, wd_ref, out_hbm,
                acc_ref, xg_ref, y_ref, sem, *, cap, pc, fc, expert_major):
    if expert_major:
        e, b = pl.program_id(0), pl.program_id(1)
        n_e, slot = pl.num_programs(0), b
    else:
        b, e = pl.program_id(0), pl.program_id(1)
        n_e, slot = pl.num_programs(1), 0

    @pl.when(e == 0)
    def _():
        acc_ref[slot] = jnp.zeros(acc_ref.shape[1:], F32)

    group = math.gcd(cap, SUBLANE)

    def gather(jg, carry):
        js = [jg * group + u for u in range(group)]
        rows = [h2p_ref[0, pl.ds(pl.multiple_of(idx_ref[0, 0, j] * pc, pc), pc), :] for j in js]
        for j, row in zip(js, rows):
            xg_ref[pl.ds(pl.multiple_of(j * pc, pc), pc), :] = row
        return carry
    lax.fori_loop(0, cap // group, gather, 0)

    lo, hi = [], []
    for c in range(pc):
        w = xg_ref[pl.ds(c, cap, stride=pc), :]
        lo.append(lax.bitcast_convert_type(w << 16, F32).astype(BF16))
        hi.append(lax.bitcast_convert_type(w & jnp.uint32(0xFFFF0000), F32).astype(BF16))
    x = jnp.concatenate(lo + hi, axis=1)
    a = _dot(x, wg_ref[0])
    u = _dot(x, wu_ref[0])
    hmid = (a * jax.nn.sigmoid(a) * u).astype(BF16)
    y = _dot(hmid, wd_ref[0])
    for c in range(fc):
        y_ref[pl.ds(c, cap, stride=fc), :] = y[:, c * LANE:(c + 1) * LANE]

    def scatter(jg, carry):
        js = [jg * group + u for u in range(group)]
        dsts = [pl.ds(pl.multiple_of(idx_ref[0, 0, j] * fc, fc), fc) for j in js]
        new = [acc_ref[slot, dst, :] + gate_ref[0, 0, j] * y_ref[pl.ds(pl.multiple_of(j * fc, fc), fc), :]
               for j, dst in zip(js, dsts)]
        for dst, val in zip(dsts, new):
            acc_ref[slot, dst, :] = val
        return carry
    lax.fori_loop(0, cap // group, scatter, 0)

    @pl.when(e == n_e - 1)
    def _():
        cp = pltpu.make_async_copy(acc_ref.at[slot], out_hbm.at[b], sem)
        cp.start()
        cp.wait()


def _moe(h2p, idx, gates, lw):
    d = lw['w_gate'].shape[1]
    pc, fc = d // 2 // LANE, d // LANE
    bsz, n_tok = h2p.shape[0], h2p.shape[1] // pc
    n_e, cap = idx.shape[1], idx.shape[2]
    ff = lw['w_gate'].shape[2]
    acc_bytes = n_tok * d * 4
    expert_major = bsz * acc_bytes <= MOE_ACC_BUDGET
    be = (lambda f: (lambda e, b: f(b, e))) if expert_major else (lambda f: f)
    smem = lambda: pl.BlockSpec((1, 1, cap), be(lambda b, e: (b * n_e + e, 0, 0)),
                                memory_space=pltpu.SMEM)
    return pl.pallas_call(
        functools.partial(_moe_kernel, cap=cap, pc=pc, fc=fc, expert_major=expert_major),
        grid=(n_e, bsz) if expert_major else (bsz, n_e),
        in_specs=[smem(), smem(),
                  pl.BlockSpec((1, n_tok * pc, LANE), be(lambda b, e: (b, 0, 0))),
                  pl.BlockSpec((1, d, ff), be(lambda b, e: (e, 0, 0))),
                  pl.BlockSpec((1, d, ff), be(lambda b, e: (e, 0, 0))),
                  pl.BlockSpec((1, ff, d), be(lambda b, e: (e, 0, 0)))],
        out_specs=pl.BlockSpec(memory_space=pl.ANY),
        out_shape=jax.ShapeDtypeStruct((bsz, n_tok * fc, LANE), F32),
        scratch_shapes=[pltpu.VMEM((bsz if expert_major else 1, n_tok * fc, LANE), F32),
                        pltpu.VMEM((cap * pc, LANE), jnp.uint32),
                        pltpu.VMEM((cap * fc, LANE), F32),
                        pltpu.SemaphoreType.DMA(())],
        compiler_params=_cparams(("arbitrary", "arbitrary")),
        name="moe",
    )(idx.reshape(bsz * n_e, 1, cap), gates.reshape(bsz * n_e, 1, cap),
      h2p, lw['w_gate'], lw['w_up'], lw['w_down'])


def _final_kernel(x_ref, m_ref, g2_ref, fg_ref, o_ref):
    o_ref[0] = _rms(x_ref[0] + g2_ref[...] * _from_token_major(m_ref, x_ref.shape[1]), fg_ref[...])


def _final(x1, moe, mod, fg, tm):
    bsz, n_tok, d = x1.shape
    tok = pl.BlockSpec((1, tm, d), lambda b, i: (b, i, 0))
    return pl.pallas_call(
        _final_kernel,
        grid=(bsz, n_tok // tm),
        in_specs=[tok, pl.BlockSpec((1, tm * (d // LANE), LANE), lambda b, i: (b, i, 0)),
                  pl.BlockSpec((None, 1, d), lambda b, i: (b, 0, 5)),
                  pl.BlockSpec((1, d), lambda b, i: (0, 0))],
        out_specs=tok,
        out_shape=jax.ShapeDtypeStruct((bsz, n_tok, d), F32),
        compiler_params=_cparams(("parallel", "parallel")),
        name="final_norm",
    )(x1, moe, mod, fg)


def _rope_tables(n_tok):
    rows = n_tok // GRID_W
    row = jnp.repeat(jnp.arange(rows, dtype=F32), GRID_W)
    col = jnp.tile(jnp.arange(GRID_W, dtype=F32), rows)
    half = ROPE // 2
    inv = ROPE_BASE ** (-jnp.arange(0, half, 2, dtype=F32) / half)
    cr, sr = jnp.cos(row[:, None] * inv), jnp.sin(row[:, None] * inv)
    cc, sc = jnp.cos(col[:, None] * inv), jnp.sin(col[:, None] * inv)
    z8 = jnp.zeros_like(cr)
    ones = jnp.ones((n_tok, NOPE), F32)
    pad = jnp.zeros((n_tok, HEAD_PAD - NOPE - ROPE), F32)
    c = jnp.concatenate([ones, cr, cr, cc, cc, pad + 1.0], axis=1)
    s1 = jnp.concatenate([ones * 0.0, -sr, z8, -sc, z8, pad], axis=1)
    s2 = jnp.concatenate([ones * 0.0, z8, sr, z8, sc, pad], axis=1)
    return c, s1, s2


def _identity_tables(n_tok):
    return (jnp.ones((n_tok, HEAD_PAD), F32), jnp.zeros((n_tok, HEAD_PAD), F32),
            jnp.zeros((n_tok, HEAD_PAD), F32))


def _pad_heads(w, per_head, take):
    k = w.shape[0]
    w = w.reshape(k, HEADS, per_head)[:, :, take]
    w = jnp.pad(w, ((0, 0), (0, 0), (0, HEAD_PAD - w.shape[2])))
    return w.reshape(k, HEADS * HEAD_PAD).astype(BF16)


def _layer_weights(i, p):
    d = p['w_in'].shape[1]
    w_in = p['w_in'][i]
    off_kv = Q_LORA
    off_conv = off_kv + KV_LORA + ROPE
    off_hy = off_conv + 2 * CONV_CH
    kr = jnp.zeros((d, LANE), F32).at[:, NOPE:NOPE + ROPE].set(w_in[:, off_kv + KV_LORA:off_conv])
    w_in_r = jnp.concatenate([w_in[:, :off_kv], w_in[:, off_kv:off_kv + KV_LORA], kr,
                              w_in[:, off_conv:off_hy], w_in[:, off_hy:]], axis=1).astype(BF16)
    gn = p['group_norm_g'][i]
    n_att = HEADS * VDIM
    rw_hi, rw_lo = _split(jnp.pad(p['router_w'][i], ((0, 0), (0, LANE - N_EXPERTS))))
    v_one = jnp.zeros((HEADS, HEAD_PAD), F32).at[:, VDIM].set(1.0).reshape(HEADS * HEAD_PAD, 1)
    row = lambda a: a.reshape(1, -1)
    return {
        'norm1_g': row(p['norm1_g'][i]), 'w_in': w_in_r,
        'q_a_g': row(p['q_a_g'][i]), 'w_q': _pad_heads(p['w_q_b'][i], NOPE + ROPE, slice(None)),
        'kv_a_g': row(p['kv_a_g'][i]),
        'w_k': _pad_heads(p['w_kv_b'][i], NOPE + VDIM, slice(0, NOPE)),
        'w_v': _pad_heads(p['w_kv_b'][i], NOPE + VDIM, slice(NOPE, NOPE + VDIM)).T,
        'v_one': v_one,
        'conv_dw_w': p['conv_dw_w'][i], 'conv_dw_b': row(p['conv_dw_b'][i]),
        'conv_ln_g': row(p['conv_ln_g'][i]), 'conv_ln_b': row(p['conv_ln_b'][i]),
        'gn_att': row(gn[:n_att]), 'gn_conv': row(gn[n_att:n_att + CONV_CH]),
        'gn_hy': row(gn[n_att + CONV_CH:]),
        'hy_short_w': p['hy_short_w'][i], 'hy_short_b': row(p['hy_short_b'][i]),
        'w_out': p['w_out'][i].astype(BF16), 'norm2_g': row(p['norm2_g'][i]),
        'router_w': jnp.concatenate([rw_hi, rw_lo], axis=1),
        'w_gate': p['w_gate'][i].astype(BF16), 'w_up': p['w_up'][i].astype(BF16),
        'w_down': p['w_down'][i].astype(BF16),
    }


def _tile(n, pref):
    return pref if n % pref == 0 else n


def _dot3(a, b):
    a_hi, a_lo = _split(a)
    b_hi, b_lo = _split(b)
    return _dot(a_hi, b_hi) + _dot(a_lo, b_hi) + _dot(a_hi, b_lo)


class _FftPlan(NamedTuple):
    n2: int
    nh: int
    k1p: int
    f1: np.ndarray
    g1: np.ndarray
    mf: np.ndarray
    mi: np.ndarray


@functools.lru_cache(maxsize=None)
def _fft_plan(n_tok):
    n = 2 * n_tok
    n2 = 64 if n_tok >= 2048 else 16
    n1 = n // n2
    nh = n1 // 2
    k1 = nh + 1
    k1p = -(-k1 // SUBLANE) * SUBLANE
    two_pi = 2.0 * np.pi
    r = np.arange(k1)
    ang1 = two_pi * ((np.arange(nh)[None, :] * r[:, None]) % n1) / n1
    f1 = np.zeros((2 * k1p, nh))
    f1[:k1], f1[k1p:k1p + k1] = np.cos(ang1), -np.sin(ang1)
    w = np.where((r == 0) | (r == nh), 1.0, 2.0)[None, :] / n
    g1 = np.zeros((nh, 2 * k1p))
    g1[:, :k1], g1[:, k1p:k1p + k1] = np.cos(ang1.T) * w, -np.sin(ang1.T) * w
    k = r[:, None, None] + n1 * np.arange(n2)[None, :, None]
    th = two_pi * ((k * np.arange(n2)[None, None, :]) % n) / n
    tc, ts = np.cos(th), -np.sin(th)
    mf = np.zeros((k1p, 2 * n2, 2 * n2))
    mi = np.zeros((k1p, 2 * n2, 2 * n2))
    mf[:k1, :n2, :n2], mf[:k1, :n2, n2:], mf[:k1, n2:, :n2], mf[:k1, n2:, n2:] = tc, -ts, ts, tc
    tct, tst = tc.transpose(0, 2, 1), ts.transpose(0, 2, 1)
    mi[:k1, :n2, :n2], mi[:k1, :n2, n2:], mi[:k1, n2:, :n2], mi[:k1, n2:, n2:] = tct, tst, -tst, tct
    return _FftPlan(n2, nh, k1p, f1, g1, mf, mi)


def _lanes(refs, index):
    return jnp.concatenate([r[index] for r in refs], axis=1)


def _put_lanes(refs, index, val):
    for t, r in enumerate(refs):
        r[index] = val[:, t * LANE:(t + 1) * LANE]


def _stage1(x_refs, lead, f_ref, a_refs, nh, n2, rows, mm):
    def body(j, carry):
        x = _lanes(x_refs, (lead, pl.ds(j, nh, stride=n2), slice(None)))
        _put_lanes(a_refs, (pl.ds(pl.multiple_of(j * rows, SUBLANE), rows), slice(None)),
                   mm(f_ref[...], x))
        return carry
    lax.fori_loop(0, n2, body, 0, unroll=8)


def _spectrum_rows(a_refs, k, k1p, n2, rows):
    return jnp.concatenate([_lanes(a_refs, (pl.ds(k, n2, stride=rows), slice(None))),
                            _lanes(a_refs, (pl.ds(k1p + k, n2, stride=rows), slice(None)))], axis=0)


def _hyena_conv_kernel(*refs, nt, nh, n2, k1p):
    u_refs, gate_refs = refs[:nt], refs[nt:2 * nt]
    skip_ref, f1_ref, g1_ref, mf_ref, mi_ref, g_ref = refs[2 * nt:2 * nt + 6]
    y_outs, a_refs = refs[2 * nt + 6:3 * nt + 6], refs[3 * nt + 6:]
    rows = 2 * k1p
    bdot = lambda f, x: _dot(f, x.astype(BF16))
    _stage1(u_refs, 0, f1_ref, a_refs, nh, n2, rows, bdot)

    def stage2(k, carry):
        y = _dot(mf_ref[k], _spectrum_rows(a_refs, k, k1p, n2, rows).astype(BF16))
        yr, yi = y[:n2], y[n2:]
        gr, gi = g_ref[0, 0, k], g_ref[0, 1, k]
        z = jnp.concatenate([yr * gr - yi * gi, yr * gi + yi * gr], axis=0).astype(BF16)
        v = _dot(mi_ref[k], z)
        _put_lanes(a_refs, (pl.ds(k, n2, stride=rows), slice(None)), v[:n2])
        _put_lanes(a_refs, (pl.ds(k1p + k, n2, stride=rows), slice(None)), v[n2:])
        return carry
    lax.fori_loop(0, k1p, stage2, 0, unroll=8)

    def inverse1(j, carry):
        v = _lanes(a_refs, (pl.ds(pl.multiple_of(j * rows, SUBLANE), rows), slice(None))).astype(BF16)
        _put_lanes(y_outs, (0, pl.ds(j, nh, stride=n2), slice(None)), _dot(g1_ref[...], v))
        return carry
    lax.fori_loop(0, n2, inverse1, 0, unroll=8)

    n_tok = y_outs[0].shape[1]
    step = math.gcd(n_tok, 512)

    def epilogue(i, carry):
        sl = (0, pl.ds(pl.multiple_of(i * step, step), step), slice(None))
        for t in range(nt):
            y_outs[t][sl] = gate_refs[t][sl] * (
                y_outs[t][sl] + u_refs[t][sl] * skip_ref[:, t * LANE:(t + 1) * LANE])
        return carry
    lax.fori_loop(0, n_tok // step, epilogue, 0)


def _lane_tiles(x):
    if isinstance(x, (list, tuple)):
        return [(a, 0) for a in x]
    return [(x, t) for t in range(x.shape[2] // LANE)]


def _hyena_conv(u, gate, skip, g, order, plan, mats):
    ut, gt = _lane_tiles(u), _lane_tiles(gate)
    nt = len(ut)
    bsz, n_tok = ut[0][0].shape[:2]
    ch = nt * LANE
    nh, n2, k1p = plan.nh, plan.n2, plan.k1p
    rows = 2 * k1p
    tok = lambda t: pl.BlockSpec((1, n_tok, LANE), lambda b: (b, 0, t))
    const = lambda a: pl.BlockSpec(a.shape, lambda b: (0,) * a.ndim, pipeline_mode=pl.Buffered(1))
    return pl.pallas_call(
        functools.partial(_hyena_conv_kernel, nt=nt, nh=nh, n2=n2, k1p=k1p),
        grid=(bsz,),
        in_specs=[tok(t) for _, t in ut] + [tok(t) for _, t in gt]
        + [pl.BlockSpec((None, 1, ch), lambda b: (order, 0, 0)),
           const(mats['f1']), const(mats['g1']), const(mats['mf']), const(mats['mi']),
           pl.BlockSpec((1, 2, k1p, n2, ch), lambda b: (order, 0, 0, 0, 0),
                        pipeline_mode=pl.Buffered(1))],
        out_specs=[tok(0)] * nt,
        out_shape=[jax.ShapeDtypeStruct((bsz, n_tok, LANE), F32)] * nt,
        scratch_shapes=[pltpu.VMEM((n2 * rows, LANE), F32)] * nt,
        compiler_params=_cparams(("parallel",)),
        name="hyena_conv",
    )(*[a for a, _ in ut], *[a for a, _ in gt], skip.reshape(skip.shape[0], 1, ch),
      mats['f1'], mats['g1'], mats['mf'], mats['mi'], g)


def _filter_spectrum_kernel(h_ref, ss_ref, f1_ref, mf_ref, g_out, a_ref, *, nh, n2, k1p):
    rows = 2 * k1p
    s = lax.rsqrt(ss_ref[0] + ss_ref[1] + EPS)
    for direction in range(2):
        _stage1([h_ref], direction, f1_ref, [a_ref], nh, n2, rows, _dot3)

        def stage2(k, carry):
            y = _dot3(mf_ref[k], _spectrum_rows([a_ref], k, k1p, n2, rows))
            if direction == 0:
                g_out[0, 0, k] = y[:n2] * s
                g_out[0, 1, k] = y[n2:] * s
            else:
                g_out[0, 0, k] = g_out[0, 0, k] + y[:n2] * s
                g_out[0, 1, k] = g_out[0, 1, k] - y[n2:] * s
            return carry
        lax.fori_loop(0, k1p, stage2, 0)


def _filter_spectrum(h, ss, plan, mats):
    groups, n_tok, ch = h.shape
    nh, n2, k1p = plan.nh, plan.n2, plan.k1p
    full = lambda a: pl.BlockSpec(a.shape, lambda o, ci: (0,) * a.ndim)
    return pl.pallas_call(
        functools.partial(_filter_spectrum_kernel, nh=nh, n2=n2, k1p=k1p),
        grid=(groups // 2, ch // LANE),
        in_specs=[pl.BlockSpec((2, n_tok, LANE), lambda o, ci: (o, 0, ci)),
                  pl.BlockSpec((2, 1, LANE), lambda o, ci: (o, 0, ci)),
                  full(mats['f1_32']), full(mats['mf32'])],
        out_specs=pl.BlockSpec((1, 2, k1p, n2, LANE), lambda o, ci: (o, 0, 0, 0, ci)),
        out_shape=jax.ShapeDtypeStruct((groups // 2, 2, k1p, n2, ch), F32),
        scratch_shapes=[pltpu.VMEM((n2 * 2 * k1p, LANE), F32)],
        compiler_params=_cparams(("arbitrary", "arbitrary")),
        name="hyena_filter_spectrum",
    )(h, ss, mats['f1_32'], mats['mf32'])


def _filt_kernel(z_ref, w1_ref, b1_ref, w2_ref, b2_ref, fr_ref, w3_ref, dec_ref, h_out, ss_out, *, tl):
    i = pl.program_id(0)
    z = z_ref[...]
    h = jnp.sin(fr_ref[0:1, :] * (_dot3(z, w1_ref[...]) + b1_ref[...]))
    h = jnp.sin(fr_ref[1:2, :] * (_dot3(h, w2_ref[...]) + b2_ref[...]))
    h = _dot3(h, w3_ref[...])
    decay = jnp.exp(-z[:, 0:1] * jnp.abs(dec_ref[...]))
    row = i * tl + lax.broadcasted_iota(jnp.int32, (tl, 1), 0)

    @pl.when(i == 0)
    def _():
        ss_out[...] = jnp.zeros_like(ss_out)

    for g in range(2 * HY_ORDER):
        hg = h[:, g * HY_CH:(g + 1) * HY_CH] * decay
        if g % 2 == 1:
            hg = jnp.where(row > 0, hg, 0.0)
        h_out[g] = hg
        ss_out[g] += jnp.sum(hg * hg, axis=0, keepdims=True)


def _hyena_filter_spectrum(n_tok, i, p, plan, mats):
    t = jnp.linspace(0.0, 1.0, n_tok, dtype=F32)[:, None]
    w = 2.0 * math.pi * jnp.arange(n_tok, dtype=F32) / n_tok
    f = jnp.linspace(1e-4, HY_BANDS - 1, HY_BANDS, dtype=F32)
    fw = w[:, None] * f[None, :]
    feat = jnp.concatenate([t, jnp.cos(fw), -jnp.sin(fw)], axis=-1)
    n_feat = feat.shape[1]
    hid = p['hy_w1'].shape[2]
    feat = jnp.pad(feat, ((0, 0), (0, hid - n_feat)))
    w1 = jnp.pad(p['hy_w1'][i], ((0, hid - n_feat), (0, 0)))
    tl = _tile(n_tok, 512)
    groups = 2 * HY_ORDER
    full = lambda a: pl.BlockSpec(a.shape, lambda s: (0,) * a.ndim)
    wts = [w1, p['hy_b1'][i][None], p['hy_w2'][i], p['hy_b2'][i][None], p['hy_sin_freq'][i],
           p['hy_w3'][i], p['hy_decay'][i][None]]
    h, ss = pl.pallas_call(
        functools.partial(_filt_kernel, tl=tl),
        grid=(n_tok // tl,),
        in_specs=[pl.BlockSpec((tl, hid), lambda s: (s, 0))] + [full(a) for a in wts],
        out_specs=[pl.BlockSpec((groups, tl, HY_CH), lambda s: (0, s, 0)),
                   pl.BlockSpec((groups, 1, HY_CH), lambda s: (0, 0, 0))],
        out_shape=[jax.ShapeDtypeStruct((groups, n_tok, HY_CH), F32),
                   jax.ShapeDtypeStruct((groups, 1, HY_CH), F32)],
        compiler_params=_cparams(("arbitrary",)),
        name="hyena_filter_mlp",
    )(feat, *wts)
    return _filter_spectrum(h, ss, plan, mats)


def _hyena(v, x1, x2, skip, plan, mats, g):
    y = v
    for o, gate in enumerate((x1, x2)):
        y = _hyena_conv(y, gate, skip, g, o, plan, mats)
    return y


def _fft_mats(plan):
    return {'f1': jnp.asarray(plan.f1, BF16), 'f1_32': jnp.asarray(plan.f1, F32),
            'g1': jnp.asarray(plan.g1, BF16), 'mf': jnp.asarray(plan.mf, BF16),
            'mf32': jnp.asarray(plan.mf, F32), 'mi': jnp.asarray(plan.mi, BF16)}


def _cumsum_lanes(x):
    lane = lax.broadcasted_iota(jnp.int32, x.shape, 1)
    s = 1
    while s < x.shape[1]:
        x = x + jnp.where(lane >= s, pltpu.roll(x, s, 1), 0.0)
        s *= 2
    return x


def _topk_kernel(aff_ref, idx_out, gate_out, pos_ref, *, cap, chunk, nb):
    a = aff_ref[0]
    n_e, n_tok = a.shape
    bits = lax.bitcast_convert_type(a, jnp.int32)

    def count(mask):
        return jnp.sum(jnp.where(mask, 1.0, 0.0), axis=1, keepdims=True)

    def bisect(_, lohi):
        lo, hi = lohi
        mid = lo + ((hi - lo) >> 1)
        ok = count(bits >= mid) >= cap
        return jnp.where(ok, mid, lo), jnp.where(ok, hi, mid)

    lo0 = jnp.zeros((n_e, 1), jnp.int32)
    hi0 = jnp.full((n_e, 1), 0x7F800000, jnp.int32)
    thr, _ = lax.fori_loop(0, 31, bisect, (lo0, hi0))
    gt = bits > thr
    eqf = jnp.where(bits == thr, 1.0, 0.0)
    need = cap - count(gt)
    rank = _cumsum_lanes(eqf) - eqf
    self_ = jnp.where(gt, 1.0, jnp.where(rank < need, eqf, 0.0))
    pos_ref[...] = _cumsum_lanes(self_) * self_

    na = cap // nb
    tok = lax.broadcasted_iota(jnp.int32, (1, n_tok), 1)
    t_hi = (tok >> 6).astype(F32)
    t_lo = (tok & 63).astype(F32)
    row_a = lax.broadcasted_iota(jnp.int32, (na, 1), 0)
    row_b = lax.broadcasted_iota(jnp.int32, (nb, 1), 0)
    nt = (((1,), (1,)), ((), ()))

    def per_expert(e, carry):
        aff = aff_ref[0, pl.ds(e, 1), :]
        a_hi = aff.astype(BF16).astype(F32)
        a_mid = (aff - a_hi).astype(BF16).astype(F32)
        a_lo = aff - a_hi - a_mid
        slot = pos_ref[pl.ds(e, 1), :].astype(jnp.int32) - 1
        hi, lo = slot >> (nb.bit_length() - 1), slot & (nb - 1)
        acc = jnp.zeros((na, 5 * nb), F32)
        for c in range(n_tok // chunk):
            sl = slice(c * chunk, (c + 1) * chunk)
            one_a = jnp.where(hi[:, sl] == row_a, 1.0, 0.0).astype(BF16)
            in_b = lo[:, sl] == row_b
            pay = jnp.concatenate([jnp.where(in_b, v[:, sl], 0.0) for v in (t_hi, t_lo, a_hi, a_mid, a_lo)],
                                  axis=0).astype(BF16)
            acc = acc + lax.dot_general(one_a, pay, nt, preferred_element_type=F32)
        rows = pl.ds(pl.multiple_of(e * na, na), na) if isinstance(e, jax.Array) else slice(e * na, (e + 1) * na)
        idx_out[0, rows, :] = (acc[:, :nb] * 64.0 + acc[:, nb:2 * nb]).astype(jnp.int32)
        gate_out[0, rows, :] = acc[:, 2 * nb:3 * nb] + acc[:, 3 * nb:4 * nb] + acc[:, 4 * nb:]
        return carry

    if na % SUBLANE == 0:
        lax.fori_loop(0, n_e, per_expert, 0)
    else:
        for e in range(n_e):
            per_expert(e, 0)


def _route(aff_t, cap):
    bsz, n_e, n_tok = aff_t.shape
    nb = 16
    out = pl.BlockSpec((1, n_e * cap // nb, nb), lambda b: (b, 0, 0))
    idx, gates = pl.pallas_call(
        functools.partial(_topk_kernel, cap=cap, chunk=_tile(n_tok, 1024), nb=nb),
        grid=(bsz,), in_specs=[pl.BlockSpec((1, n_e, n_tok), lambda b: (b, 0, 0))],
        out_specs=[out, out],
        out_shape=[jax.ShapeDtypeStruct((bsz, n_e * cap // nb, nb), jnp.int32),
                   jax.ShapeDtypeStruct((bsz, n_e * cap // nb, nb), F32)],
        scratch_shapes=[pltpu.VMEM((n_e, n_tok), F32)],
        compiler_params=_cparams(("parallel",)),
        name="topk_route",
    )(aff_t)
    return idx.reshape(bsz, n_e, cap), gates.reshape(bsz, n_e, cap)


def kernel(x, c, ctx, c_ctx, mod_w, mod_b, norm1_g, w_in, q_a_g, w_q_b, kv_a_g, w_kv_b, conv_dw_w, conv_dw_b, conv_ln_g, conv_ln_b, hy_short_w, hy_short_b, hy_w1, hy_b1, hy_w2, hy_b2, hy_w3, hy_sin_freq, hy_decay, hy_skip, group_norm_g, w_out, norm2_g, router_w, w_gate, w_up, w_down, final_norm_g):
    p = dict(mod_w=mod_w, mod_b=mod_b, norm1_g=norm1_g, w_in=w_in, q_a_g=q_a_g, w_q_b=w_q_b,
             kv_a_g=kv_a_g, w_kv_b=w_kv_b, conv_dw_w=conv_dw_w, conv_dw_b=conv_dw_b,
             conv_ln_g=conv_ln_g, conv_ln_b=conv_ln_b, hy_short_w=hy_short_w,
             hy_short_b=hy_short_b, hy_w1=hy_w1, hy_b1=hy_b1, hy_w2=hy_w2, hy_b2=hy_b2,
             hy_w3=hy_w3, hy_sin_freq=hy_sin_freq, hy_decay=hy_decay, hy_skip=hy_skip,
             group_norm_g=group_norm_g, w_out=w_out, norm2_g=norm2_g, router_w=router_w,
             w_gate=w_gate, w_up=w_up, w_down=w_down)
    depth = mod_w.shape[0]
    bsz, n_lat, d = x.shape
    n_ctx = ctx.shape[1]

    rows = -(-(bsz + 1) // 8) * 8
    cc = jnp.concatenate([c, c_ctx[None, :], jnp.zeros((rows - bsz - 1, d), F32)], axis=0)
    mod_all = _modulation(cc, mod_w, mod_b)
    lat_row = lambda b: b
    ctx_row = lambda b: bsz

    tabs_l = _rope_tables(n_lat)
    tabs_c = _identity_tables(n_ctx)
    tm_l, tm_c = _tile(n_lat, 512), _tile(n_ctx, 256)
    tq_l, tq_c = _tile(n_lat, 256), _tile(n_ctx, 256)
    tl_l, tl_c = _tile(n_lat, 512), _tile(n_ctx, 256)

    xl, xc = x, ctx
    prev_l = prev_c = None
    for i in range(depth):
        last = i == depth - 1
        lw = _layer_weights(i, p)
        mod = mod_all[i].reshape(rows, 1, N_MOD * d)

        def side(xs, prev, row_of_b, tabs, tm, tq, tl, extra_keys, need_mix):
            xs, q, k, v, uc, uh = _inproj(xs, prev, mod, row_of_b, lw, tabs, tm)
            if not need_mix:
                return xs, (k, v), None
            att = _attention(q, [(k, v)] + extra_keys, tq)
            cn, hv, hx1, hx2 = _conv(uc, uh, lw, tl)
            plan = _fft_plan(xs.shape[1])
            mats = _fft_mats(plan)
            filt = _hyena_filter_spectrum(xs.shape[1], i, p, plan, mats)
            hy = _hyena(hv, hx1, hx2, p['hy_skip'][i], plan, mats, filt)
            x1, h2p, aff_t = _outproj(att, cn, hy, xs, mod, row_of_b, lw, tm)
            cap = CAPACITY_FACTOR * xs.shape[1] // N_EXPERTS
            idx, gates = _route(aff_t, cap)
            moe = _moe(h2p, idx, gates, lw)
            return x1, (k, v), moe

        xc, kv_c, moe_c = side(xc, prev_c, ctx_row, tabs_c, tm_c, tq_c, tl_c, [], not last)
        xl, _, moe_l = side(xl, prev_l, lat_row, tabs_l, tm_l, tq_l, tl_l, [kv_c], True)
        prev_l = (moe_l, mod)
        prev_c = None if moe_c is None else (moe_c, mod)

    mod = mod_all[depth - 1].reshape(rows, 1, N_MOD * d)
    return _final(xl, prev_l[0], mod, final_norm_g.reshape(1, d), tm_l)
```

```python
import functools
import math
from typing import NamedTuple

import numpy as np
import jax
import jax.numpy as jnp
from jax import lax
from jax.experimental import pallas as pl
from jax.experimental.pallas import tpu as pltpu

F32 = jnp.float32
BF16 = jnp.bfloat16
EPS = 1e-6

GRID_W = 64
N_MOD = 6
HEADS = 8
NOPE = 64
ROPE = 32
VDIM = 64
Q_LORA = 256
KV_LORA = 128
ROPE_BASE = 10000.0
CONV_CH = 256
CONV_K = 31
HY_CH = 256
HY_ORDER = 2
HY_BANDS = 16
N_EXPERTS = 16
CAPACITY_FACTOR = 2

LANE = 128
SUBLANE = 8
MXU_DIM = 256
HEAD_PAD = LANE
BF16_ROWS = 16
V_ROWS = -(-(VDIM + 1) // BF16_ROWS) * BF16_ROWS
HALO = 16
VMEM_LIMIT = 56 * 1024 * 1024
MOE_ACC_BUDGET = 16 * 1024 * 1024

C_Q = 0
C_KVN = C_Q + Q_LORA
C_KR = C_KVN + KV_LORA
C_CONV = C_KR + LANE
C_HY = C_CONV + 2 * CONV_CH
C_END = C_HY + 3 * HY_CH


def _cparams(sem):
    return pltpu.CompilerParams(dimension_semantics=sem, vmem_limit_bytes=VMEM_LIMIT)


def _rms(x, g):
    return x * lax.rsqrt(jnp.mean(x * x, axis=-1, keepdims=True) + EPS) * g


def _split(a):
    hi = a.astype(BF16)
    lo = (a - hi.astype(F32)).astype(BF16)
    return hi, lo


def _dot(a, b):
    return jnp.dot(a, b, preferred_element_type=F32)


def _from_token_major(ref, n_rows):
    fc = ref.shape[1] // n_rows
    return jnp.concatenate([ref[0, pl.ds(c, n_rows, stride=fc), :] for c in range(fc)], axis=1)


def _mod_kernel(c_ref, w_ref, b_ref, o_ref):
    c = c_ref[...]
    a = c * jax.nn.sigmoid(c)
    a_hi, a_lo = _split(a)
    w_hi, w_lo = _split(w_ref[0])
    o_ref[0] = _dot(a_hi, w_hi) + _dot(a_lo, w_hi) + _dot(a_hi, w_lo) + b_ref[0]


def _modulation(cc, mod_w, mod_b):
    depth, d, n = mod_w.shape
    rows = cc.shape[0]
    tn = 1536
    return pl.pallas_call(
        _mod_kernel,
        grid=(depth, n // tn),
        in_specs=[
            pl.BlockSpec((rows, d), lambda l, j: (0, 0)),
            pl.BlockSpec((1, d, tn), lambda l, j: (l, 0, j)),
            pl.BlockSpec((1, 1, tn), lambda l, j: (l, 0, j)),
        ],
        out_specs=pl.BlockSpec((1, rows, tn), lambda l, j: (l, 0, j)),
        out_shape=jax.ShapeDtypeStruct((depth, rows, n), F32),
        compiler_params=_cparams(("arbitrary", "arbitrary")),
        name="modulation",
    )(cc, mod_w, mod_b.reshape(depth, 1, n))


def _rope(x, c, s1, s2):
    return x * c + pltpu.roll(x, LANE - 8, 1) * s1 + pltpu.roll(x, 8, 1) * s2


def _inproj_kernel(*refs, fuse_prev, scale):
    if fuse_prev:
        xa_ref, xb_ref, g2_ref = refs[:3]
        refs = refs[3:]
    else:
        xa_ref = refs[0]
        refs = refs[1:]
    (sh_ref, sc_ref, n1g_ref, win_ref, qag_ref, wq_ref, kvg_ref, wk_ref, wv_ref,
     c_ref, s1_ref, s2_ref, vone_ref) = refs[:13]
    outs = refs[13:]
    if fuse_prev:
        x_out, q_out, k_out, v_out, uc_out, uh_out = outs
        x = xa_ref[0] + g2_ref[...] * _from_token_major(xb_ref, xa_ref.shape[1])
        x_out[0] = x
    else:
        q_out, k_out, v_out, uc_out, uh_out = outs
        x = xa_ref[0]
    hn = _rms(x, n1g_ref[...]) * (1.0 + sc_ref[...]) + sh_ref[...]
    u = _dot(hn.astype(BF16), win_ref[...])
    c, s1, s2 = c_ref[...], s1_ref[...], s2_ref[...]

    nq = _rms(u[:, C_Q:C_KVN], qag_ref[...]).astype(BF16)
    q = _dot(nq, wq_ref[...])
    for h in range(HEADS):
        sl = slice(h * HEAD_PAD, (h + 1) * HEAD_PAD)
        q_out[0, :, sl] = (_rope(q[:, sl], c, s1, s2) * scale).astype(BF16)

    kvn = _rms(u[:, C_KVN:C_KR], kvg_ref[...]).astype(BF16)
    kr = _rope(u[:, C_KR:C_CONV], c, s1, s2)
    k = _dot(kvn, wk_ref[...])
    for h in range(HEADS):
        sl = slice(h * HEAD_PAD, (h + 1) * HEAD_PAD)
        k_out[0, :, sl] = (k[:, sl] + kr).astype(BF16)
    vt = lax.dot_general(wv_ref[...], kvn, (((1,), (1,)), ((), ())), preferred_element_type=F32)
    v_out[0] = (vt + vone_ref[...]).astype(BF16)
    uc_out[0] = u[:, C_CONV:C_HY]
    uh_out[0] = u[:, C_HY:C_END]


def _inproj(x, prev, mod, row_of_b, lw, tabs, tm):
    bsz, n_tok, d = x.shape
    fuse_prev = prev is not None
    grid = (bsz, n_tok // tm)
    tok = lambda w: pl.BlockSpec((1, tm, w), lambda b, i: (b, i, 0))
    modspec = lambda k: pl.BlockSpec((None, 1, d), lambda b, i: (row_of_b(b), 0, k))
    full = lambda a: pl.BlockSpec(a.shape, lambda b, i: (0,) * a.ndim)
    tabspec = pl.BlockSpec((tm, LANE), lambda b, i: (i, 0))

    args, specs = [x], [tok(d)]
    if fuse_prev:
        args += [prev[0], prev[1]]
        specs += [pl.BlockSpec((1, tm * (d // LANE), LANE), lambda b, i: (b, i, 0)), modspec(5)]
    args += [mod, mod, lw['norm1_g'], lw['w_in'], lw['q_a_g'], lw['w_q'], lw['kv_a_g'],
             lw['w_k'], lw['w_v'], tabs[0], tabs[1], tabs[2], lw['v_one']]
    specs += [modspec(0), modspec(1), full(lw['norm1_g']), full(lw['w_in']), full(lw['q_a_g']),
              full(lw['w_q']), full(lw['kv_a_g']), full(lw['w_k']), full(lw['w_v']),
              tabspec, tabspec, tabspec, full(lw['v_one'])]
    hp = HEADS * HEAD_PAD
    out_shape, out_specs = [], []
    if fuse_prev:
        out_shape.append(jax.ShapeDtypeStruct((bsz, n_tok, d), F32))
        out_specs.append(tok(d))
    out_shape += [jax.ShapeDtypeStruct((bsz, n_tok, hp), BF16)] * 2
    out_specs += [tok(hp)] * 2
    out_shape += [jax.ShapeDtypeStruct((bsz, hp, n_tok), BF16),
                  jax.ShapeDtypeStruct((bsz, n_tok, 2 * CONV_CH), F32),
                  jax.ShapeDtypeStruct((bsz, n_tok, 3 * HY_CH), F32)]
    out_specs += [pl.BlockSpec((1, hp, tm), lambda b, i: (b, 0, i)),
                  tok(2 * CONV_CH), tok(3 * HY_CH)]
    scale = float((NOPE + ROPE) ** -0.5 * math.log2(math.e))
    res = pl.pallas_call(
        functools.partial(_inproj_kernel, fuse_prev=fuse_prev, scale=scale),
        grid=grid, in_specs=specs, out_specs=out_specs, out_shape=out_shape,
        compiler_params=_cparams(("parallel", "parallel")),
        name="inproj",
    )(*args)
    if not fuse_prev:
        res = [x] + list(res)
    return res


def _attn_kernel(*refs, n_sets, hps, tk, ahead):
    q_ref = refs[0]
    kv = refs[1:1 + 2 * n_sets]
    o_ref = refs[1 + 2 * n_sets]
    nt = (((1,), (1,)), ((), ()))
    sls = [slice(hh * HEAD_PAD, (hh + 1) * HEAD_PAD) for hh in range(hps)]
    qs = [q_ref[0, :, sl] for sl in sls]
    items = [(hh, i, c) for i in range(n_sets) for c in range(kv[2 * i].shape[1] // tk)
             for hh in range(hps)]

    def scores(item):
        hh, i, c = item
        return lax.dot_general(kv[2 * i][0, c * tk:(c + 1) * tk, sls[hh]], qs[hh], nt,
                               preferred_element_type=F32)

    m, o = [None] * hps, [None] * hps
    pending = [scores(it) for it in items[:ahead]]
    for n, (hh, i, c) in enumerate(items):
        s = pending.pop(0)
        if n + ahead < len(items):
            pending.append(scores(items[n + ahead]))
        mt = jnp.max(s, axis=0, keepdims=True)
        m_new = mt if m[hh] is None else jnp.maximum(m[hh], mt)
        pv = _dot(kv[2 * i + 1][0, hh * HEAD_PAD:hh * HEAD_PAD + V_ROWS, c * tk:(c + 1) * tk],
                  jnp.exp2(s - m_new).astype(BF16))
        o[hh] = pv if o[hh] is None else o[hh] * jnp.exp2(m[hh] - m_new) + pv
        m[hh] = m_new
    pad = jnp.zeros((HEAD_PAD - V_ROWS, o[0].shape[1]), F32)
    outs = [jnp.concatenate([oh * (1.0 / oh[VDIM:VDIM + 1, :]), pad], axis=0).T for oh in o]
    lane = lax.broadcasted_iota(jnp.int32, outs[0].shape, 1)
    for pr in range(hps // 2):
        o_ref[0, :, pr * LANE:(pr + 1) * LANE] = jnp.where(
            lane < VDIM, outs[2 * pr], pltpu.roll(outs[2 * pr + 1], VDIM, 1))


def _attention(q, key_sets, tq):
    bsz, n_q, hp = q.shape
    hps = 8
    grid = (bsz, HEADS // hps, n_q // tq)
    args = [q]
    specs = [pl.BlockSpec((1, tq, hps * HEAD_PAD), lambda b, h, i: (b, i, h))]
    for k, v in key_sets:
        n_k = k.shape[1]
        args += [k, v]
        specs += [pl.BlockSpec((1, n_k, hps * HEAD_PAD), lambda b, h, i: (b, 0, h)),
                  pl.BlockSpec((1, hps * HEAD_PAD, n_k), lambda b, h, i: (b, h, 0))]
    return pl.pallas_call(
        functools.partial(_attn_kernel, n_sets=len(key_sets), hps=hps,
                          tk=min([MXU_DIM] + [k.shape[1] for k, _ in key_sets]), ahead=8),
        grid=grid, in_specs=specs,
        out_specs=pl.BlockSpec((1, tq, hps * VDIM), lambda b, h, i: (b, i, h)),
        out_shape=jax.ShapeDtypeStruct((bsz, n_q, HEADS * VDIM), F32),
        compiler_params=_cparams(("parallel", "parallel", "arbitrary")),
        name="attention",
    )(*args)


def _conv_kernel(ucp_ref, uc_ref, ucn_ref, uhp_ref, uh_ref, uhn_ref,
                 cw_ref, cb_ref, lg_ref, lb_ref, gn_ref, hw_ref, hb_ref,
                 cn_out, v_out, x1_out, x2_out, ypad, hpad, yph, *, tl, row_tile):
    i = pl.program_id(1)
    has_prev = (i > 0).astype(F32)
    has_next = (i < pl.num_programs(1) - 1).astype(F32)

    def glu(u):
        return u[:, :CONV_CH] * jax.nn.sigmoid(u[:, CONV_CH:])

    ypad[0:HALO, :] = glu(ucp_ref[0]) * has_prev
    ypad[HALO:HALO + tl, :] = glu(uc_ref[0])
    ypad[HALO + tl:, :] = glu(ucn_ref[0]) * has_next
    hpad[0:HALO, :] = uhp_ref[0] * has_prev
    hpad[HALO:HALO + tl, :] = uh_ref[0]
    hpad[HALO + tl:, :] = uhn_ref[0] * has_next

    span = tl + 2 * HALO - SUBLANE
    for ph in range(SUBLANE):
        yph[ph] = ypad[ph:ph + span, :]

    half = CONV_K // 2
    for r in range(tl // row_tile):
        base = r * row_tile
        acc = jnp.zeros((row_tile, CONV_CH), F32) + cb_ref[...]
        for k in range(CONV_K):
            off = HALO + base + k - half
            ph = off % SUBLANE
            acc = acc + cw_ref[k:k + 1, :] * yph[ph, off - ph:off - ph + row_tile, :]
        mu = jnp.mean(acc, axis=-1, keepdims=True)
        cen = acc - mu
        var = jnp.mean(cen * cen, axis=-1, keepdims=True)
        y = cen * lax.rsqrt(var + EPS) * lg_ref[...] + lb_ref[...]
        y = y * jax.nn.sigmoid(y)
        cn_out[0, base:base + row_tile, :] = _rms(y, gn_ref[...])

        z = jnp.zeros((row_tile, 3 * HY_CH), F32) + hb_ref[...]
        for k in range(3):
            off = HALO + base + k - 1
            z = z + hw_ref[k:k + 1, :] * hpad[off:off + row_tile, :]
        v_out[0, base:base + row_tile, :] = z[:, :HY_CH]
        x1_out[0, base:base + row_tile, :] = z[:, HY_CH:2 * HY_CH]
        x2_out[0, base:base + row_tile, :] = z[:, 2 * HY_CH:]


def _conv(uc, uh, lw, tl):
    bsz, n_tok, _ = uc.shape
    nh = tl // HALO
    last = n_tok // HALO - 1
    cur = lambda w: pl.BlockSpec((1, tl, w), lambda b, i: (b, i, 0))
    prv = lambda w: pl.BlockSpec((1, HALO, w), lambda b, i: (b, jnp.maximum(i * nh - 1, 0), 0))
    nxt = lambda w: pl.BlockSpec((1, HALO, w), lambda b, i: (b, jnp.minimum((i + 1) * nh, last), 0))
    full = lambda a: pl.BlockSpec(a.shape, lambda b, i: (0,) * a.ndim)
    wts = [lw['conv_dw_w'], lw['conv_dw_b'], lw['conv_ln_g'], lw['conv_ln_b'], lw['gn_conv'],
           lw['hy_short_w'], lw['hy_short_b']]
    return pl.pallas_call(
        functools.partial(_conv_kernel, tl=tl, row_tile=min(64, tl)),
        grid=(bsz, n_tok // tl),
        in_specs=[prv(2 * CONV_CH), cur(2 * CONV_CH), nxt(2 * CONV_CH),
                  prv(3 * HY_CH), cur(3 * HY_CH), nxt(3 * HY_CH)] + [full(w) for w in wts],
        out_specs=[cur(CONV_CH)] + [cur(HY_CH)] * 3,
        out_shape=[jax.ShapeDtypeStruct((bsz, n_tok, CONV_CH), F32)]
        + [jax.ShapeDtypeStruct((bsz, n_tok, HY_CH), F32)] * 3,
        scratch_shapes=[pltpu.VMEM((tl + 2 * HALO, CONV_CH), F32),
                        pltpu.VMEM((tl + 2 * HALO, 3 * HY_CH), F32),
                        pltpu.VMEM((SUBLANE, tl + 2 * HALO - SUBLANE, CONV_CH), F32)],
        compiler_params=_cparams(("parallel", "parallel")),
        name="conv",
    )(uc, uc, uc, uh, uh, uh, *wts)


def _outproj_kernel(att_ref, cn_ref, *refs, n_hy):
    hy_refs = refs[:n_hy]
    (x_ref, g1_ref, sh_ref, sc_ref, gna_ref, gnh_ref, wo_ref, n2g_ref, rw_ref,
     x1_out, h2p_out, aff_out) = refs[n_hy:]
    a = _rms(att_ref[0], gna_ref[...]).astype(BF16)
    c = cn_ref[0].astype(BF16)
    h = _rms(_lanes(hy_refs, 0), gnh_ref[...]).astype(BF16)
    na, nc = a.shape[1], c.shape[1]
    y = (_dot(a, wo_ref[0:na, :]) + _dot(c, wo_ref[na:na + nc, :]) + _dot(h, wo_ref[na + nc:, :]))
    x1 = x_ref[0] + g1_ref[...] * y
    x1_out[0] = x1
    h2 = _rms(x1, n2g_ref[...]) * (1.0 + sc_ref[...]) + sh_ref[...]
    d = h2.shape[1]
    lo = lax.bitcast_convert_type(h2[:, :d // 2].astype(BF16).astype(F32), jnp.uint32)
    hi = lax.bitcast_convert_type(h2[:, d // 2:].astype(BF16).astype(F32), jnp.uint32)
    packed = hi | (lo >> 16)
    n_rows, pc = packed.shape[0], packed.shape[1] // LANE
    for c in range(pc):
        h2p_out[0, pl.ds(c, n_rows, stride=pc), :] = packed[:, c * LANE:(c + 1) * LANE]
    h_hi, h_lo = _split(h2)
    t = _dot(h_hi, rw_ref[...])
    logits = t[:, :LANE] + t[:, LANE:] + _dot(h_lo, rw_ref[:, :LANE])
    lane = lax.broadcasted_iota(jnp.int32, logits.shape, 1)
    logits = jnp.where(lane < N_EXPERTS, logits, -1e30)
    e = jnp.exp(logits - jnp.max(logits, axis=1, keepdims=True))
    aff = e / jnp.sum(e, axis=1, keepdims=True)
    aff_out[0] = aff.T[:N_EXPERTS, :]


def _outproj(att, cn, hy, x, mod, row_of_b, lw, tm):
    bsz, n_tok, d = x.shape
    tok = lambda w: pl.BlockSpec((1, tm, w), lambda b, i: (b, i, 0))
    modspec = lambda k: pl.BlockSpec((None, 1, d), lambda b, i: (row_of_b(b), 0, k))
    full = lambda a: pl.BlockSpec(a.shape, lambda b, i: (0,) * a.ndim)
    wts = [lw['gn_att'], lw['gn_hy'], lw['w_out'], lw['norm2_g'], lw['router_w']]
    pc = d // 2 // LANE
    return pl.pallas_call(
        functools.partial(_outproj_kernel, n_hy=len(hy)),
        grid=(bsz, n_tok // tm),
        in_specs=[tok(att.shape[2]), tok(cn.shape[2])] + [tok(LANE)] * len(hy) + [tok(d),
                  modspec(2), modspec(3), modspec(4)] + [full(w) for w in wts],
        out_specs=[tok(d), pl.BlockSpec((1, tm * pc, LANE), lambda b, i: (b, i, 0)),
                   pl.BlockSpec((1, N_EXPERTS, tm), lambda b, i: (b, 0, i))],
        out_shape=[jax.ShapeDtypeStruct((bsz, n_tok, d), F32),
                   jax.ShapeDtypeStruct((bsz, n_tok * pc, LANE), jnp.uint32),
                   jax.ShapeDtypeStruct((bsz, N_EXPERTS, n_tok), F32)],
        compiler_params=_cparams(("parallel", "parallel")),
        name="outproj",
    )(att, cn, *hy, x, mod, mod, mod, *wts)


def _moe_kernel(idx_ref, idx_nxt_ref, idx_prv_ref, gate_ref, gate_prv_ref, h2p_ref,
                wg_ref, wu_ref, wd_ref, out_hbm, acc_ref, xg0, xg1, y0, y1, sem, *, cap, pc, fc):
    b, e = pl.program_id(0), pl.program_id(1)
    n_e = pl.num_programs(1)
    group = math.gcd(cap, SUBLANE)

    def gather_rows(idx_r, xg, js):
        rows = [h2p_ref[0, pl.ds(pl.multiple_of(idx_r[0, 0, j] * pc, pc), pc), :] for j in js]
        for j, row in zip(js, rows):
            xg[pl.ds(pl.multiple_of(j * pc, pc), pc), :] = row

    def scatter_rows(idx_r, gate_r, y_r, js):
        dsts = [pl.ds(pl.multiple_of(idx_r[0, 0, j] * fc, fc), fc) for j in js]
        new = [acc_ref[dst, :] + gate_r[0, 0, j] * y_r[pl.ds(pl.multiple_of(j * fc, fc), fc), :]
               for j, dst in zip(js, dsts)]
        for dst, val in zip(dsts, new):
            acc_ref[dst, :] = val

    def in_groups(fn, unrolled):
        if unrolled:
            for jg in range(cap // group):
                fn([jg * group + u for u in range(group)])
        else:
            def body(jg, carry):
                fn([jg * group + u for u in range(group)])
                return carry
            lax.fori_loop(0, cap // group, body, 0)

    @pl.when(e == 0)
    def _():
        acc_ref[...] = jnp.zeros_like(acc_ref)
        y1[...] = jnp.zeros_like(y1)
        in_groups(functools.partial(gather_rows, idx_ref, xg0), False)

    def step(xg_cur, xg_nxt, y_cur, y_prv):
        in_groups(functools.partial(gather_rows, idx_nxt_ref, xg_nxt), True)
        lo, hi = [], []
        for c in range(pc):
            w = xg_cur[pl.ds(c, cap, stride=pc), :]
            lo.append(lax.bitcast_convert_type(w << 16, F32).astype(BF16))
            hi.append(lax.bitcast_convert_type(w & jnp.uint32(0xFFFF0000), F32).astype(BF16))
        x = jnp.concatenate(lo + hi, axis=1)
        a = _dot(x, wg_ref[0])
        u = _dot(x, wu_ref[0])
        hmid = (a * jax.nn.sigmoid(a) * u).astype(BF16)
        y = _dot(hmid, wd_ref[0])
        in_groups(functools.partial(scatter_rows, idx_prv_ref, gate_prv_ref, y_prv), True)
        for c in range(fc):
            y_cur[pl.ds(c, cap, stride=fc), :] = y[:, c * LANE:(c + 1) * LANE]

    @pl.when(e % 2 == 0)
    def _():
        step(xg0, xg1, y0, y1)

    @pl.when(e % 2 == 1)
    def _():
        step(xg1, xg0, y1, y0)

    def finish(y_last):
        in_groups(functools.partial(scatter_rows, idx_ref, gate_ref, y_last), False)
        cp = pltpu.make_async_copy(acc_ref, out_hbm.at[b], sem)
        cp.start()
        cp.wait()

    @pl.when((e == n_e - 1) & (e % 2 == 0))
    def _():
        finish(y0)

    @pl.when((e == n_e - 1) & (e % 2 == 1))
    def _():
        finish(y1)


def _moe(h2p, idx, gates, lw):
    d = lw['w_gate'].shape[1]
    pc, fc = d // 2 // LANE, d // LANE
    bsz, n_tok = h2p.shape[0], h2p.shape[1] // pc
    n_e, cap = idx.shape[1], idx.shape[2]
    ff = lw['w_gate'].shape[2]
    if bsz > 1 and bsz * n_tok * d * 4 <= MOE_ACC_BUDGET:
        offs = (jnp.arange(bsz, dtype=jnp.int32) * n_tok)[:, None, None]
        idx = jnp.swapaxes(idx + offs, 0, 1).reshape(1, n_e, bsz * cap)
        gates = jnp.swapaxes(gates, 0, 1).reshape(1, n_e, bsz * cap)
        out = _moe(h2p.reshape(1, bsz * n_tok * pc, LANE), idx, gates, lw)
        return out.reshape(bsz, n_tok * fc, LANE)
    last = n_e - 1
    smem = lambda f: pl.BlockSpec((1, 1, cap), lambda b, e: (b * n_e + f(e), 0, 0),
                                  memory_space=pltpu.SMEM)
    cur, nxt, prv = (lambda e: e), (lambda e: jnp.minimum(e + 1, last)), (lambda e: jnp.maximum(e - 1, 0))
    idx_r, gates_r = idx.reshape(bsz * n_e, 1, cap), gates.reshape(bsz * n_e, 1, cap)
    return pl.pallas_call(
        functools.partial(_moe_kernel, cap=cap, pc=pc, fc=fc),
        grid=(bsz, n_e),
        in_specs=[smem(cur), smem(nxt), smem(prv), smem(cur), smem(prv),
                  pl.BlockSpec((1, n_tok * pc, LANE), lambda b, e: (b, 0, 0)),
                  pl.BlockSpec((1, d, ff), lambda b, e: (e, 0, 0)),
                  pl.BlockSpec((1, d, ff), lambda b, e: (e, 0, 0)),
                  pl.BlockSpec((1, ff, d), lambda b, e: (e, 0, 0))],
        out_specs=pl.BlockSpec(memory_space=pl.ANY),
        out_shape=jax.ShapeDtypeStruct((bsz, n_tok * fc, LANE), F32),
        scratch_shapes=[pltpu.VMEM((n_tok * fc, LANE), F32)]
        + [pltpu.VMEM((cap * pc, LANE), jnp.uint32)] * 2
        + [pltpu.VMEM((cap * fc, LANE), F32)] * 2
        + [pltpu.SemaphoreType.DMA(())],
        compiler_params=_cparams(("arbitrary", "arbitrary")),
        name="moe",
    )(idx_r, idx_r, idx_r, gates_r, gates_r, h2p, lw['w_gate'], lw['w_up'], lw['w_down'])


def _final_kernel(x_ref, m_ref, g2_ref, fg_ref, o_ref):
    o_ref[0] = _rms(x_ref[0] + g2_ref[...] * _from_token_major(m_ref, x_ref.shape[1]), fg_ref[...])


def _final(x1, moe, mod, fg, tm):
    bsz, n_tok, d = x1.shape
    tok = pl.BlockSpec((1, tm, d), lambda b, i: (b, i, 0))
    return pl.pallas_call(
        _final_kernel,
        grid=(bsz, n_tok // tm),
        in_specs=[tok, pl.BlockSpec((1, tm * (d // LANE), LANE), lambda b, i: (b, i, 0)),
                  pl.BlockSpec((None, 1, d), lambda b, i: (b, 0, 5)),
                  pl.BlockSpec((1, d), lambda b, i: (0, 0))],
        out_specs=tok,
        out_shape=jax.ShapeDtypeStruct((bsz, n_tok, d), F32),
        compiler_params=_cparams(("parallel", "parallel")),
        name="final_norm",
    )(x1, moe, mod, fg)


def _rope_tables(n_tok):
    rows = n_tok // GRID_W
    row = jnp.repeat(jnp.arange(rows, dtype=F32), GRID_W)
    col = jnp.tile(jnp.arange(GRID_W, dtype=F32), rows)
    half = ROPE // 2
    inv = ROPE_BASE ** (-jnp.arange(0, half, 2, dtype=F32) / half)
    cr, sr = jnp.cos(row[:, None] * inv), jnp.sin(row[:, None] * inv)
    cc, sc = jnp.cos(col[:, None] * inv), jnp.sin(col[:, None] * inv)
    z8 = jnp.zeros_like(cr)
    ones = jnp.ones((n_tok, NOPE), F32)
    pad = jnp.zeros((n_tok, HEAD_PAD - NOPE - ROPE), F32)
    c = jnp.concatenate([ones, cr, cr, cc, cc, pad + 1.0], axis=1)
    s1 = jnp.concatenate([ones * 0.0, -sr, z8, -sc, z8, pad], axis=1)
    s2 = jnp.concatenate([ones * 0.0, z8, sr, z8, sc, pad], axis=1)
    return c, s1, s2


def _identity_tables(n_tok):
    return (jnp.ones((n_tok, HEAD_PAD), F32), jnp.zeros((n_tok, HEAD_PAD), F32),
            jnp.zeros((n_tok, HEAD_PAD), F32))


def _pad_heads(w, per_head, take):
    k = w.shape[0]
    w = w.reshape(k, HEADS, per_head)[:, :, take]
    w = jnp.pad(w, ((0, 0), (0, 0), (0, HEAD_PAD - w.shape[2])))
    return w.reshape(k, HEADS * HEAD_PAD).astype(BF16)


def _layer_weights(i, p):
    d = p['w_in'].shape[1]
    w_in = p['w_in'][i]
    off_kv = Q_LORA
    off_conv = off_kv + KV_LORA + ROPE
    off_hy = off_conv + 2 * CONV_CH
    kr = jnp.zeros((d, LANE), F32).at[:, NOPE:NOPE + ROPE].set(w_in[:, off_kv + KV_LORA:off_conv])
    w_in_r = jnp.concatenate([w_in[:, :off_kv], w_in[:, off_kv:off_kv + KV_LORA], kr,
                              w_in[:, off_conv:off_hy], w_in[:, off_hy:]], axis=1).astype(BF16)
    gn = p['group_norm_g'][i]
    n_att = HEADS * VDIM
    rw_hi, rw_lo = _split(jnp.pad(p['router_w'][i], ((0, 0), (0, LANE - N_EXPERTS))))
    v_one = jnp.zeros((HEADS, HEAD_PAD), F32).at[:, VDIM].set(1.0).reshape(HEADS * HEAD_PAD, 1)
    row = lambda a: a.reshape(1, -1)
    return {
        'norm1_g': row(p['norm1_g'][i]), 'w_in': w_in_r,
        'q_a_g': row(p['q_a_g'][i]), 'w_q': _pad_heads(p['w_q_b'][i], NOPE + ROPE, slice(None)),
        'kv_a_g': row(p['kv_a_g'][i]),
        'w_k': _pad_heads(p['w_kv_b'][i], NOPE + VDIM, slice(0, NOPE)),
        'w_v': _pad_heads(p['w_kv_b'][i], NOPE + VDIM, slice(NOPE, NOPE + VDIM)).T,
        'v_one': v_one,
        'conv_dw_w': p['conv_dw_w'][i], 'conv_dw_b': row(p['conv_dw_b'][i]),
        'conv_ln_g': row(p['conv_ln_g'][i]), 'conv_ln_b': row(p['conv_ln_b'][i]),
        'gn_att': row(gn[:n_att]), 'gn_conv': row(gn[n_att:n_att + CONV_CH]),
        'gn_hy': row(gn[n_att + CONV_CH:]),
        'hy_short_w': p['hy_short_w'][i], 'hy_short_b': row(p['hy_short_b'][i]),
        'w_out': p['w_out'][i].astype(BF16), 'norm2_g': row(p['norm2_g'][i]),
        'router_w': jnp.concatenate([rw_hi, rw_lo], axis=1),
        'w_gate': p['w_gate'][i].astype(BF16), 'w_up': p['w_up'][i].astype(BF16),
        'w_down': p['w_down'][i].astype(BF16),
    }


def _tile(n, pref):
    return pref if n % pref == 0 else n


def _dot3(a, b):
    a_hi, a_lo = _split(a)
    b_hi, b_lo = _split(b)
    return _dot(a_hi, b_hi) + _dot(a_lo, b_hi) + _dot(a_hi, b_lo)


class _FftPlan(NamedTuple):
    n2: int
    nh: int
    k1p: int
    f1: np.ndarray
    g1: np.ndarray
    mf: np.ndarray
    mi: np.ndarray


@functools.lru_cache(maxsize=None)
def _fft_plan(n_tok):
    n = 2 * n_tok
    n2 = 64 if n_tok >= 2048 else 16
    n1 = n // n2
    nh = n1 // 2
    k1 = nh + 1
    k1p = -(-k1 // SUBLANE) * SUBLANE
    two_pi = 2.0 * np.pi
    r = np.arange(k1)
    ang1 = two_pi * ((np.arange(nh)[None, :] * r[:, None]) % n1) / n1
    f1 = np.zeros((2 * k1p, nh))
    f1[:k1], f1[k1p:k1p + k1] = np.cos(ang1), -np.sin(ang1)
    w = np.where((r == 0) | (r == nh), 1.0, 2.0)[None, :] / n
    g1 = np.zeros((nh, 2 * k1p))
    g1[:, :k1], g1[:, k1p:k1p + k1] = np.cos(ang1.T) * w, -np.sin(ang1.T) * w
    k = r[:, None, None] + n1 * np.arange(n2)[None, :, None]
    th = two_pi * ((k * np.arange(n2)[None, None, :]) % n) / n
    tc, ts = np.cos(th), -np.sin(th)
    mf = np.zeros((k1p, 2 * n2, 2 * n2))
    mi = np.zeros((k1p, 2 * n2, 2 * n2))
    mf[:k1, :n2, :n2], mf[:k1, :n2, n2:], mf[:k1, n2:, :n2], mf[:k1, n2:, n2:] = tc, -ts, ts, tc
    tct, tst = tc.transpose(0, 2, 1), ts.transpose(0, 2, 1)
    mi[:k1, :n2, :n2], mi[:k1, :n2, n2:], mi[:k1, n2:, :n2], mi[:k1, n2:, n2:] = tct, tst, -tst, tct
    return _FftPlan(n2, nh, k1p, f1, g1, mf, mi)


def _lanes(refs, index):
    return jnp.concatenate([r[index] for r in refs], axis=1)


def _put_lanes(refs, index, val):
    for t, r in enumerate(refs):
        r[index] = val[:, t * LANE:(t + 1) * LANE]


def _stage1(x_refs, lead, f_ref, a_refs, nh, n2, rows, mm):
    def body(j, carry):
        x = _lanes(x_refs, (lead, pl.ds(j, nh, stride=n2), slice(None)))
        _put_lanes(a_refs, (pl.ds(pl.multiple_of(j * rows, SUBLANE), rows), slice(None)),
                   mm(f_ref[...], x))
        return carry
    lax.fori_loop(0, n2, body, 0, unroll=8)


def _spectrum_rows(a_refs, k, k1p, n2, rows):
    return jnp.concatenate([_lanes(a_refs, (pl.ds(k, n2, stride=rows), slice(None))),
                            _lanes(a_refs, (pl.ds(k1p + k, n2, stride=rows), slice(None)))], axis=0)


def _hyena_conv_kernel(*refs, nt, nh, n2, k1p):
    u_refs, gate_refs = refs[:nt], refs[nt:2 * nt]
    skip_ref, f1_ref, g1_ref, mf_ref, mi_ref, g_ref = refs[2 * nt:2 * nt + 6]
    y_outs, a_refs = refs[2 * nt + 6:3 * nt + 6], refs[3 * nt + 6:]
    rows = 2 * k1p
    bdot = lambda f, x: _dot(f, x.astype(BF16))
    _stage1(u_refs, 0, f1_ref, a_refs, nh, n2, rows, bdot)

    def stage2(k, carry):
        y = _dot(mf_ref[k], _spectrum_rows(a_refs, k, k1p, n2, rows).astype(BF16))
        yr, yi = y[:n2], y[n2:]
        gr, gi = g_ref[0, 0, k], g_ref[0, 1, k]
        z = jnp.concatenate([yr * gr - yi * gi, yr * gi + yi * gr], axis=0).astype(BF16)
        v = _dot(mi_ref[k], z)
        _put_lanes(a_refs, (pl.ds(k, n2, stride=rows), slice(None)), v[:n2])
        _put_lanes(a_refs, (pl.ds(k1p + k, n2, stride=rows), slice(None)), v[n2:])
        return carry
    lax.fori_loop(0, k1p, stage2, 0, unroll=8)

    def inverse1(j, carry):
        v = _lanes(a_refs, (pl.ds(pl.multiple_of(j * rows, SUBLANE), rows), slice(None))).astype(BF16)
        _put_lanes(y_outs, (0, pl.ds(j, nh, stride=n2), slice(None)), _dot(g1_ref[...], v))
        return carry
    lax.fori_loop(0, n2, inverse1, 0, unroll=8)

    n_tok = y_outs[0].shape[1]
    step = math.gcd(n_tok, 512)

    def epilogue(i, carry):
        sl = (0, pl.ds(pl.multiple_of(i * step, step), step), slice(None))
        for t in range(nt):
            y_outs[t][sl] = gate_refs[t][sl] * (
                y_outs[t][sl] + u_refs[t][sl] * skip_ref[:, t * LANE:(t + 1) * LANE])
        return carry
    lax.fori_loop(0, n_tok // step, epilogue, 0)


def _lane_tiles(x):
    if isinstance(x, (list, tuple)):
        return [(a, 0) for a in x]
    return [(x, t) for t in range(x.shape[2] // LANE)]


def _hyena_conv(u, gate, skip, g, order, plan, mats):
    ut, gt = _lane_tiles(u), _lane_tiles(gate)
    nt = len(ut)
    bsz, n_tok = ut[0][0].shape[:2]
    ch = nt * LANE
    nh, n2, k1p = plan.nh, plan.n2, plan.k1p
    rows = 2 * k1p
    tok = lambda t: pl.BlockSpec((1, n_tok, LANE), lambda b: (b, 0, t))
    const = lambda a: pl.BlockSpec(a.shape, lambda b: (0,) * a.ndim, pipeline_mode=pl.Buffered(1))
    return pl.pallas_call(
        functools.partial(_hyena_conv_kernel, nt=nt, nh=nh, n2=n2, k1p=k1p),
        grid=(bsz,),
        in_specs=[tok(t) for _, t in ut] + [tok(t) for _, t in gt]
        + [pl.BlockSpec((None, 1, ch), lambda b: (order, 0, 0)),
           const(mats['f1']), const(mats['g1']), const(mats['mf']), const(mats['mi']),
           pl.BlockSpec((1, 2, k1p, n2, ch), lambda b: (order, 0, 0, 0, 0),
                        pipeline_mode=pl.Buffered(1))],
        out_specs=[tok(0)] * nt,
        out_shape=[jax.ShapeDtypeStruct((bsz, n_tok, LANE), F32)] * nt,
        scratch_shapes=[pltpu.VMEM((n2 * rows, LANE), F32)] * nt,
        compiler_params=_cparams(("parallel",)),
        name="hyena_conv",
    )(*[a for a, _ in ut], *[a for a, _ in gt], skip.reshape(skip.shape[0], 1, ch),
      mats['f1'], mats['g1'], mats['mf'], mats['mi'], g)


def _filter_spectrum_kernel(h_ref, ss_ref, f1_ref, mf_ref, g_out, a_ref, *, nh, n2, k1p):
    rows = 2 * k1p
    s = lax.rsqrt(ss_ref[0] + ss_ref[1] + EPS)
    for direction in range(2):
        _stage1([h_ref], direction, f1_ref, [a_ref], nh, n2, rows, _dot3)

        def stage2(k, carry):
            y = _dot3(mf_ref[k], _spectrum_rows([a_ref], k, k1p, n2, rows))
            if direction == 0:
                g_out[0, 0, k] = y[:n2] * s
                g_out[0, 1, k] = y[n2:] * s
            else:
                g_out[0, 0, k] = g_out[0, 0, k] + y[:n2] * s
                g_out[0, 1, k] = g_out[0, 1, k] - y[n2:] * s
            return carry
        lax.fori_loop(0, k1p, stage2, 0)


def _filter_spectrum(h, ss, plan, mats):
    groups, n_tok, ch = h.shape
    nh, n2, k1p = plan.nh, plan.n2, plan.k1p
    full = lambda a: pl.BlockSpec(a.shape, lambda o, ci: (0,) * a.ndim)
    return pl.pallas_call(
        functools.partial(_filter_spectrum_kernel, nh=nh, n2=n2, k1p=k1p),
        grid=(groups // 2, ch // LANE),
        in_specs=[pl.BlockSpec((2, n_tok, LANE), lambda o, ci: (o, 0, ci)),
                  pl.BlockSpec((2, 1, LANE), lambda o, ci: (o, 0, ci)),
                  full(mats['f1_32']), full(mats['mf32'])],
        out_specs=pl.BlockSpec((1, 2, k1p, n2, LANE), lambda o, ci: (o, 0, 0, 0, ci)),
        out_shape=jax.ShapeDtypeStruct((groups // 2, 2, k1p, n2, ch), F32),
        scratch_shapes=[pltpu.VMEM((n2 * 2 * k1p, LANE), F32)],
        compiler_params=_cparams(("arbitrary", "arbitrary")),
        name="hyena_filter_spectrum",
    )(h, ss, mats['f1_32'], mats['mf32'])


def _filt_kernel(z_ref, w1_ref, b1_ref, w2_ref, b2_ref, fr_ref, w3_ref, dec_ref, h_out, ss_out, *, tl):
    i = pl.program_id(0)
    z = z_ref[...]
    h = jnp.sin(fr_ref[0:1, :] * (_dot3(z, w1_ref[...]) + b1_ref[...]))
    h = jnp.sin(fr_ref[1:2, :] * (_dot3(h, w2_ref[...]) + b2_ref[...]))
    h = _dot3(h, w3_ref[...])
    decay = jnp.exp(-z[:, 0:1] * jnp.abs(dec_ref[...]))
    row = i * tl + lax.broadcasted_iota(jnp.int32, (tl, 1), 0)

    @pl.when(i == 0)
    def _():
        ss_out[...] = jnp.zeros_like(ss_out)

    for g in range(2 * HY_ORDER):
        hg = h[:, g * HY_CH:(g + 1) * HY_CH] * decay
        if g % 2 == 1:
            hg = jnp.where(row > 0, hg, 0.0)
        h_out[g] = hg
        ss_out[g] += jnp.sum(hg * hg, axis=0, keepdims=True)


def _hyena_filter_spectrum(n_tok, i, p, plan, mats):
    t = jnp.linspace(0.0, 1.0, n_tok, dtype=F32)[:, None]
    w = 2.0 * math.pi * jnp.arange(n_tok, dtype=F32) / n_tok
    f = jnp.linspace(1e-4, HY_BANDS - 1, HY_BANDS, dtype=F32)
    fw = w[:, None] * f[None, :]
    feat = jnp.concatenate([t, jnp.cos(fw), -jnp.sin(fw)], axis=-1)
    n_feat = feat.shape[1]
    hid = p['hy_w1'].shape[2]
    feat = jnp.pad(feat, ((0, 0), (0, hid - n_feat)))
    w1 = jnp.pad(p['hy_w1'][i], ((0, hid - n_feat), (0, 0)))
    tl = _tile(n_tok, 512)
    groups = 2 * HY_ORDER
    full = lambda a: pl.BlockSpec(a.shape, lambda s: (0,) * a.ndim)
    wts = [w1, p['hy_b1'][i][None], p['hy_w2'][i], p['hy_b2'][i][None], p['hy_sin_freq'][i],
           p['hy_w3'][i], p['hy_decay'][i][None]]
    h, ss = pl.pallas_call(
        functools.partial(_filt_kernel, tl=tl),
        grid=(n_tok // tl,),
        in_specs=[pl.BlockSpec((tl, hid), lambda s: (s, 0))] + [full(a) for a in wts],
        out_specs=[pl.BlockSpec((groups, tl, HY_CH), lambda s: (0, s, 0)),
                   pl.BlockSpec((groups, 1, HY_CH), lambda s: (0, 0, 0))],
        out_shape=[jax.ShapeDtypeStruct((groups, n_tok, HY_CH), F32),
                   jax.ShapeDtypeStruct((groups, 1, HY_CH), F32)],
        compiler_params=_cparams(("arbitrary",)),
        name="hyena_filter_mlp",
    )(feat, *wts)
    return _filter_spectrum(h, ss, plan, mats)


def _hyena(v, x1, x2, skip, plan, mats, g):
    y = v
    for o, gate in enumerate((x1, x2)):
        y = _hyena_conv(y, gate, skip, g, o, plan, mats)
    return y


def _fft_mats(plan):
    return {'f1': jnp.asarray(plan.f1, BF16), 'f1_32': jnp.asarray(plan.f1, F32),
            'g1': jnp.asarray(plan.g1, BF16), 'mf': jnp.asarray(plan.mf, BF16),
            'mf32': jnp.asarray(plan.mf, F32), 'mi': jnp.asarray(plan.mi, BF16)}


def _cumsum_lanes(x):
    lane = lax.broadcasted_iota(jnp.int32, x.shape, 1)
    s = 1
    while s < x.shape[1]:
        x = x + jnp.where(lane >= s, pltpu.roll(x, s, 1), 0.0)
        s *= 2
    return x


def _topk_kernel(aff_ref, idx_out, gate_out, pos_ref, *, cap, chunk, nb):
    a = aff_ref[0]
    n_e, n_tok = a.shape
    bits = lax.bitcast_convert_type(a, jnp.int32)

    def count(mask):
        return jnp.sum(jnp.where(mask, 1.0, 0.0), axis=1, keepdims=True)

    def bisect(_, lohi):
        lo, hi = lohi
        mid = lo + ((hi - lo) >> 1)
        ok = count(bits >= mid) >= cap
        return jnp.where(ok, mid, lo), jnp.where(ok, hi, mid)

    lo0 = jnp.zeros((n_e, 1), jnp.int32)
    hi0 = jnp.full((n_e, 1), 0x7F800000, jnp.int32)
    thr, _ = lax.fori_loop(0, 31, bisect, (lo0, hi0))
    gt = bits > thr
    eqf = jnp.where(bits == thr, 1.0, 0.0)
    need = cap - count(gt)
    rank = _cumsum_lanes(eqf) - eqf
    self_ = jnp.where(gt, 1.0, jnp.where(rank < need, eqf, 0.0))
    pos_ref[...] = _cumsum_lanes(self_) * self_

    na = cap // nb
    tok = lax.broadcasted_iota(jnp.int32, (1, n_tok), 1)
    t_hi = (tok >> 6).astype(F32)
    t_lo = (tok & 63).astype(F32)
    row_a = lax.broadcasted_iota(jnp.int32, (na, 1), 0)
    row_b = lax.broadcasted_iota(jnp.int32, (nb, 1), 0)
    nt = (((1,), (1,)), ((), ()))

    def per_expert(e, carry):
        aff = aff_ref[0, pl.ds(e, 1), :]
        a_hi = aff.astype(BF16).astype(F32)
        a_mid = (aff - a_hi).astype(BF16).astype(F32)
        a_lo = aff - a_hi - a_mid
        slot = pos_ref[pl.ds(e, 1), :].astype(jnp.int32) - 1
        hi, lo = slot >> (nb.bit_length() - 1), slot & (nb - 1)
        acc = jnp.zeros((na, 5 * nb), F32)
        for c in range(n_tok // chunk):
            sl = slice(c * chunk, (c + 1) * chunk)
            one_a = jnp.where(hi[:, sl] == row_a, 1.0, 0.0).astype(BF16)
            in_b = lo[:, sl] == row_b
            pay = jnp.concatenate([jnp.where(in_b, v[:, sl], 0.0) for v in (t_hi, t_lo, a_hi, a_mid, a_lo)],
                                  axis=0).astype(BF16)
            acc = acc + lax.dot_general(one_a, pay, nt, preferred_element_type=F32)
        rows = pl.ds(pl.multiple_of(e * na, na), na) if isinstance(e, jax.Array) else slice(e * na, (e + 1) * na)
        idx_out[0, rows, :] = (acc[:, :nb] * 64.0 + acc[:, nb:2 * nb]).astype(jnp.int32)
        gate_out[0, rows, :] = acc[:, 2 * nb:3 * nb] + acc[:, 3 * nb:4 * nb] + acc[:, 4 * nb:]
        return carry

    if na % SUBLANE == 0:
        lax.fori_loop(0, n_e, per_expert, 0)
    else:
        for e in range(n_e):
            per_expert(e, 0)


def _route(aff_t, cap):
    bsz, n_e, n_tok = aff_t.shape
    nb = 16
    out = pl.BlockSpec((1, n_e * cap // nb, nb), lambda b: (b, 0, 0))
    idx, gates = pl.pallas_call(
        functools.partial(_topk_kernel, cap=cap, chunk=_tile(n_tok, 1024), nb=nb),
        grid=(bsz,), in_specs=[pl.BlockSpec((1, n_e, n_tok), lambda b: (b, 0, 0))],
        out_specs=[out, out],
        out_shape=[jax.ShapeDtypeStruct((bsz, n_e * cap // nb, nb), jnp.int32),
                   jax.ShapeDtypeStruct((bsz, n_e * cap // nb, nb), F32)],
        scratch_shapes=[pltpu.VMEM((n_e, n_tok), F32)],
        compiler_params=_cparams(("parallel",)),
        name="topk_route",
    )(aff_t)
    return idx.reshape(bsz, n_e, cap), gates.reshape(bsz, n_e, cap)


def kernel(x, c, ctx, c_ctx, mod_w, mod_b, norm1_g, w_in, q_a_g, w_q_b, kv_a_g, w_kv_b, conv_dw_w, conv_dw_b, conv_ln_g, conv_ln_b, hy_short_w, hy_short_b, hy_w1, hy_b1, hy_w2, hy_b2, hy_w3, hy_sin_freq, hy_decay, hy_skip, group_norm_g, w_out, norm2_g, router_w, w_gate, w_up, w_down, final_norm_g):
    p = dict(mod_w=mod_w, mod_b=mod_b, norm1_g=norm1_g, w_in=w_in, q_a_g=q_a_g, w_q_b=w_q_b,
             kv_a_g=kv_a_g, w_kv_b=w_kv_b, conv_dw_w=conv_dw_w, conv_dw_b=conv_dw_b,
             conv_ln_g=conv_ln_g, conv_ln_b=conv_ln_b, hy_short_w=hy_short_w,
             hy_short_b=hy_short_b, hy_w1=hy_w1, hy_b1=hy_b1, hy_w2=hy_w2, hy_b2=hy_b2,
             hy_w3=hy_w3, hy_sin_freq=hy_sin_freq, hy_decay=hy_decay, hy_skip=hy_skip,
             group_norm_g=group_norm_g, w_out=w_out, norm2_g=norm2_g, router_w=router_w,
             w_gate=w_gate, w_up=w_up, w_down=w_down)
    depth = mod_w.shape[0]
    bsz, n_lat, d = x.shape
    n_ctx = ctx.shape[1]

    rows = -(-(bsz + 1) // 8) * 8
    cc = jnp.concatenate([c, c_ctx[None, :], jnp.zeros((rows - bsz - 1, d), F32)], axis=0)
    mod_all = _modulation(cc, mod_w, mod_b)
    lat_row = lambda b: b
    ctx_row = lambda b: bsz

    tabs_l = _rope_tables(n_lat)
    tabs_c = _identity_tables(n_ctx)
    tm_l, tm_c = _tile(n_lat, 512), _tile(n_ctx, 256)
    tq_l, tq_c = _tile(n_lat, 256), _tile(n_ctx, 256)
    tl_l, tl_c = _tile(n_lat, 512), _tile(n_ctx, 256)

    xl, xc = x, ctx
    prev_l = prev_c = None
    for i in range(depth):
        last = i == depth - 1
        lw = _layer_weights(i, p)
        mod = mod_all[i].reshape(rows, 1, N_MOD * d)

        def side(xs, prev, row_of_b, tabs, tm, tq, tl, extra_keys, need_mix):
            xs, q, k, v, uc, uh = _inproj(xs, prev, mod, row_of_b, lw, tabs, tm)
            if not need_mix:
                return xs, (k, v), None
            att = _attention(q, [(k, v)] + extra_keys, tq)
            cn, hv, hx1, hx2 = _conv(uc, uh, lw, tl)
            plan = _fft_plan(xs.shape[1])
            mats = _fft_mats(plan)
            filt = _hyena_filter_spectrum(xs.shape[1], i, p, plan, mats)
            hy = _hyena(hv, hx1, hx2, p['hy_skip'][i], plan, mats, filt)
            x1, h2p, aff_t = _outproj(att, cn, hy, xs, mod, row_of_b, lw, tm)
            cap = CAPACITY_FACTOR * xs.shape[1] // N_EXPERTS
            idx, gates = _route(aff_t, cap)
            moe = _moe(h2p, idx, gates, lw)
            return x1, (k, v), moe

        xc, kv_c, moe_c = side(xc, prev_c, ctx_row, tabs_c, tm_c, tq_c, tl_c, [], not last)
        xl, _, moe_l = side(xl, prev_l, lat_row, tabs_l, tm_l, tq_l, tl_l, [kv_c], True)
        prev_l = (moe_l, mod)
        prev_c = None if moe_c is None else (moe_c, mod)

    mod = mod_all[depth - 1].reshape(rows, 1, N_MOD * d)
    return _final(xl, prev_l[0], mod, final_norm_g.reshape(1, d), tm_l)
```

```python
import functools
import math
from typing import NamedTuple

import numpy as np
import jax
import jax.numpy as jnp
from jax import lax
from jax.experimental import pallas as pl
from jax.experimental.pallas import tpu as pltpu

F32 = jnp.float32
BF16 = jnp.bfloat16
EPS = 1e-6

GRID_W = 64
N_MOD = 6
HEADS = 8
NOPE = 64
ROPE = 32
VDIM = 64
Q_LORA = 256
KV_LORA = 128
ROPE_BASE = 10000.0
CONV_CH = 256
CONV_K = 31
HY_CH = 256
HY_ORDER = 2
HY_BANDS = 16
N_EXPERTS = 16
CAPACITY_FACTOR = 2

LANE = 128
SUBLANE = 8
MXU_DIM = 256
HEAD_PAD = LANE
BF16_ROWS = 16
V_ROWS = -(-(VDIM + 1) // BF16_ROWS) * BF16_ROWS
HALO = 16
VMEM_LIMIT = 56 * 1024 * 1024
MOE_ACC_BUDGET = 16 * 1024 * 1024

C_Q = 0
C_KVN = C_Q + Q_LORA
C_KR = C_KVN + KV_LORA
C_CONV = C_KR + LANE
C_HY = C_CONV + 2 * CONV_CH
C_END = C_HY + 3 * HY_CH


def _cparams(sem):
    return pltpu.CompilerParams(dimension_semantics=sem, vmem_limit_bytes=VMEM_LIMIT)


def _rms(x, g):
    return x * lax.rsqrt(jnp.mean(x * x, axis=-1, keepdims=True) + EPS) * g


def _split(a):
    hi = a.astype(BF16)
    lo = (a - hi.astype(F32)).astype(BF16)
    return hi, lo


def _dot(a, b):
    return jnp.dot(a, b, preferred_element_type=F32)


def _from_token_major(ref, n_rows):
    fc = ref.shape[1] // n_rows
    return jnp.concatenate([ref[0, pl.ds(c, n_rows, stride=fc), :] for c in range(fc)], axis=1)


def _mod_kernel(c_ref, w_ref, b_ref, o_ref):
    c = c_ref[...]
    a = c * jax.nn.sigmoid(c)
    a_hi, a_lo = _split(a)
    w_hi, w_lo = _split(w_ref[0])
    o_ref[0] = _dot(a_hi, w_hi) + _dot(a_lo, w_hi) + _dot(a_hi, w_lo) + b_ref[0]


def _modulation(cc, mod_w, mod_b):
    depth, d, n = mod_w.shape
    rows = cc.shape[0]
    tn = 1536
    return pl.pallas_call(
        _mod_kernel,
        grid=(depth, n // tn),
        in_specs=[
            pl.BlockSpec((rows, d), lambda l, j: (0, 0)),
            pl.BlockSpec((1, d, tn), lambda l, j: (l, 0, j)),
            pl.BlockSpec((1, 1, tn), lambda l, j: (l, 0, j)),
        ],
        out_specs=pl.BlockSpec((1, rows, tn), lambda l, j: (l, 0, j)),
        out_shape=jax.ShapeDtypeStruct((depth, rows, n), F32),
        compiler_params=_cparams(("arbitrary", "arbitrary")),
        name="modulation",
    )(cc, mod_w, mod_b.reshape(depth, 1, n))


def _rope(x, c, s1, s2):
    return x * c + pltpu.roll(x, LANE - 8, 1) * s1 + pltpu.roll(x, 8, 1) * s2


def _inproj_kernel(*refs, fuse_prev, scale):
    if fuse_prev:
        xa_ref, xb_ref, g2_ref = refs[:3]
        refs = refs[3:]
    else:
        xa_ref = refs[0]
        refs = refs[1:]
    (sh_ref, sc_ref, n1g_ref, win_ref, qag_ref, wq_ref, kvg_ref, wk_ref, wv_ref,
     c_ref, s1_ref, s2_ref, vone_ref) = refs[:13]
    outs = refs[13:]
    if fuse_prev:
        x_out, q_out, k_out, v_out, uc_out, uh_out = outs
        x = xa_ref[0] + g2_ref[...] * _from_token_major(xb_ref, xa_ref.shape[1])
        x_out[0] = x
    else:
        q_out, k_out, v_out, uc_out, uh_out = outs
        x = xa_ref[0]
    hn = _rms(x, n1g_ref[...]) * (1.0 + sc_ref[...]) + sh_ref[...]
    u = _dot(hn.astype(BF16), win_ref[...])
    c, s1, s2 = c_ref[...], s1_ref[...], s2_ref[...]

    nq = _rms(u[:, C_Q:C_KVN], qag_ref[...]).astype(BF16)
    q = _dot(nq, wq_ref[...])
    for h in range(HEADS):
        sl = slice(h * HEAD_PAD, (h + 1) * HEAD_PAD)
        q_out[0, :, sl] = (_rope(q[:, sl], c, s1, s2) * scale).astype(BF16)

    kvn = _rms(u[:, C_KVN:C_KR], kvg_ref[...]).astype(BF16)
    kr = _rope(u[:, C_KR:C_CONV], c, s1, s2)
    k = _dot(kvn, wk_ref[...])
    for h in range(HEADS):
        sl = slice(h * HEAD_PAD, (h + 1) * HEAD_PAD)
        k_out[0, :, sl] = (k[:, sl] + kr).astype(BF16)
    vt = lax.dot_general(wv_ref[...], kvn, (((1,), (1,)), ((), ())), preferred_element_type=F32)
    v_out[0] = (vt + vone_ref[...]).astype(BF16)
    uc_out[0] = u[:, C_CONV:C_HY].astype(BF16)
    uh_out[0] = u[:, C_HY:C_END].astype(BF16)


def _inproj(x, prev, mod, row_of_b, lw, tabs, tm):
    bsz, n_tok, d = x.shape
    fuse_prev = prev is not None
    grid = (bsz, n_tok // tm)
    tok = lambda w: pl.BlockSpec((1, tm, w), lambda b, i: (b, i, 0))
    modspec = lambda k: pl.BlockSpec((None, 1, d), lambda b, i: (row_of_b(b), 0, k))
    full = lambda a: pl.BlockSpec(a.shape, lambda b, i: (0,) * a.ndim)
    tabspec = pl.BlockSpec((tm, LANE), lambda b, i: (i, 0))

    args, specs = [x], [tok(d)]
    if fuse_prev:
        args += [prev[0], prev[1]]
        specs += [pl.BlockSpec((1, tm * (d // LANE), LANE), lambda b, i: (b, i, 0)), modspec(5)]
    args += [mod, mod, lw['norm1_g'], lw['w_in'], lw['q_a_g'], lw['w_q'], lw['kv_a_g'],
             lw['w_k'], lw['w_v'], tabs[0], tabs[1], tabs[2], lw['v_one']]
    specs += [modspec(0), modspec(1), full(lw['norm1_g']), full(lw['w_in']), full(lw['q_a_g']),
              full(lw['w_q']), full(lw['kv_a_g']), full(lw['w_k']), full(lw['w_v']),
              tabspec, tabspec, tabspec, full(lw['v_one'])]
    hp = HEADS * HEAD_PAD
    out_shape, out_specs = [], []
    if fuse_prev:
        out_shape.append(jax.ShapeDtypeStruct((bsz, n_tok, d), F32))
        out_specs.append(tok(d))
    out_shape += [jax.ShapeDtypeStruct((bsz, n_tok, hp), BF16)] * 2
    out_specs += [tok(hp)] * 2
    out_shape += [jax.ShapeDtypeStruct((bsz, hp, n_tok), BF16),
                  jax.ShapeDtypeStruct((bsz, n_tok, 2 * CONV_CH), BF16),
                  jax.ShapeDtypeStruct((bsz, n_tok, 3 * HY_CH), BF16)]
    out_specs += [pl.BlockSpec((1, hp, tm), lambda b, i: (b, 0, i)),
                  tok(2 * CONV_CH), tok(3 * HY_CH)]
    scale = float((NOPE + ROPE) ** -0.5 * math.log2(math.e))
    res = pl.pallas_call(
        functools.partial(_inproj_kernel, fuse_prev=fuse_prev, scale=scale),
        grid=grid, in_specs=specs, out_specs=out_specs, out_shape=out_shape,
        compiler_params=_cparams(("parallel", "parallel")),
        name="inproj",
    )(*args)
    if not fuse_prev:
        res = [x] + list(res)
    return res


def _attn_kernel(*refs, n_sets, hps, tk, ahead):
    q_ref = refs[0]
    kv = refs[1:1 + 2 * n_sets]
    o_ref = refs[1 + 2 * n_sets]
    nt = (((1,), (1,)), ((), ()))
    sls = [slice(hh * HEAD_PAD, (hh + 1) * HEAD_PAD) for hh in range(hps)]
    qs = [q_ref[0, :, sl] for sl in sls]
    tks = [min(tk, kv[2 * i].shape[1]) for i in range(n_sets)]
    items = [(hh, i, c) for i in range(n_sets) for c in range(kv[2 * i].shape[1] // tks[i])
             for hh in range(hps)]

    def scores(item):
        hh, i, c = item
        return lax.dot_general(kv[2 * i][0, c * tks[i]:(c + 1) * tks[i], sls[hh]], qs[hh], nt,
                               preferred_element_type=F32)

    m, o = [None] * hps, [None] * hps
    pending = [scores(it) for it in items[:ahead]]
    for n, (hh, i, c) in enumerate(items):
        s = pending.pop(0)
        if n + ahead < len(items):
            pending.append(scores(items[n + ahead]))
        mt = jnp.max(s, axis=0, keepdims=True)
        m_new = mt if m[hh] is None else jnp.maximum(m[hh], mt)
        pv = _dot(kv[2 * i + 1][0, hh * HEAD_PAD:hh * HEAD_PAD + V_ROWS, c * tks[i]:(c + 1) * tks[i]],
                  jnp.exp2(s - m_new).astype(BF16))
        o[hh] = pv if o[hh] is None else o[hh] * jnp.exp2(m[hh] - m_new) + pv
        m[hh] = m_new
    pad = jnp.zeros((HEAD_PAD - V_ROWS, o[0].shape[1]), F32)
    outs = [jnp.concatenate([oh * (1.0 / oh[VDIM:VDIM + 1, :]), pad], axis=0).T for oh in o]
    lane = lax.broadcasted_iota(jnp.int32, outs[0].shape, 1)
    for pr in range(hps // 2):
        o_ref[0, :, pr * LANE:(pr + 1) * LANE] = jnp.where(
            lane < VDIM, outs[2 * pr], pltpu.roll(outs[2 * pr + 1], VDIM, 1))


def _attention(q, key_sets, tq):
    bsz, n_q, hp = q.shape
    hps = 8
    grid = (bsz, HEADS // hps, n_q // tq)
    args = [q]
    specs = [pl.BlockSpec((1, tq, hps * HEAD_PAD), lambda b, h, i: (b, i, h))]
    for k, v in key_sets:
        n_k = k.shape[1]
        args += [k, v]
        specs += [pl.BlockSpec((1, n_k, hps * HEAD_PAD), lambda b, h, i: (b, 0, h)),
                  pl.BlockSpec((1, hps * HEAD_PAD, n_k), lambda b, h, i: (b, h, 0))]
    return pl.pallas_call(
        functools.partial(_attn_kernel, n_sets=len(key_sets), hps=hps,
                          tk=MXU_DIM, ahead=8),
        grid=grid, in_specs=specs,
        out_specs=pl.BlockSpec((1, tq, hps * VDIM), lambda b, h, i: (b, i, h)),
        out_shape=jax.ShapeDtypeStruct((bsz, n_q, HEADS * VDIM), F32),
        compiler_params=_cparams(("parallel", "parallel", "arbitrary")),
        name="attention",
    )(*args)


def _conv_kernel(ucp_ref, uc_ref, ucn_ref, uhp_ref, uh_ref, uhn_ref,
                 cw_ref, cb_ref, lg_ref, lb_ref, gn_ref, hw_ref, hb_ref,
                 cn_out, v_out, x1_out, x2_out, ypad, hpad, yph, *, tl, row_tile):
    i = pl.program_id(1)
    has_prev = (i > 0).astype(F32)
    has_next = (i < pl.num_programs(1) - 1).astype(F32)

    def glu(u):
        return u[:, :CONV_CH] * jax.nn.sigmoid(u[:, CONV_CH:])

    f32 = lambda ref: ref[0].astype(F32)
    ypad[0:HALO, :] = glu(f32(ucp_ref)) * has_prev
    ypad[HALO:HALO + tl, :] = glu(f32(uc_ref))
    ypad[HALO + tl:, :] = glu(f32(ucn_ref)) * has_next
    hpad[0:HALO, :] = f32(uhp_ref) * has_prev
    hpad[HALO:HALO + tl, :] = f32(uh_ref)
    hpad[HALO + tl:, :] = f32(uhn_ref) * has_next

    span = tl + 2 * HALO - SUBLANE
    for ph in range(1, SUBLANE):
        yph[ph] = ypad[ph:ph + span, :]

    half = CONV_K // 2
    for r in range(tl // row_tile):
        base = r * row_tile
        acc = jnp.zeros((row_tile, CONV_CH), F32) + cb_ref[...]
        for k in range(CONV_K):
            off = HALO + base + k - half
            ph = off % SUBLANE
            src = ypad[off:off + row_tile, :] if ph == 0 else yph[ph, off - ph:off - ph + row_tile, :]
            acc = acc + cw_ref[k:k + 1, :] * src
        mu = jnp.mean(acc, axis=-1, keepdims=True)
        cen = acc - mu
        var = jnp.mean(cen * cen, axis=-1, keepdims=True)
        y = cen * lax.rsqrt(var + EPS) * lg_ref[...] + lb_ref[...]
        y = y * jax.nn.sigmoid(y)
        cn_out[0, base:base + row_tile, :] = _rms(y, gn_ref[...])

        z = jnp.zeros((row_tile, 3 * HY_CH), F32) + hb_ref[...]
        for k in range(3):
            off = HALO + base + k - 1
            z = z + hw_ref[k:k + 1, :] * hpad[off:off + row_tile, :]
        v_out[0, base:base + row_tile, :] = z[:, :HY_CH]
        x1_out[0, base:base + row_tile, :] = z[:, HY_CH:2 * HY_CH]
        x2_out[0, base:base + row_tile, :] = z[:, 2 * HY_CH:]


def _conv(uc, uh, lw, tl):
    bsz, n_tok, _ = uc.shape
    nh = tl // HALO
    last = n_tok // HALO - 1
    cur = lambda w: pl.BlockSpec((1, tl, w), lambda b, i: (b, i, 0))
    prv = lambda w: pl.BlockSpec((1, HALO, w), lambda b, i: (b, jnp.maximum(i * nh - 1, 0), 0))
    nxt = lambda w: pl.BlockSpec((1, HALO, w), lambda b, i: (b, jnp.minimum((i + 1) * nh, last), 0))
    full = lambda a: pl.BlockSpec(a.shape, lambda b, i: (0,) * a.ndim)
    wts = [lw['conv_dw_w'], lw['conv_dw_b'], lw['conv_ln_g'], lw['conv_ln_b'], lw['gn_conv'],
           lw['hy_short_w'], lw['hy_short_b']]
    return pl.pallas_call(
        functools.partial(_conv_kernel, tl=tl, row_tile=min(64, tl)),
        grid=(bsz, n_tok // tl),
        in_specs=[prv(2 * CONV_CH), cur(2 * CONV_CH), nxt(2 * CONV_CH),
                  prv(3 * HY_CH), cur(3 * HY_CH), nxt(3 * HY_CH)] + [full(w) for w in wts],
        out_specs=[cur(CONV_CH)] + [cur(HY_CH)] * 3,
        out_shape=[jax.ShapeDtypeStruct((bsz, n_tok, CONV_CH), F32)]
        + [jax.ShapeDtypeStruct((bsz, n_tok, HY_CH), F32)] * 3,
        scratch_shapes=[pltpu.VMEM((tl + 2 * HALO, CONV_CH), F32),
                        pltpu.VMEM((tl + 2 * HALO, 3 * HY_CH), F32),
                        pltpu.VMEM((SUBLANE, tl + 2 * HALO - SUBLANE, CONV_CH), F32)],
        compiler_params=_cparams(("parallel", "parallel")),
        name="conv",
    )(uc, uc, uc, uh, uh, uh, *wts)


def _outproj_kernel(att_ref, cn_ref, *refs, n_hy):
    hy_refs = refs[:n_hy]
    (x_ref, g1_ref, sh_ref, sc_ref, gna_ref, gnh_ref, wo_ref, n2g_ref, rw_ref,
     x1_out, h2p_out, aff_out) = refs[n_hy:]
    a = _rms(att_ref[0], gna_ref[...]).astype(BF16)
    c = cn_ref[0].astype(BF16)
    h = _rms(_lanes(hy_refs, 0), gnh_ref[...]).astype(BF16)
    na, nc = a.shape[1], c.shape[1]
    y = (_dot(a, wo_ref[0:na, :]) + _dot(c, wo_ref[na:na + nc, :]) + _dot(h, wo_ref[na + nc:, :]))
    x1 = x_ref[0] + g1_ref[...] * y
    x1_out[0] = x1
    h2 = _rms(x1, n2g_ref[...]) * (1.0 + sc_ref[...]) + sh_ref[...]
    d = h2.shape[1]
    lo = lax.bitcast_convert_type(h2[:, :d // 2].astype(BF16).astype(F32), jnp.uint32)
    hi = lax.bitcast_convert_type(h2[:, d // 2:].astype(BF16).astype(F32), jnp.uint32)
    packed = hi | (lo >> 16)
    n_rows, pc = packed.shape[0], packed.shape[1] // LANE
    for c in range(pc):
        h2p_out[0, pl.ds(c, n_rows, stride=pc), :] = packed[:, c * LANE:(c + 1) * LANE]
    h_hi, h_lo = _split(h2)
    t = _dot(h_hi, rw_ref[...])
    logits = t[:, :LANE] + t[:, LANE:] + _dot(h_lo, rw_ref[:, :LANE])
    lane = lax.broadcasted_iota(jnp.int32, logits.shape, 1)
    logits = jnp.where(lane < N_EXPERTS, logits, -1e30)
    e = jnp.exp(logits - jnp.max(logits, axis=1, keepdims=True))
    aff = e / jnp.sum(e, axis=1, keepdims=True)
    aff_out[0] = aff.T[:N_EXPERTS, :]


def _outproj(att, cn, hy, x, mod, row_of_b, lw, tm):
    bsz, n_tok, d = x.shape
    tok = lambda w: pl.BlockSpec((1, tm, w), lambda b, i: (b, i, 0))
    modspec = lambda k: pl.BlockSpec((None, 1, d), lambda b, i: (row_of_b(b), 0, k))
    full = lambda a: pl.BlockSpec(a.shape, lambda b, i: (0,) * a.ndim)
    wts = [lw['gn_att'], lw['gn_hy'], lw['w_out'], lw['norm2_g'], lw['router_w']]
    pc = d // 2 // LANE
    return pl.pallas_call(
        functools.partial(_outproj_kernel, n_hy=len(hy)),
        grid=(bsz, n_tok // tm),
        in_specs=[tok(att.shape[2]), tok(cn.shape[2])] + [tok(LANE)] * len(hy) + [tok(d),
                  modspec(2), modspec(3), modspec(4)] + [full(w) for w in wts],
        out_specs=[tok(d), pl.BlockSpec((1, tm * pc, LANE), lambda b, i: (b, i, 0)),
                   pl.BlockSpec((1, N_EXPERTS, tm), lambda b, i: (b, 0, i))],
        out_shape=[jax.ShapeDtypeStruct((bsz, n_tok, d), F32),
                   jax.ShapeDtypeStruct((bsz, n_tok * pc, LANE), jnp.uint32),
                   jax.ShapeDtypeStruct((bsz, N_EXPERTS, n_tok), F32)],
        compiler_params=_cparams(("parallel", "parallel")),
        name="outproj",
    )(att, cn, *hy, x, mod, mod, mod, *wts)


def _moe_kernel(idx_ref, idx_nxt_ref, idx_prv_ref, gate_ref, gate_prv_ref, h2p_ref,
                wg_ref, wu_ref, wd_ref, out_hbm, acc_ref, xg0, xg1, y0, y1, sem, *, cap, pc, fc):
    b, e = pl.program_id(0), pl.program_id(1)
    n_e = pl.num_programs(1)
    group = math.gcd(cap, SUBLANE)

    def gather_rows(idx_r, xg, js):
        rows = [h2p_ref[0, pl.ds(pl.multiple_of(idx_r[0, 0, j] * pc, pc), pc), :] for j in js]
        for j, row in zip(js, rows):
            xg[pl.ds(pl.multiple_of(j * pc, pc), pc), :] = row

    def scatter_rows(idx_r, gate_r, y_r, js):
        dsts = [pl.ds(pl.multiple_of(idx_r[0, 0, j] * fc, fc), fc) for j in js]
        new = [acc_ref[dst, :] + gate_r[0, 0, j] * y_r[pl.ds(pl.multiple_of(j * fc, fc), fc), :]
               for j, dst in zip(js, dsts)]
        for dst, val in zip(dsts, new):
            acc_ref[dst, :] = val

    def in_groups(fn, unrolled):
        if unrolled:
            for jg in range(cap // group):
                fn([jg * group + u for u in range(group)])
        else:
            def body(jg, carry):
                fn([jg * group + u for u in range(group)])
                return carry
            lax.fori_loop(0, cap // group, body, 0)

    @pl.when(e == 0)
    def _():
        acc_ref[...] = jnp.zeros_like(acc_ref)
        y1[...] = jnp.zeros_like(y1)
        in_groups(functools.partial(gather_rows, idx_ref, xg0), False)

    def step(xg_cur, xg_nxt, y_cur, y_prv):
        in_groups(functools.partial(gather_rows, idx_nxt_ref, xg_nxt), True)
        lo, hi = [], []
        for c in range(pc):
            w = xg_cur[pl.ds(c, cap, stride=pc), :]
            lo.append(lax.bitcast_convert_type(w << 16, F32).astype(BF16))
            hi.append(lax.bitcast_convert_type(w & jnp.uint32(0xFFFF0000), F32).astype(BF16))
        x = jnp.concatenate(lo + hi, axis=1)
        a = _dot(x, wg_ref[0])
        u = _dot(x, wu_ref[0])
        hmid = (a * jax.nn.sigmoid(a) * u).astype(BF16)
        y = _dot(hmid, wd_ref[0])
        in_groups(functools.partial(scatter_rows, idx_prv_ref, gate_prv_ref, y_prv), True)
        for c in range(fc):
            y_cur[pl.ds(c, cap, stride=fc), :] = y[:, c * LANE:(c + 1) * LANE]

    @pl.when(e % 2 == 0)
    def _():
        step(xg0, xg1, y0, y1)

    @pl.when(e % 2 == 1)
    def _():
        step(xg1, xg0, y1, y0)

    def finish(y_last):
        in_groups(functools.partial(scatter_rows, idx_ref, gate_ref, y_last), False)
        cp = pltpu.make_async_copy(acc_ref, out_hbm.at[b], sem)
        cp.start()
        cp.wait()

    @pl.when((e == n_e - 1) & (e % 2 == 0))
    def _():
        finish(y0)

    @pl.when((e == n_e - 1) & (e % 2 == 1))
    def _():
        finish(y1)


def _moe(h2p, idx, gates, lw):
    d = lw['w_gate'].shape[1]
    pc, fc = d // 2 // LANE, d // LANE
    bsz, n_tok = h2p.shape[0], h2p.shape[1] // pc
    n_e, cap = idx.shape[1], idx.shape[2]
    ff = lw['w_gate'].shape[2]
    if bsz > 1 and bsz * n_tok * d * 4 <= MOE_ACC_BUDGET:
        offs = (jnp.arange(bsz, dtype=jnp.int32) * n_tok)[:, None, None]
        idx = jnp.swapaxes(idx + offs, 0, 1).reshape(1, n_e, bsz * cap)
        gates = jnp.swapaxes(gates, 0, 1).reshape(1, n_e, bsz * cap)
        out = _moe(h2p.reshape(1, bsz * n_tok * pc, LANE), idx, gates, lw)
        return out.reshape(bsz, n_tok * fc, LANE)
    last = n_e - 1
    smem = lambda f: pl.BlockSpec((1, 1, cap), lambda b, e: (b * n_e + f(e), 0, 0),
                                  memory_space=pltpu.SMEM)
    cur, nxt, prv = (lambda e: e), (lambda e: jnp.minimum(e + 1, last)), (lambda e: jnp.maximum(e - 1, 0))
    idx_r, gates_r = idx.reshape(bsz * n_e, 1, cap), gates.reshape(bsz * n_e, 1, cap)
    return pl.pallas_call(
        functools.partial(_moe_kernel, cap=cap, pc=pc, fc=fc),
        grid=(bsz, n_e),
        in_specs=[smem(cur), smem(nxt), smem(prv), smem(cur), smem(prv),
                  pl.BlockSpec((1, n_tok * pc, LANE), lambda b, e: (b, 0, 0)),
                  pl.BlockSpec((1, d, ff), lambda b, e: (e, 0, 0)),
                  pl.BlockSpec((1, d, ff), lambda b, e: (e, 0, 0)),
                  pl.BlockSpec((1, ff, d), lambda b, e: (e, 0, 0))],
        out_specs=pl.BlockSpec(memory_space=pl.ANY),
        out_shape=jax.ShapeDtypeStruct((bsz, n_tok * fc, LANE), F32),
        scratch_shapes=[pltpu.VMEM((n_tok * fc, LANE), F32)]
        + [pltpu.VMEM((cap * pc, LANE), jnp.uint32)] * 2
        + [pltpu.VMEM((cap * fc, LANE), F32)] * 2
        + [pltpu.SemaphoreType.DMA(())],
        compiler_params=_cparams(("arbitrary", "arbitrary")),
        name="moe",
    )(idx_r, idx_r, idx_r, gates_r, gates_r, h2p, lw['w_gate'], lw['w_up'], lw['w_down'])


def _final_kernel(x_ref, m_ref, g2_ref, fg_ref, o_ref):
    o_ref[0] = _rms(x_ref[0] + g2_ref[...] * _from_token_major(m_ref, x_ref.shape[1]), fg_ref[...])


def _final(x1, moe, mod, fg, tm):
    bsz, n_tok, d = x1.shape
    tok = pl.BlockSpec((1, tm, d), lambda b, i: (b, i, 0))
    return pl.pallas_call(
        _final_kernel,
        grid=(bsz, n_tok // tm),
        in_specs=[tok, pl.BlockSpec((1, tm * (d // LANE), LANE), lambda b, i: (b, i, 0)),
                  pl.BlockSpec((None, 1, d), lambda b, i: (b, 0, 5)),
                  pl.BlockSpec((1, d), lambda b, i: (0, 0))],
        out_specs=tok,
        out_shape=jax.ShapeDtypeStruct((bsz, n_tok, d), F32),
        compiler_params=_cparams(("parallel", "parallel")),
        name="final_norm",
    )(x1, moe, mod, fg)


def _rope_tables(n_tok):
    rows = n_tok // GRID_W
    row = jnp.repeat(jnp.arange(rows, dtype=F32), GRID_W)
    col = jnp.tile(jnp.arange(GRID_W, dtype=F32), rows)
    half = ROPE // 2
    inv = ROPE_BASE ** (-jnp.arange(0, half, 2, dtype=F32) / half)
    cr, sr = jnp.cos(row[:, None] * inv), jnp.sin(row[:, None] * inv)
    cc, sc = jnp.cos(col[:, None] * inv), jnp.sin(col[:, None] * inv)
    z8 = jnp.zeros_like(cr)
    ones = jnp.ones((n_tok, NOPE), F32)
    pad = jnp.zeros((n_tok, HEAD_PAD - NOPE - ROPE), F32)
    c = jnp.concatenate([ones, cr, cr, cc, cc, pad + 1.0], axis=1)
    s1 = jnp.concatenate([ones * 0.0, -sr, z8, -sc, z8, pad], axis=1)
    s2 = jnp.concatenate([ones * 0.0, z8, sr, z8, sc, pad], axis=1)
    return c, s1, s2


def _identity_tables(n_tok):
    return (jnp.ones((n_tok, HEAD_PAD), F32), jnp.zeros((n_tok, HEAD_PAD), F32),
            jnp.zeros((n_tok, HEAD_PAD), F32))


def _pad_heads(w, per_head, take):
    k = w.shape[0]
    w = w.reshape(k, HEADS, per_head)[:, :, take]
    w = jnp.pad(w, ((0, 0), (0, 0), (0, HEAD_PAD - w.shape[2])))
    return w.reshape(k, HEADS * HEAD_PAD).astype(BF16)


def _layer_weights(i, p):
    d = p['w_in'].shape[1]
    w_in = p['w_in'][i]
    off_kv = Q_LORA
    off_conv = off_kv + KV_LORA + ROPE
    off_hy = off_conv + 2 * CONV_CH
    kr = jnp.zeros((d, LANE), F32).at[:, NOPE:NOPE + ROPE].set(w_in[:, off_kv + KV_LORA:off_conv])
    w_in_r = jnp.concatenate([w_in[:, :off_kv], w_in[:, off_kv:off_kv + KV_LORA], kr,
                              w_in[:, off_conv:off_hy], w_in[:, off_hy:]], axis=1).astype(BF16)
    gn = p['group_norm_g'][i]
    n_att = HEADS * VDIM
    rw_hi, rw_lo = _split(jnp.pad(p['router_w'][i], ((0, 0), (0, LANE - N_EXPERTS))))
    v_one = jnp.zeros((HEADS, HEAD_PAD), F32).at[:, VDIM].set(1.0).reshape(HEADS * HEAD_PAD, 1)
    row = lambda a: a.reshape(1, -1)
    return {
        'norm1_g': row(p['norm1_g'][i]), 'w_in': w_in_r,
        'q_a_g': row(p['q_a_g'][i]), 'w_q': _pad_heads(p['w_q_b'][i], NOPE + ROPE, slice(None)),
        'kv_a_g': row(p['kv_a_g'][i]),
        'w_k': _pad_heads(p['w_kv_b'][i], NOPE + VDIM, slice(0, NOPE)),
        'w_v': _pad_heads(p['w_kv_b'][i], NOPE + VDIM, slice(NOPE, NOPE + VDIM)).T,
        'v_one': v_one,
        'conv_dw_w': p['conv_dw_w'][i], 'conv_dw_b': row(p['conv_dw_b'][i]),
        'conv_ln_g': row(p['conv_ln_g'][i]), 'conv_ln_b': row(p['conv_ln_b'][i]),
        'gn_att': row(gn[:n_att]), 'gn_conv': row(gn[n_att:n_att + CONV_CH]),
        'gn_hy': row(gn[n_att + CONV_CH:]),
        'hy_short_w': p['hy_short_w'][i], 'hy_short_b': row(p['hy_short_b'][i]),
        'w_out': p['w_out'][i].astype(BF16), 'norm2_g': row(p['norm2_g'][i]),
        'router_w': jnp.concatenate([rw_hi, rw_lo], axis=1),
        'w_gate': p['w_gate'][i].astype(BF16), 'w_up': p['w_up'][i].astype(BF16),
        'w_down': p['w_down'][i].astype(BF16),
    }


def _tile(n, pref):
    return pref if n % pref == 0 else n


def _dot3(a, b):
    a_hi, a_lo = _split(a)
    b_hi, b_lo = _split(b)
    return _dot(a_hi, b_hi) + _dot(a_lo, b_hi) + _dot(a_hi, b_lo)


class _FftPlan(NamedTuple):
    n2: int
    nh: int
    k1p: int
    f1: np.ndarray
    g1: np.ndarray
    mf: np.ndarray
    mi: np.ndarray


@functools.lru_cache(maxsize=None)
def _fft_plan(n_tok):
    n = 2 * n_tok
    n2 = 64 if n_tok >= 2048 else 16
    n1 = n // n2
    nh = n1 // 2
    k1 = nh + 1
    k1p = -(-k1 // SUBLANE) * SUBLANE
    two_pi = 2.0 * np.pi
    r = np.arange(k1)
    ang1 = two_pi * ((np.arange(nh)[None, :] * r[:, None]) % n1) / n1
    f1 = np.zeros((2 * k1p, nh))
    f1[:k1], f1[k1p:k1p + k1] = np.cos(ang1), -np.sin(ang1)
    w = np.where((r == 0) | (r == nh), 1.0, 2.0)[None, :] / n
    g1 = np.zeros((nh, 2 * k1p))
    g1[:, :k1], g1[:, k1p:k1p + k1] = np.cos(ang1.T) * w, -np.sin(ang1.T) * w
    k = r[:, None, None] + n1 * np.arange(n2)[None, :, None]
    th = two_pi * ((k * np.arange(n2)[None, None, :]) % n) / n
    tc, ts = np.cos(th), -np.sin(th)
    mf = np.zeros((k1p, 2 * n2, 2 * n2))
    mi = np.zeros((k1p, 2 * n2, 2 * n2))
    mf[:k1, :n2, :n2], mf[:k1, :n2, n2:], mf[:k1, n2:, :n2], mf[:k1, n2:, n2:] = tc, -ts, ts, tc
    tct, tst = tc.transpose(0, 2, 1), ts.transpose(0, 2, 1)
    mi[:k1, :n2, :n2], mi[:k1, :n2, n2:], mi[:k1, n2:, :n2], mi[:k1, n2:, n2:] = tct, tst, -tst, tct
    return _FftPlan(n2, nh, k1p, f1, g1, mf, mi)


def _lanes(refs, index):
    return jnp.concatenate([r[index] for r in refs], axis=1)


def _put_lanes(refs, index, val):
    for t, r in enumerate(refs):
        r[index] = val[:, t * LANE:(t + 1) * LANE]


def _stage1(x_refs, lead, f_ref, a_refs, nh, n2, rows, mm):
    def body(j, carry):
        x = _lanes(x_refs, (lead, pl.ds(j, nh, stride=n2), slice(None)))
        _put_lanes(a_refs, (pl.ds(pl.multiple_of(j * rows, SUBLANE), rows), slice(None)),
                   mm(f_ref[...], x))
        return carry
    lax.fori_loop(0, n2, body, 0, unroll=8)


def _spectrum_rows(a_refs, k, k1p, n2, rows):
    return jnp.concatenate([_lanes(a_refs, (pl.ds(k, n2, stride=rows), slice(None))),
                            _lanes(a_refs, (pl.ds(k1p + k, n2, stride=rows), slice(None)))], axis=0)


def _hyena_conv_kernel(*refs, nt, nh, n2, k1p):
    u_refs, gate_refs = refs[:nt], refs[nt:2 * nt]
    skip_ref, f1_ref, g1_ref, mf_ref, mi_ref, g_ref = refs[2 * nt:2 * nt + 6]
    y_outs, a_refs = refs[2 * nt + 6:3 * nt + 6], refs[3 * nt + 6:]
    rows = 2 * k1p
    bdot = lambda f, x: _dot(f, x.astype(BF16))
    _stage1(u_refs, 0, f1_ref, a_refs, nh, n2, rows, bdot)

    def stage2(k, carry):
        y = _dot(mf_ref[k], _spectrum_rows(a_refs, k, k1p, n2, rows).astype(BF16))
        yr, yi = y[:n2], y[n2:]
        gr, gi = g_ref[0, 0, k], g_ref[0, 1, k]
        z = jnp.concatenate([yr * gr - yi * gi, yr * gi + yi * gr], axis=0).astype(BF16)
        v = _dot(mi_ref[k], z)
        _put_lanes(a_refs, (pl.ds(k, n2, stride=rows), slice(None)), v[:n2])
        _put_lanes(a_refs, (pl.ds(k1p + k, n2, stride=rows), slice(None)), v[n2:])
        return carry
    lax.fori_loop(0, k1p, stage2, 0, unroll=8)

    def inverse1(j, carry):
        v = _lanes(a_refs, (pl.ds(pl.multiple_of(j * rows, SUBLANE), rows), slice(None))).astype(BF16)
        _put_lanes(y_outs, (0, pl.ds(j, nh, stride=n2), slice(None)), _dot(g1_ref[...], v))
        return carry
    lax.fori_loop(0, n2, inverse1, 0, unroll=8)

    n_tok = y_outs[0].shape[1]
    step = math.gcd(n_tok, 512)

    def epilogue(i, carry):
        sl = (0, pl.ds(pl.multiple_of(i * step, step), step), slice(None))
        for t in range(nt):
            y_outs[t][sl] = gate_refs[t][sl] * (
                y_outs[t][sl] + u_refs[t][sl] * skip_ref[:, t * LANE:(t + 1) * LANE])
        return carry
    lax.fori_loop(0, n_tok // step, epilogue, 0)


def _lane_tiles(x):
    if isinstance(x, (list, tuple)):
        return [(a, 0) for a in x]
    return [(x, t) for t in range(x.shape[2] // LANE)]


def _hyena_conv(u, gate, skip, g, order, plan, mats):
    ut, gt = _lane_tiles(u), _lane_tiles(gate)
    nt = len(ut)
    bsz, n_tok = ut[0][0].shape[:2]
    ch = nt * LANE
    nh, n2, k1p = plan.nh, plan.n2, plan.k1p
    rows = 2 * k1p
    tok = lambda t: pl.BlockSpec((1, n_tok, LANE), lambda b: (b, 0, t))
    const = lambda a: pl.BlockSpec(a.shape, lambda b: (0,) * a.ndim, pipeline_mode=pl.Buffered(1))
    return pl.pallas_call(
        functools.partial(_hyena_conv_kernel, nt=nt, nh=nh, n2=n2, k1p=k1p),
        grid=(bsz,),
        in_specs=[tok(t) for _, t in ut] + [tok(t) for _, t in gt]
        + [pl.BlockSpec((None, 1, ch), lambda b: (order, 0, 0)),
           const(mats['f1']), const(mats['g1']), const(mats['mf']), const(mats['mi']),
           pl.BlockSpec((1, 2, k1p, n2, ch), lambda b: (order, 0, 0, 0, 0),
                        pipeline_mode=pl.Buffered(1))],
        out_specs=[tok(0)] * nt,
        out_shape=[jax.ShapeDtypeStruct((bsz, n_tok, LANE), F32)] * nt,
        scratch_shapes=[pltpu.VMEM((n2 * rows, LANE), F32)] * nt,
        compiler_params=_cparams(("parallel",)),
        name="hyena_conv",
    )(*[a for a, _ in ut], *[a for a, _ in gt], skip.reshape(skip.shape[0], 1, ch),
      mats['f1'], mats['g1'], mats['mf'], mats['mi'], g)


def _filter_spectrum_kernel(h_ref, ss_ref, f1_ref, mf_ref, g_out, a_ref, *, nh, n2, k1p):
    rows = 2 * k1p
    s = lax.rsqrt(ss_ref[0] + ss_ref[1] + EPS)
    for direction in range(2):
        _stage1([h_ref], direction, f1_ref, [a_ref], nh, n2, rows, _dot3)

        def stage2(k, carry):
            y = _dot3(mf_ref[k], _spectrum_rows([a_ref], k, k1p, n2, rows))
            if direction == 0:
                g_out[0, 0, k] = y[:n2] * s
                g_out[0, 1, k] = y[n2:] * s
            else:
                g_out[0, 0, k] = g_out[0, 0, k] + y[:n2] * s
                g_out[0, 1, k] = g_out[0, 1, k] - y[n2:] * s
            return carry
        lax.fori_loop(0, k1p, stage2, 0)


def _filter_spectrum(h, ss, plan, mats):
    groups, n_tok, ch = h.shape
    nh, n2, k1p = plan.nh, plan.n2, plan.k1p
    full = lambda a: pl.BlockSpec(a.shape, lambda o, ci: (0,) * a.ndim)
    return pl.pallas_call(
        functools.partial(_filter_spectrum_kernel, nh=nh, n2=n2, k1p=k1p),
        grid=(groups // 2, ch // LANE),
        in_specs=[pl.BlockSpec((2, n_tok, LANE), lambda o, ci: (o, 0, ci)),
                  pl.BlockSpec((2, 1, LANE), lambda o, ci: (o, 0, ci)),
                  full(mats['f1_32']), full(mats['mf32'])],
        out_specs=pl.BlockSpec((1, 2, k1p, n2, LANE), lambda o, ci: (o, 0, 0, 0, ci)),
        out_shape=jax.ShapeDtypeStruct((groups // 2, 2, k1p, n2, ch), F32),
        scratch_shapes=[pltpu.VMEM((n2 * 2 * k1p, LANE), F32)],
        compiler_params=_cparams(("arbitrary", "arbitrary")),
        name="hyena_filter_spectrum",
    )(h, ss, mats['f1_32'], mats['mf32'])


def _filt_kernel(z_ref, w1_ref, b1_ref, w2_ref, b2_ref, fr_ref, w3_ref, dec_ref, h_out, ss_out, *, tl):
    i = pl.program_id(0)
    z = z_ref[...]
    h = jnp.sin(fr_ref[0:1, :] * (_dot3(z, w1_ref[...]) + b1_ref[...]))
    h = jnp.sin(fr_ref[1:2, :] * (_dot3(h, w2_ref[...]) + b2_ref[...]))
    h = _dot3(h, w3_ref[...])
    decay = jnp.exp(-z[:, 0:1] * jnp.abs(dec_ref[...]))
    row = i * tl + lax.broadcasted_iota(jnp.int32, (tl, 1), 0)

    @pl.when(i == 0)
    def _():
        ss_out[...] = jnp.zeros_like(ss_out)

    for g in range(2 * HY_ORDER):
        hg = h[:, g * HY_CH:(g + 1) * HY_CH] * decay
        if g % 2 == 1:
            hg = jnp.where(row > 0, hg, 0.0)
        h_out[g] = hg
        ss_out[g] += jnp.sum(hg * hg, axis=0, keepdims=True)


def _hyena_filter_spectrum(n_tok, i, p, plan, mats):
    t = jnp.linspace(0.0, 1.0, n_tok, dtype=F32)[:, None]
    w = 2.0 * math.pi * jnp.arange(n_tok, dtype=F32) / n_tok
    f = jnp.linspace(1e-4, HY_BANDS - 1, HY_BANDS, dtype=F32)
    fw = w[:, None] * f[None, :]
    feat = jnp.concatenate([t, jnp.cos(fw), -jnp.sin(fw)], axis=-1)
    n_feat = feat.shape[1]
    hid = p['hy_w1'].shape[2]
    feat = jnp.pad(feat, ((0, 0), (0, hid - n_feat)))
    w1 = jnp.pad(p['hy_w1'][i], ((0, hid - n_feat), (0, 0)))
    tl = _tile(n_tok, 512)
    groups = 2 * HY_ORDER
    full = lambda a: pl.BlockSpec(a.shape, lambda s: (0,) * a.ndim)
    wts = [w1, p['hy_b1'][i][None], p['hy_w2'][i], p['hy_b2'][i][None], p['hy_sin_freq'][i],
           p['hy_w3'][i], p['hy_decay'][i][None]]
    h, ss = pl.pallas_call(
        functools.partial(_filt_kernel, tl=tl),
        grid=(n_tok // tl,),
        in_specs=[pl.BlockSpec((tl, hid), lambda s: (s, 0))] + [full(a) for a in wts],
        out_specs=[pl.BlockSpec((groups, tl, HY_CH), lambda s: (0, s, 0)),
                   pl.BlockSpec((groups, 1, HY_CH), lambda s: (0, 0, 0))],
        out_shape=[jax.ShapeDtypeStruct((groups, n_tok, HY_CH), F32),
                   jax.ShapeDtypeStruct((groups, 1, HY_CH), F32)],
        compiler_params=_cparams(("arbitrary",)),
        name="hyena_filter_mlp",
    )(feat, *wts)
    return _filter_spectrum(h, ss, plan, mats)


def _hyena(v, x1, x2, skip, plan, mats, g):
    y = v
    for o, gate in enumerate((x1, x2)):
        y = _hyena_conv(y, gate, skip, g, o, plan, mats)
    return y


def _fft_mats(plan):
    return {'f1': jnp.asarray(plan.f1, BF16), 'f1_32': jnp.asarray(plan.f1, F32),
            'g1': jnp.asarray(plan.g1, BF16), 'mf': jnp.asarray(plan.mf, BF16),
            'mf32': jnp.asarray(plan.mf, F32), 'mi': jnp.asarray(plan.mi, BF16)}


def _cumsum_lanes(x):
    lane = lax.broadcasted_iota(jnp.int32, x.shape, 1)
    s = 1
    while s < x.shape[1]:
        x = x + jnp.where(lane >= s, pltpu.roll(x, s, 1), 0.0)
        s *= 2
    return x


def _topk_kernel(aff_ref, idx_out, gate_out, pos_ref, *, cap, chunk, nb):
    a = aff_ref[0]
    n_e, n_tok = a.shape
    bits = lax.bitcast_convert_type(a, jnp.int32)

    def count(mask):
        return jnp.sum(jnp.where(mask, 1.0, 0.0), axis=1, keepdims=True)

    def bisect(_, lohi):
        lo, hi = lohi
        mid = lo + ((hi - lo) >> 1)
        ok = count(bits >= mid) >= cap
        return jnp.where(ok, mid, lo), jnp.where(ok, hi, mid)

    lo0 = jnp.zeros((n_e, 1), jnp.int32)
    hi0 = jnp.full((n_e, 1), 0x7F800000, jnp.int32)
    thr, _ = lax.fori_loop(0, 31, bisect, (lo0, hi0))
    gt = bits > thr
    eqf = jnp.where(bits == thr, 1.0, 0.0)
    need = cap - count(gt)
    rank = _cumsum_lanes(eqf) - eqf
    self_ = jnp.where(gt, 1.0, jnp.where(rank < need, eqf, 0.0))
    pos_ref[...] = _cumsum_lanes(self_) * self_

    na = cap // nb
    tok = lax.broadcasted_iota(jnp.int32, (1, n_tok), 1)
    t_hi = (tok >> 6).astype(F32)
    t_lo = (tok & 63).astype(F32)
    row_a = lax.broadcasted_iota(jnp.int32, (na, 1), 0)
    row_b = lax.broadcasted_iota(jnp.int32, (nb, 1), 0)
    nt = (((1,), (1,)), ((), ()))

    def per_expert(e, carry):
        aff = aff_ref[0, pl.ds(e, 1), :]
        a_hi = aff.astype(BF16).astype(F32)
        a_mid = (aff - a_hi).astype(BF16).astype(F32)
        a_lo = aff - a_hi - a_mid
        slot = pos_ref[pl.ds(e, 1), :].astype(jnp.int32) - 1
        hi, lo = slot >> (nb.bit_length() - 1), slot & (nb - 1)
        acc = jnp.zeros((na, 5 * nb), F32)
        for c in range(n_tok // chunk):
            sl = slice(c * chunk, (c + 1) * chunk)
            one_a = jnp.where(hi[:, sl] == row_a, 1.0, 0.0).astype(BF16)
            in_b = lo[:, sl] == row_b
            pay = jnp.concatenate([jnp.where(in_b, v[:, sl], 0.0) for v in (t_hi, t_lo, a_hi, a_mid, a_lo)],
                                  axis=0).astype(BF16)
            acc = acc + lax.dot_general(one_a, pay, nt, preferred_element_type=F32)
        rows = pl.ds(pl.multiple_of(e * na, na), na) if isinstance(e, jax.Array) else slice(e * na, (e + 1) * na)
        idx_out[0, rows, :] = (acc[:, :nb] * 64.0 + acc[:, nb:2 * nb]).astype(jnp.int32)
        gate_out[0, rows, :] = acc[:, 2 * nb:3 * nb] + acc[:, 3 * nb:4 * nb] + acc[:, 4 * nb:]
        return carry

    if na % SUBLANE == 0:
        lax.fori_loop(0, n_e, per_expert, 0)
    else:
        for e in range(n_e):
            per_expert(e, 0)


def _route(aff_t, cap):
    bsz, n_e, n_tok = aff_t.shape
    nb = 16
    out = pl.BlockSpec((1, n_e * cap // nb, nb), lambda b: (b, 0, 0))
    idx, gates = pl.pallas_call(
        functools.partial(_topk_kernel, cap=cap, chunk=_tile(n_tok, 1024), nb=nb),
        grid=(bsz,), in_specs=[pl.BlockSpec((1, n_e, n_tok), lambda b: (b, 0, 0))],
        out_specs=[out, out],
        out_shape=[jax.ShapeDtypeStruct((bsz, n_e * cap // nb, nb), jnp.int32),
                   jax.ShapeDtypeStruct((bsz, n_e * cap // nb, nb), F32)],
        scratch_shapes=[pltpu.VMEM((n_e, n_tok), F32)],
        compiler_params=_cparams(("parallel",)),
        name="topk_route",
    )(aff_t)
    return idx.reshape(bsz, n_e, cap), gates.reshape(bsz, n_e, cap)


def kernel(x, c, ctx, c_ctx, mod_w, mod_b, norm1_g, w_in, q_a_g, w_q_b, kv_a_g, w_kv_b, conv_dw_w, conv_dw_b, conv_ln_g, conv_ln_b, hy_short_w, hy_short_b, hy_w1, hy_b1, hy_w2, hy_b2, hy_w3, hy_sin_freq, hy_decay, hy_skip, group_norm_g, w_out, norm2_g, router_w, w_gate, w_up, w_down, final_norm_g):
    p = dict(mod_w=mod_w, mod_b=mod_b, norm1_g=norm1_g, w_in=w_in, q_a_g=q_a_g, w_q_b=w_q_b,
             kv_a_g=kv_a_g, w_kv_b=w_kv_b, conv_dw_w=conv_dw_w, conv_dw_b=conv_dw_b,
             conv_ln_g=conv_ln_g, conv_ln_b=conv_ln_b, hy_short_w=hy_short_w,
             hy_short_b=hy_short_b, hy_w1=hy_w1, hy_b1=hy_b1, hy_w2=hy_w2, hy_b2=hy_b2,
             hy_w3=hy_w3, hy_sin_freq=hy_sin_freq, hy_decay=hy_decay, hy_skip=hy_skip,
             group_norm_g=group_norm_g, w_out=w_out, norm2_g=norm2_g, router_w=router_w,
             w_gate=w_gate, w_up=w_up, w_down=w_down)
    depth = mod_w.shape[0]
    bsz, n_lat, d = x.shape
    n_ctx = ctx.shape[1]

    rows = -(-(bsz + 1) // 8) * 8
    cc = jnp.concatenate([c, c_ctx[None, :], jnp.zeros((rows - bsz - 1, d), F32)], axis=0)
    mod_all = _modulation(cc, mod_w, mod_b)
    lat_row = lambda b: b
    ctx_row = lambda b: bsz

    tabs_l = _rope_tables(n_lat)
    tabs_c = _identity_tables(n_ctx)
    tm_l, tm_c = _tile(n_lat, 512), _tile(n_ctx, 256)
    tq_l, tq_c = _tile(n_lat, 256), _tile(n_ctx, 256)
    tl_l, tl_c = _tile(n_lat, 512), _tile(n_ctx, 256)

    xl, xc = x, ctx
    prev_l = prev_c = None
    for i in range(depth):
        last = i == depth - 1
        lw = _layer_weights(i, p)
        mod = mod_all[i].reshape(rows, 1, N_MOD * d)

        def side(xs, prev, row_of_b, tabs, tm, tq, tl, extra_keys, need_mix):
            xs, q, k, v, uc, uh = _inproj(xs, prev, mod, row_of_b, lw, tabs, tm)
            if not need_mix:
                return xs, (k, v), None
            att = _attention(q, [(k, v)] + extra_keys, tq)
            cn, hv, hx1, hx2 = _conv(uc, uh, lw, tl)
            plan = _fft_plan(xs.shape[1])
            mats = _fft_mats(plan)
            filt = _hyena_filter_spectrum(xs.shape[1], i, p, plan, mats)
            hy = _hyena(hv, hx1, hx2, p['hy_skip'][i], plan, mats, filt)
            x1, h2p, aff_t = _outproj(att, cn, hy, xs, mod, row_of_b, lw, tm)
            cap = CAPACITY_FACTOR * xs.shape[1] // N_EXPERTS
            idx, gates = _route(aff_t, cap)
            moe = _moe(h2p, idx, gates, lw)
            return x1, (k, v), moe

        xc, kv_c, moe_c = side(xc, prev_c, ctx_row, tabs_c, tm_c, tq_c, tl_c, [], not last)
        xl, _, moe_l = side(xl, prev_l, lat_row, tabs_l, tm_l, tq_l, tl_l, [kv_c], True)
        prev_l = (moe_l, mod)
        prev_c = None if moe_c is None else (moe_c, mod)

    mod = mod_all[depth - 1].reshape(rows, 1, N_MOD * d)
    return _final(xl, prev_l[0], mod, final_norm_g.reshape(1, d), tm_l)
```

```python
import functools
import math
from typing import NamedTuple

import numpy as np
import jax
import jax.numpy as jnp
from jax import lax
from jax.experimental import pallas as pl
from jax.experimental.pallas import tpu as pltpu

F32 = jnp.float32
BF16 = jnp.bfloat16
EPS = 1e-6

GRID_W = 64
N_MOD = 6
HEADS = 8
NOPE = 64
ROPE = 32
VDIM = 64
Q_LORA = 256
KV_LORA = 128
ROPE_BASE = 10000.0
CONV_CH = 256
CONV_K = 31
HY_CH = 256
HY_ORDER = 2
HY_BANDS = 16
N_EXPERTS = 16
CAPACITY_FACTOR = 2

LANE = 128
SUBLANE = 8
MXU_DIM = 256
HEAD_PAD = LANE
BF16_ROWS = 16
V_ROWS = -(-(VDIM + 1) // BF16_ROWS) * BF16_ROWS
HALO = 16
VMEM_LIMIT = 56 * 1024 * 1024
MOE_ACC_BUDGET = 16 * 1024 * 1024

C_Q = 0
C_KVN = C_Q + Q_LORA
C_KR = C_KVN + KV_LORA
C_CONV = C_KR + LANE
C_HY = C_CONV + 2 * CONV_CH
C_END = C_HY + 3 * HY_CH


def _cparams(sem):
    return pltpu.CompilerParams(dimension_semantics=sem, vmem_limit_bytes=VMEM_LIMIT)


def _rms(x, g):
    return x * lax.rsqrt(jnp.mean(x * x, axis=-1, keepdims=True) + EPS) * g


def _split(a):
    hi = a.astype(BF16)
    lo = (a - hi.astype(F32)).astype(BF16)
    return hi, lo


def _dot(a, b):
    return jnp.dot(a, b, preferred_element_type=F32)


def _from_token_major(ref, n_rows):
    fc = ref.shape[1] // n_rows
    return jnp.concatenate([ref[0, pl.ds(c, n_rows, stride=fc), :] for c in range(fc)], axis=1)


def _mod_kernel(c_ref, w_ref, b_ref, o_ref):
    c = c_ref[...]
    a = c * jax.nn.sigmoid(c)
    a_hi, a_lo = _split(a)
    w_hi, w_lo = _split(w_ref[0])
    o_ref[0] = _dot(a_hi, w_hi) + _dot(a_lo, w_hi) + _dot(a_hi, w_lo) + b_ref[0]


def _modulation(cc, mod_w, mod_b):
    depth, d, n = mod_w.shape
    rows = cc.shape[0]
    tn = 1536
    return pl.pallas_call(
        _mod_kernel,
        grid=(depth, n // tn),
        in_specs=[
            pl.BlockSpec((rows, d), lambda l, j: (0, 0)),
            pl.BlockSpec((1, d, tn), lambda l, j: (l, 0, j)),
            pl.BlockSpec((1, 1, tn), lambda l, j: (l, 0, j)),
        ],
        out_specs=pl.BlockSpec((1, rows, tn), lambda l, j: (l, 0, j)),
        out_shape=jax.ShapeDtypeStruct((depth, rows, n), F32),
        compiler_params=_cparams(("arbitrary", "arbitrary")),
        name="modulation",
    )(cc, mod_w, mod_b.reshape(depth, 1, n))


def _rope(x, c, s1, s2):
    return x * c + pltpu.roll(x, LANE - 8, 1) * s1 + pltpu.roll(x, 8, 1) * s2


def _inproj_kernel(*refs, fuse_prev, scale):
    if fuse_prev:
        xa_ref, xb_ref, g2_ref = refs[:3]
        refs = refs[3:]
    else:
        xa_ref = refs[0]
        refs = refs[1:]
    (sh_ref, sc_ref, n1g_ref, win_ref, qag_ref, wq_ref, kvg_ref, wk_ref, wv_ref,
     c_ref, s1_ref, s2_ref, vone_ref) = refs[:13]
    outs = refs[13:]
    if fuse_prev:
        x_out, q_out, k_out, v_out, uc_out, uh_out = outs
        x = xa_ref[0] + g2_ref[...] * _from_token_major(xb_ref, xa_ref.shape[1])
        x_out[0] = x
    else:
        q_out, k_out, v_out, uc_out, uh_out = outs
        x = xa_ref[0]
    hn = _rms(x, n1g_ref[...]) * (1.0 + sc_ref[...]) + sh_ref[...]
    u = _dot(hn.astype(BF16), win_ref[...])
    c, s1, s2 = c_ref[...], s1_ref[...], s2_ref[...]

    nq = _rms(u[:, C_Q:C_KVN], qag_ref[...]).astype(BF16)
    q = _dot(nq, wq_ref[...])
    for h in range(HEADS):
        sl = slice(h * HEAD_PAD, (h + 1) * HEAD_PAD)
        q_out[0, :, sl] = (_rope(q[:, sl], c, s1, s2) * scale).astype(BF16)

    kvn = _rms(u[:, C_KVN:C_KR], kvg_ref[...]).astype(BF16)
    kr = _rope(u[:, C_KR:C_CONV], c, s1, s2)
    k = _dot(kvn, wk_ref[...])
    for h in range(HEADS):
        sl = slice(h * HEAD_PAD, (h + 1) * HEAD_PAD)
        k_out[0, :, sl] = (k[:, sl] + kr).astype(BF16)
    vt = lax.dot_general(wv_ref[...], kvn, (((1,), (1,)), ((), ())), preferred_element_type=F32)
    v_out[0] = (vt + vone_ref[...]).astype(BF16)
    uc_out[0] = u[:, C_CONV:C_HY].astype(BF16)
    uh_out[0] = u[:, C_HY:C_END].astype(BF16)


def _inproj(x, prev, mod, row_of_b, lw, tabs, tm):
    bsz, n_tok, d = x.shape
    fuse_prev = prev is not None
    grid = (bsz, n_tok // tm)
    tok = lambda w: pl.BlockSpec((1, tm, w), lambda b, i: (b, i, 0))
    modspec = lambda k: pl.BlockSpec((None, 1, d), lambda b, i: (row_of_b(b), 0, k))
    full = lambda a: pl.BlockSpec(a.shape, lambda b, i: (0,) * a.ndim)
    tabspec = pl.BlockSpec((tm, LANE), lambda b, i: (i, 0))

    args, specs = [x], [tok(d)]
    if fuse_prev:
        args += [prev[0], prev[1]]
        specs += [pl.BlockSpec((1, tm * (d // LANE), LANE), lambda b, i: (b, i, 0)), modspec(5)]
    args += [mod, mod, lw['norm1_g'], lw['w_in'], lw['q_a_g'], lw['w_q'], lw['kv_a_g'],
             lw['w_k'], lw['w_v'], tabs[0], tabs[1], tabs[2], lw['v_one']]
    specs += [modspec(0), modspec(1), full(lw['norm1_g']), full(lw['w_in']), full(lw['q_a_g']),
              full(lw['w_q']), full(lw['kv_a_g']), full(lw['w_k']), full(lw['w_v']),
              tabspec, tabspec, tabspec, full(lw['v_one'])]
    hp = HEADS * HEAD_PAD
    out_shape, out_specs = [], []
    if fuse_prev:
        out_shape.append(jax.ShapeDtypeStruct((bsz, n_tok, d), F32))
        out_specs.append(tok(d))
    out_shape += [jax.ShapeDtypeStruct((bsz, n_tok, hp), BF16)] * 2
    out_specs += [tok(hp)] * 2
    out_shape += [jax.ShapeDtypeStruct((bsz, hp, n_tok), BF16),
                  jax.ShapeDtypeStruct((bsz, n_tok, 2 * CONV_CH), BF16),
                  jax.ShapeDtypeStruct((bsz, n_tok, 3 * HY_CH), BF16)]
    out_specs += [pl.BlockSpec((1, hp, tm), lambda b, i: (b, 0, i)),
                  tok(2 * CONV_CH), tok(3 * HY_CH)]
    scale = float((NOPE + ROPE) ** -0.5 * math.log2(math.e))
    res = pl.pallas_call(
        functools.partial(_inproj_kernel, fuse_prev=fuse_prev, scale=scale),
        grid=grid, in_specs=specs, out_specs=out_specs, out_shape=out_shape,
        compiler_params=_cparams(("parallel", "parallel")),
        name="inproj",
    )(*args)
    if not fuse_prev:
        res = [x] + list(res)
    return res


def _attn_kernel(*refs, n_sets, hps, tk, ahead):
    q_ref = refs[0]
    kv = refs[1:1 + 2 * n_sets]
    o_ref = refs[1 + 2 * n_sets]
    nt = (((1,), (1,)), ((), ()))
    sls = [slice(hh * HEAD_PAD, (hh + 1) * HEAD_PAD) for hh in range(hps)]
    qs = [q_ref[0, :, sl] for sl in sls]
    tks = [min(tk, kv[2 * i].shape[1]) for i in range(n_sets)]
    items = [(hh, i, c) for hh in range(hps) for i in range(n_sets)
             for c in range(kv[2 * i].shape[1] // tks[i])]

    def scores(item):
        hh, i, c = item
        return lax.dot_general(kv[2 * i][0, c * tks[i]:(c + 1) * tks[i], sls[hh]], qs[hh], nt,
                               preferred_element_type=F32)

    m, o = [None] * hps, [None] * hps
    pending = [scores(it) for it in items[:ahead]]
    for n, (hh, i, c) in enumerate(items):
        s = pending.pop(0)
        if n + ahead < len(items):
            pending.append(scores(items[n + ahead]))
        mt = jnp.max(s, axis=0, keepdims=True)
        m_new = mt if m[hh] is None else jnp.maximum(m[hh], mt)
        pv = _dot(kv[2 * i + 1][0, hh * HEAD_PAD:hh * HEAD_PAD + V_ROWS, c * tks[i]:(c + 1) * tks[i]],
                  jnp.exp2(s - m_new).astype(BF16))
        o[hh] = pv if o[hh] is None else o[hh] * jnp.exp2(m[hh] - m_new) + pv
        m[hh] = m_new
    pad = jnp.zeros((HEAD_PAD - V_ROWS, o[0].shape[1]), F32)
    outs = [jnp.concatenate([oh * (1.0 / oh[VDIM:VDIM + 1, :]), pad], axis=0).T for oh in o]
    lane = lax.broadcasted_iota(jnp.int32, outs[0].shape, 1)
    for pr in range(hps // 2):
        o_ref[0, :, pr * LANE:(pr + 1) * LANE] = jnp.where(
            lane < VDIM, outs[2 * pr], pltpu.roll(outs[2 * pr + 1], VDIM, 1))


def _attention(q, key_sets, tq):
    bsz, n_q, hp = q.shape
    hps = 8
    grid = (bsz, HEADS // hps, n_q // tq)
    args = [q]
    specs = [pl.BlockSpec((1, tq, hps * HEAD_PAD), lambda b, h, i: (b, i, h))]
    for k, v in key_sets:
        n_k = k.shape[1]
        args += [k, v]
        specs += [pl.BlockSpec((1, n_k, hps * HEAD_PAD), lambda b, h, i: (b, 0, h)),
                  pl.BlockSpec((1, hps * HEAD_PAD, n_k), lambda b, h, i: (b, h, 0))]
    return pl.pallas_call(
        functools.partial(_attn_kernel, n_sets=len(key_sets), hps=hps,
                          tk=MXU_DIM, ahead=5),
        grid=grid, in_specs=specs,
        out_specs=pl.BlockSpec((1, tq, hps * VDIM), lambda b, h, i: (b, i, h)),
        out_shape=jax.ShapeDtypeStruct((bsz, n_q, HEADS * VDIM), F32),
        compiler_params=_cparams(("parallel", "parallel", "arbitrary")),
        name="attention",
    )(*args)


def _conv_kernel(ucp_ref, uc_ref, ucn_ref, uhp_ref, uh_ref, uhn_ref,
                 cw_ref, cb_ref, lg_ref, lb_ref, gn_ref, hw_ref, hb_ref,
                 cn_out, v_out, x1_out, x2_out, ypad, hpad, yph, *, tl, row_tile):
    i = pl.program_id(1)
    has_prev = (i > 0).astype(F32)
    has_next = (i < pl.num_programs(1) - 1).astype(F32)

    def glu(u):
        return u[:, :CONV_CH] * jax.nn.sigmoid(u[:, CONV_CH:])

    f32 = lambda ref: ref[0].astype(F32)
    ypad[0:HALO, :] = glu(f32(ucp_ref)) * has_prev
    ypad[HALO:HALO + tl, :] = glu(f32(uc_ref))
    ypad[HALO + tl:, :] = glu(f32(ucn_ref)) * has_next
    hpad[0:HALO, :] = f32(uhp_ref) * has_prev
    hpad[HALO:HALO + tl, :] = f32(uh_ref)
    hpad[HALO + tl:, :] = f32(uhn_ref) * has_next

    span = tl + 2 * HALO - SUBLANE
    for ph in range(1, SUBLANE):
        yph[ph] = ypad[ph:ph + span, :]

    half = CONV_K // 2
    for r in range(tl // row_tile):
        base = r * row_tile
        acc = jnp.zeros((row_tile, CONV_CH), F32) + cb_ref[...]
        for k in range(CONV_K):
            off = HALO + base + k - half
            ph = off % SUBLANE
            src = ypad[off:off + row_tile, :] if ph == 0 else yph[ph, off - ph:off - ph + row_tile, :]
            acc = acc + cw_ref[k:k + 1, :] * src
        mu = jnp.mean(acc, axis=-1, keepdims=True)
        cen = acc - mu
        var = jnp.mean(cen * cen, axis=-1, keepdims=True)
        y = cen * lax.rsqrt(var + EPS) * lg_ref[...] + lb_ref[...]
        y = y * jax.nn.sigmoid(y)
        cn_out[0, base:base + row_tile, :] = _rms(y, gn_ref[...])

        z = jnp.zeros((row_tile, 3 * HY_CH), F32) + hb_ref[...]
        for k in range(3):
            off = HALO + base + k - 1
            z = z + hw_ref[k:k + 1, :] * hpad[off:off + row_tile, :]
        v_out[0, base:base + row_tile, :] = z[:, :HY_CH]
        x1_out[0, base:base + row_tile, :] = z[:, HY_CH:2 * HY_CH]
        x2_out[0, base:base + row_tile, :] = z[:, 2 * HY_CH:]


def _conv(uc, uh, lw, tl):
    bsz, n_tok, _ = uc.shape
    nh = tl // HALO
    last = n_tok // HALO - 1
    cur = lambda w: pl.BlockSpec((1, tl, w), lambda b, i: (b, i, 0))
    prv = lambda w: pl.BlockSpec((1, HALO, w), lambda b, i: (b, jnp.maximum(i * nh - 1, 0), 0))
    nxt = lambda w: pl.BlockSpec((1, HALO, w), lambda b, i: (b, jnp.minimum((i + 1) * nh, last), 0))
    full = lambda a: pl.BlockSpec(a.shape, lambda b, i: (0,) * a.ndim)
    wts = [lw['conv_dw_w'], lw['conv_dw_b'], lw['conv_ln_g'], lw['conv_ln_b'], lw['gn_conv'],
           lw['hy_short_w'], lw['hy_short_b']]
    return pl.pallas_call(
        functools.partial(_conv_kernel, tl=tl, row_tile=min(64, tl)),
        grid=(bsz, n_tok // tl),
        in_specs=[prv(2 * CONV_CH), cur(2 * CONV_CH), nxt(2 * CONV_CH),
                  prv(3 * HY_CH), cur(3 * HY_CH), nxt(3 * HY_CH)] + [full(w) for w in wts],
        out_specs=[cur(CONV_CH)] + [cur(HY_CH)] * 3,
        out_shape=[jax.ShapeDtypeStruct((bsz, n_tok, CONV_CH), F32)]
        + [jax.ShapeDtypeStruct((bsz, n_tok, HY_CH), F32)] * 3,
        scratch_shapes=[pltpu.VMEM((tl + 2 * HALO, CONV_CH), F32),
                        pltpu.VMEM((tl + 2 * HALO, 3 * HY_CH), F32),
                        pltpu.VMEM((SUBLANE, tl + 2 * HALO - SUBLANE, CONV_CH), F32)],
        compiler_params=_cparams(("parallel", "parallel")),
        name="conv",
    )(uc, uc, uc, uh, uh, uh, *wts)


def _outproj_kernel(att_ref, cn_ref, *refs, n_hy):
    hy_refs = refs[:n_hy]
    (x_ref, g1_ref, sh_ref, sc_ref, gna_ref, gnh_ref, wo_ref, n2g_ref, rw_ref,
     x1_out, h2p_out, aff_out) = refs[n_hy:]
    a = _rms(att_ref[0], gna_ref[...]).astype(BF16)
    c = cn_ref[0].astype(BF16)
    h = _rms(_lanes(hy_refs, 0), gnh_ref[...]).astype(BF16)
    na, nc = a.shape[1], c.shape[1]
    y = (_dot(a, wo_ref[0:na, :]) + _dot(c, wo_ref[na:na + nc, :]) + _dot(h, wo_ref[na + nc:, :]))
    x1 = x_ref[0] + g1_ref[...] * y
    x1_out[0] = x1
    h2 = _rms(x1, n2g_ref[...]) * (1.0 + sc_ref[...]) + sh_ref[...]
    d = h2.shape[1]
    lo = lax.bitcast_convert_type(h2[:, :d // 2].astype(BF16).astype(F32), jnp.uint32)
    hi = lax.bitcast_convert_type(h2[:, d // 2:].astype(BF16).astype(F32), jnp.uint32)
    packed = hi | (lo >> 16)
    n_rows, pc = packed.shape[0], packed.shape[1] // LANE
    for c in range(pc):
        h2p_out[0, pl.ds(c, n_rows, stride=pc), :] = packed[:, c * LANE:(c + 1) * LANE]
    h_hi, h_lo = _split(h2)
    t = _dot(h_hi, rw_ref[...])
    logits = t[:, :LANE] + t[:, LANE:] + _dot(h_lo, rw_ref[:, :LANE])
    lane = lax.broadcasted_iota(jnp.int32, logits.shape, 1)
    logits = jnp.where(lane < N_EXPERTS, logits, -1e30)
    e = jnp.exp(logits - jnp.max(logits, axis=1, keepdims=True))
    aff = e / jnp.sum(e, axis=1, keepdims=True)
    aff_out[0] = aff.T[:N_EXPERTS, :]


def _outproj(att, cn, hy, x, mod, row_of_b, lw, tm):
    bsz, n_tok, d = x.shape
    tok = lambda w: pl.BlockSpec((1, tm, w), lambda b, i: (b, i, 0))
    modspec = lambda k: pl.BlockSpec((None, 1, d), lambda b, i: (row_of_b(b), 0, k))
    full = lambda a: pl.BlockSpec(a.shape, lambda b, i: (0,) * a.ndim)
    wts = [lw['gn_att'], lw['gn_hy'], lw['w_out'], lw['norm2_g'], lw['router_w']]
    pc = d // 2 // LANE
    return pl.pallas_call(
        functools.partial(_outproj_kernel, n_hy=len(hy)),
        grid=(bsz, n_tok // tm),
        in_specs=[tok(att.shape[2]), tok(cn.shape[2])] + [tok(LANE)] * len(hy) + [tok(d),
                  modspec(2), modspec(3), modspec(4)] + [full(w) for w in wts],
        out_specs=[tok(d), pl.BlockSpec((1, tm * pc, LANE), lambda b, i: (b, i, 0)),
                   pl.BlockSpec((1, N_EXPERTS, tm), lambda b, i: (b, 0, i))],
        out_shape=[jax.ShapeDtypeStruct((bsz, n_tok, d), F32),
                   jax.ShapeDtypeStruct((bsz, n_tok * pc, LANE), jnp.uint32),
                   jax.ShapeDtypeStruct((bsz, N_EXPERTS, n_tok), F32)],
        compiler_params=_cparams(("parallel", "parallel")),
        name="outproj",
    )(att, cn, *hy, x, mod, mod, mod, *wts)


def _moe_kernel(idx_ref, idx_nxt_ref, idx_prv_ref, gate_ref, gate_prv_ref, h2p_ref,
                wg_ref, wu_ref, wd_ref, out_hbm, acc_ref, xg0, xg1, y0, y1, sem, *, cap, pc, fc):
    b, e = pl.program_id(0), pl.program_id(1)
    n_e = pl.num_programs(1)
    group = math.gcd(cap, SUBLANE)

    def gather_rows(idx_r, xg, js):
        rows = [h2p_ref[0, pl.ds(pl.multiple_of(idx_r[0, 0, j] * pc, pc), pc), :] for j in js]
        for j, row in zip(js, rows):
            xg[pl.ds(pl.multiple_of(j * pc, pc), pc), :] = row

    def scatter_rows(idx_r, gate_r, y_r, js):
        dsts = [pl.ds(pl.multiple_of(idx_r[0, 0, j] * fc, fc), fc) for j in js]
        new = [acc_ref[dst, :] + gate_r[0, 0, j] * y_r[pl.ds(pl.multiple_of(j * fc, fc), fc), :]
               for j, dst in zip(js, dsts)]
        for dst, val in zip(dsts, new):
            acc_ref[dst, :] = val

    def in_groups(fn, unrolled):
        if unrolled:
            for jg in range(cap // group):
                fn([jg * group + u for u in range(group)])
        else:
            def body(jg, carry):
                fn([jg * group + u for u in range(group)])
                return carry
            lax.fori_loop(0, cap // group, body, 0)

    @pl.when(e == 0)
    def _():
        acc_ref[...] = jnp.zeros_like(acc_ref)
        y1[...] = jnp.zeros_like(y1)
        in_groups(functools.partial(gather_rows, idx_ref, xg0), False)

    def step(xg_cur, xg_nxt, y_cur, y_prv):
        in_groups(functools.partial(gather_rows, idx_nxt_ref, xg_nxt), True)
        lo, hi = [], []
        for c in range(pc):
            w = xg_cur[pl.ds(c, cap, stride=pc), :]
            lo.append(lax.bitcast_convert_type(w << 16, F32).astype(BF16))
            hi.append(lax.bitcast_convert_type(w & jnp.uint32(0xFFFF0000), F32).astype(BF16))
        x = jnp.concatenate(lo + hi, axis=1)
        a = _dot(x, wg_ref[0])
        u = _dot(x, wu_ref[0])
        hmid = (a * jax.nn.sigmoid(a) * u).astype(BF16)
        y = _dot(hmid, wd_ref[0])
        in_groups(functools.partial(scatter_rows, idx_prv_ref, gate_prv_ref, y_prv), True)
        for c in range(fc):
            y_cur[pl.ds(c, cap, stride=fc), :] = y[:, c * LANE:(c + 1) * LANE]

    @pl.when(e % 2 == 0)
    def _():
        step(xg0, xg1, y0, y1)

    @pl.when(e % 2 == 1)
    def _():
        step(xg1, xg0, y1, y0)

    def finish(y_last):
        in_groups(functools.partial(scatter_rows, idx_ref, gate_ref, y_last), False)
        cp = pltpu.make_async_copy(acc_ref, out_hbm.at[b], sem)
        cp.start()
        cp.wait()

    @pl.when((e == n_e - 1) & (e % 2 == 0))
    def _():
        finish(y0)

    @pl.when((e == n_e - 1) & (e % 2 == 1))
    def _():
        finish(y1)


def _moe(h2p, idx, gates, lw):
    d = lw['w_gate'].shape[1]
    pc, fc = d // 2 // LANE, d // LANE
    bsz, n_tok = h2p.shape[0], h2p.shape[1] // pc
    n_e, cap = idx.shape[1], idx.shape[2]
    ff = lw['w_gate'].shape[2]
    if bsz > 1 and bsz * n_tok * d * 4 <= MOE_ACC_BUDGET:
        offs = (jnp.arange(bsz, dtype=jnp.int32) * n_tok)[:, None, None]
        idx = jnp.swapaxes(idx + offs, 0, 1).reshape(1, n_e, bsz * cap)
        gates = jnp.swapaxes(gates, 0, 1).reshape(1, n_e, bsz * cap)
        out = _moe(h2p.reshape(1, bsz * n_tok * pc, LANE), idx, gates, lw)
        return out.reshape(bsz, n_tok * fc, LANE)
    last = n_e - 1
    smem = lambda f: pl.BlockSpec((1, 1, cap), lambda b, e: (b * n_e + f(e), 0, 0),
                                  memory_space=pltpu.SMEM)
    cur, nxt, prv = (lambda e: e), (lambda e: jnp.minimum(e + 1, last)), (lambda e: jnp.maximum(e - 1, 0))
    idx_r, gates_r = idx.reshape(bsz * n_e, 1, cap), gates.reshape(bsz * n_e, 1, cap)
    return pl.pallas_call(
        functools.partial(_moe_kernel, cap=cap, pc=pc, fc=fc),
        grid=(bsz, n_e),
        in_specs=[smem(cur), smem(nxt), smem(prv), smem(cur), smem(prv),
                  pl.BlockSpec((1, n_tok * pc, LANE), lambda b, e: (b, 0, 0)),
                  pl.BlockSpec((1, d, ff), lambda b, e: (e, 0, 0)),
                  pl.BlockSpec((1, d, ff), lambda b, e: (e, 0, 0)),
                  pl.BlockSpec((1, ff, d), lambda b, e: (e, 0, 0))],
        out_specs=pl.BlockSpec(memory_space=pl.ANY),
        out_shape=jax.ShapeDtypeStruct((bsz, n_tok * fc, LANE), F32),
        scratch_shapes=[pltpu.VMEM((n_tok * fc, LANE), F32)]
        + [pltpu.VMEM((cap * pc, LANE), jnp.uint32)] * 2
        + [pltpu.VMEM((cap * fc, LANE), F32)] * 2
        + [pltpu.SemaphoreType.DMA(())],
        compiler_params=_cparams(("arbitrary", "arbitrary")),
        name="moe",
    )(idx_r, idx_r, idx_r, gates_r, gates_r, h2p, lw['w_gate'], lw['w_up'], lw['w_down'])


def _final_kernel(x_ref, m_ref, g2_ref, fg_ref, o_ref):
    o_ref[0] = _rms(x_ref[0] + g2_ref[...] * _from_token_major(m_ref, x_ref.shape[1]), fg_ref[...])


def _final(x1, moe, mod, fg, tm):
    bsz, n_tok, d = x1.shape
    tok = pl.BlockSpec((1, tm, d), lambda b, i: (b, i, 0))
    return pl.pallas_call(
        _final_kernel,
        grid=(bsz, n_tok // tm),
        in_specs=[tok, pl.BlockSpec((1, tm * (d // LANE), LANE), lambda b, i: (b, i, 0)),
                  pl.BlockSpec((None, 1, d), lambda b, i: (b, 0, 5)),
                  pl.BlockSpec((1, d), lambda b, i: (0, 0))],
        out_specs=tok,
        out_shape=jax.ShapeDtypeStruct((bsz, n_tok, d), F32),
        compiler_params=_cparams(("parallel", "parallel")),
        name="final_norm",
    )(x1, moe, mod, fg)


def _rope_tables(n_tok):
    rows = n_tok // GRID_W
    row = jnp.repeat(jnp.arange(rows, dtype=F32), GRID_W)
    col = jnp.tile(jnp.arange(GRID_W, dtype=F32), rows)
    half = ROPE // 2
    inv = ROPE_BASE ** (-jnp.arange(0, half, 2, dtype=F32) / half)
    cr, sr = jnp.cos(row[:, None] * inv), jnp.sin(row[:, None] * inv)
    cc, sc = jnp.cos(col[:, None] * inv), jnp.sin(col[:, None] * inv)
    z8 = jnp.zeros_like(cr)
    ones = jnp.ones((n_tok, NOPE), F32)
    pad = jnp.zeros((n_tok, HEAD_PAD - NOPE - ROPE), F32)
    c = jnp.concatenate([ones, cr, cr, cc, cc, pad + 1.0], axis=1)
    s1 = jnp.concatenate([ones * 0.0, -sr, z8, -sc, z8, pad], axis=1)
    s2 = jnp.concatenate([ones * 0.0, z8, sr, z8, sc, pad], axis=1)
    return c, s1, s2


def _identity_tables(n_tok):
    return (jnp.ones((n_tok, HEAD_PAD), F32), jnp.zeros((n_tok, HEAD_PAD), F32),
            jnp.zeros((n_tok, HEAD_PAD), F32))


def _pad_heads(w, per_head, take):
    k = w.shape[0]
    w = w.reshape(k, HEADS, per_head)[:, :, take]
    w = jnp.pad(w, ((0, 0), (0, 0), (0, HEAD_PAD - w.shape[2])))
    return w.reshape(k, HEADS * HEAD_PAD).astype(BF16)


def _layer_weights(i, p):
    d = p['w_in'].shape[1]
    w_in = p['w_in'][i]
    off_kv = Q_LORA
    off_conv = off_kv + KV_LORA + ROPE
    off_hy = off_conv + 2 * CONV_CH
    kr = jnp.zeros((d, LANE), F32).at[:, NOPE:NOPE + ROPE].set(w_in[:, off_kv + KV_LORA:off_conv])
    w_in_r = jnp.concatenate([w_in[:, :off_kv], w_in[:, off_kv:off_kv + KV_LORA], kr,
                              w_in[:, off_conv:off_hy], w_in[:, off_hy:]], axis=1).astype(BF16)
    gn = p['group_norm_g'][i]
    n_att = HEADS * VDIM
    rw_hi, rw_lo = _split(jnp.pad(p['router_w'][i], ((0, 0), (0, LANE - N_EXPERTS))))
    v_one = jnp.zeros((HEADS, HEAD_PAD), F32).at[:, VDIM].set(1.0).reshape(HEADS * HEAD_PAD, 1)
    row = lambda a: a.reshape(1, -1)
    return {
        'norm1_g': row(p['norm1_g'][i]), 'w_in': w_in_r,
        'q_a_g': row(p['q_a_g'][i]), 'w_q': _pad_heads(p['w_q_b'][i], NOPE + ROPE, slice(None)),
        'kv_a_g': row(p['kv_a_g'][i]),
        'w_k': _pad_heads(p['w_kv_b'][i], NOPE + VDIM, slice(0, NOPE)),
        'w_v': _pad_heads(p['w_kv_b'][i], NOPE + VDIM, slice(NOPE, NOPE + VDIM)).T,
        'v_one': v_one,
        'conv_dw_w': p['conv_dw_w'][i], 'conv_dw_b': row(p['conv_dw_b'][i]),
        'conv_ln_g': row(p['conv_ln_g'][i]), 'conv_ln_b': row(p['conv_ln_b'][i]),
        'gn_att': row(gn[:n_att]), 'gn_conv': row(gn[n_att:n_att + CONV_CH]),
        'gn_hy': row(gn[n_att + CONV_CH:]),
        'hy_short_w': p['hy_short_w'][i], 'hy_short_b': row(p['hy_short_b'][i]),
        'w_out': p['w_out'][i].astype(BF16), 'norm2_g': row(p['norm2_g'][i]),
        'router_w': jnp.concatenate([rw_hi, rw_lo], axis=1),
        'w_gate': p['w_gate'][i].astype(BF16), 'w_up': p['w_up'][i].astype(BF16),
        'w_down': p['w_down'][i].astype(BF16),
    }


def _tile(n, pref):
    return pref if n % pref == 0 else n


def _dot3(a, b):
    a_hi, a_lo = _split(a)
    b_hi, b_lo = _split(b)
    return _dot(a_hi, b_hi) + _dot(a_lo, b_hi) + _dot(a_hi, b_lo)


class _FftPlan(NamedTuple):
    n2: int
    nh: int
    k1p: int
    f1: np.ndarray
    g1: np.ndarray
    mf: np.ndarray
    mi: np.ndarray


@functools.lru_cache(maxsize=None)
def _fft_plan(n_tok):
    n = 2 * n_tok
    n2 = 64 if n_tok >= 2048 else 16
    n1 = n // n2
    nh = n1 // 2
    k1 = nh + 1
    k1p = -(-k1 // SUBLANE) * SUBLANE
    two_pi = 2.0 * np.pi
    r = np.arange(k1)
    ang1 = two_pi * ((np.arange(nh)[None, :] * r[:, None]) % n1) / n1
    f1 = np.zeros((2 * k1p, nh))
    f1[:k1], f1[k1p:k1p + k1] = np.cos(ang1), -np.sin(ang1)
    w = np.where((r == 0) | (r == nh), 1.0, 2.0)[None, :] / n
    g1 = np.zeros((nh, 2 * k1p))
    g1[:, :k1], g1[:, k1p:k1p + k1] = np.cos(ang1.T) * w, -np.sin(ang1.T) * w
    k = r[:, None, None] + n1 * np.arange(n2)[None, :, None]
    th = two_pi * ((k * np.arange(n2)[None, None, :]) % n) / n
    tc, ts = np.cos(th), -np.sin(th)
    mf = np.zeros((k1p, 2 * n2, 2 * n2))
    mi = np.zeros((k1p, 2 * n2, 2 * n2))
    mf[:k1, :n2, :n2], mf[:k1, :n2, n2:], mf[:k1, n2:, :n2], mf[:k1, n2:, n2:] = tc, -ts, ts, tc
    tct, tst = tc.transpose(0, 2, 1), ts.transpose(0, 2, 1)
    mi[:k1, :n2, :n2], mi[:k1, :n2, n2:], mi[:k1, n2:, :n2], mi[:k1, n2:, n2:] = tct, tst, -tst, tct
    return _FftPlan(n2, nh, k1p, f1, g1, mf, mi)


def _lanes(refs, index):
    return jnp.concatenate([r[index] for r in refs], axis=1)


def _put_lanes(refs, index, val):
    for t, r in enumerate(refs):
        r[index] = val[:, t * LANE:(t + 1) * LANE]


def _stage1(x_refs, lead, f_ref, a_refs, nh, n2, rows, mm):
    def body(j, carry):
        x = _lanes(x_refs, (lead, pl.ds(j, nh, stride=n2), slice(None)))
        _put_lanes(a_refs, (pl.ds(pl.multiple_of(j * rows, SUBLANE), rows), slice(None)),
                   mm(f_ref[...], x))
        return carry
    lax.fori_loop(0, n2, body, 0, unroll=8)


def _spectrum_rows(a_refs, k, k1p, n2, rows):
    return jnp.concatenate([_lanes(a_refs, (pl.ds(k, n2, stride=rows), slice(None))),
                            _lanes(a_refs, (pl.ds(k1p + k, n2, stride=rows), slice(None)))], axis=0)


def _hyena_conv_kernel(*refs, nt, nh, n2, k1p):
    u_refs, gate_refs = refs[:nt], refs[nt:2 * nt]
    skip_ref, f1_ref, g1_ref, mf_ref, mi_ref, g_ref = refs[2 * nt:2 * nt + 6]
    y_outs, a_refs = refs[2 * nt + 6:3 * nt + 6], refs[3 * nt + 6:]
    rows = 2 * k1p
    bdot = lambda f, x: _dot(f, x.astype(BF16))
    _stage1(u_refs, 0, f1_ref, a_refs, nh, n2, rows, bdot)

    def stage2(k, carry):
        y = _dot(mf_ref[k], _spectrum_rows(a_refs, k, k1p, n2, rows).astype(BF16))
        yr, yi = y[:n2], y[n2:]
        gr, gi = g_ref[0, 0, k], g_ref[0, 1, k]
        z = jnp.concatenate([yr * gr - yi * gi, yr * gi + yi * gr], axis=0).astype(BF16)
        v = _dot(mi_ref[k], z)
        _put_lanes(a_refs, (pl.ds(k, n2, stride=rows), slice(None)), v[:n2])
        _put_lanes(a_refs, (pl.ds(k1p + k, n2, stride=rows), slice(None)), v[n2:])
        return carry
    lax.fori_loop(0, k1p, stage2, 0, unroll=8)

    def inverse1(j, carry):
        v = _lanes(a_refs, (pl.ds(pl.multiple_of(j * rows, SUBLANE), rows), slice(None))).astype(BF16)
        _put_lanes(y_outs, (0, pl.ds(j, nh, stride=n2), slice(None)), _dot(g1_ref[...], v))
        return carry
    lax.fori_loop(0, n2, inverse1, 0, unroll=8)

    n_tok = y_outs[0].shape[1]
    step = math.gcd(n_tok, 512)

    def epilogue(i, carry):
        sl = (0, pl.ds(pl.multiple_of(i * step, step), step), slice(None))
        for t in range(nt):
            y_outs[t][sl] = gate_refs[t][sl] * (
                y_outs[t][sl] + u_refs[t][sl] * skip_ref[:, t * LANE:(t + 1) * LANE])
        return carry
    lax.fori_loop(0, n_tok // step, epilogue, 0)


def _lane_tiles(x):
    if isinstance(x, (list, tuple)):
        return [(a, 0) for a in x]
    return [(x, t) for t in range(x.shape[2] // LANE)]


def _hyena_conv(u, gate, skip, g, order, plan, mats):
    ut, gt = _lane_tiles(u), _lane_tiles(gate)
    nt = len(ut)
    bsz, n_tok = ut[0][0].shape[:2]
    ch = nt * LANE
    nh, n2, k1p = plan.nh, plan.n2, plan.k1p
    rows = 2 * k1p
    tok = lambda t: pl.BlockSpec((1, n_tok, LANE), lambda b: (b, 0, t))
    const = lambda a: pl.BlockSpec(a.shape, lambda b: (0,) * a.ndim, pipeline_mode=pl.Buffered(1))
    return pl.pallas_call(
        functools.partial(_hyena_conv_kernel, nt=nt, nh=nh, n2=n2, k1p=k1p),
        grid=(bsz,),
        in_specs=[tok(t) for _, t in ut] + [tok(t) for _, t in gt]
        + [pl.BlockSpec((None, 1, ch), lambda b: (order, 0, 0)),
           const(mats['f1']), const(mats['g1']), const(mats['mf']), const(mats['mi']),
           pl.BlockSpec((1, 2, k1p, n2, ch), lambda b: (order, 0, 0, 0, 0),
                        pipeline_mode=pl.Buffered(1))],
        out_specs=[tok(0)] * nt,
        out_shape=[jax.ShapeDtypeStruct((bsz, n_tok, LANE), F32)] * nt,
        scratch_shapes=[pltpu.VMEM((n2 * rows, LANE), F32)] * nt,
        compiler_params=_cparams(("parallel",)),
        name="hyena_conv",
    )(*[a for a, _ in ut], *[a for a, _ in gt], skip.reshape(skip.shape[0], 1, ch),
      mats['f1'], mats['g1'], mats['mf'], mats['mi'], g)


def _filter_spectrum_kernel(h_ref, ss_ref, f1_ref, mf_ref, g_out, a_ref, *, nh, n2, k1p):
    rows = 2 * k1p
    s = lax.rsqrt(ss_ref[0] + ss_ref[1] + EPS)
    for direction in range(2):
        _stage1([h_ref], direction, f1_ref, [a_ref], nh, n2, rows, _dot3)

        def stage2(k, carry):
            y = _dot3(mf_ref[k], _spectrum_rows([a_ref], k, k1p, n2, rows))
            if direction == 0:
                g_out[0, 0, k] = y[:n2] * s
                g_out[0, 1, k] = y[n2:] * s
            else:
                g_out[0, 0, k] = g_out[0, 0, k] + y[:n2] * s
                g_out[0, 1, k] = g_out[0, 1, k] - y[n2:] * s
            return carry
        lax.fori_loop(0, k1p, stage2, 0)


def _filter_spectrum(h, ss, plan, mats):
    groups, n_tok, ch = h.shape
    nh, n2, k1p = plan.nh, plan.n2, plan.k1p
    full = lambda a: pl.BlockSpec(a.shape, lambda o, ci: (0,) * a.ndim)
    return pl.pallas_call(
        functools.partial(_filter_spectrum_kernel, nh=nh, n2=n2, k1p=k1p),
        grid=(groups // 2, ch // LANE),
        in_specs=[pl.BlockSpec((2, n_tok, LANE), lambda o, ci: (o, 0, ci)),
                  pl.BlockSpec((2, 1, LANE), lambda o, ci: (o, 0, ci)),
                  full(mats['f1_32']), full(mats['mf32'])],
        out_specs=pl.BlockSpec((1, 2, k1p, n2, LANE), lambda o, ci: (o, 0, 0, 0, ci)),
        out_shape=jax.ShapeDtypeStruct((groups // 2, 2, k1p, n2, ch), F32),
        scratch_shapes=[pltpu.VMEM((n2 * 2 * k1p, LANE), F32)],
        compiler_params=_cparams(("arbitrary", "arbitrary")),
        name="hyena_filter_spectrum",
    )(h, ss, mats['f1_32'], mats['mf32'])


def _filt_kernel(z_ref, w1_ref, b1_ref, w2_ref, b2_ref, fr_ref, w3_ref, dec_ref, h_out, ss_out, *, tl):
    i = pl.program_id(0)
    z = z_ref[...]
    h = jnp.sin(fr_ref[0:1, :] * (_dot3(z, w1_ref[...]) + b1_ref[...]))
    h = jnp.sin(fr_ref[1:2, :] * (_dot3(h, w2_ref[...]) + b2_ref[...]))
    h = _dot3(h, w3_ref[...])
    decay = jnp.exp(-z[:, 0:1] * jnp.abs(dec_ref[...]))
    row = i * tl + lax.broadcasted_iota(jnp.int32, (tl, 1), 0)

    @pl.when(i == 0)
    def _():
        ss_out[...] = jnp.zeros_like(ss_out)

    for g in range(2 * HY_ORDER):
        hg = h[:, g * HY_CH:(g + 1) * HY_CH] * decay
        if g % 2 == 1:
            hg = jnp.where(row > 0, hg, 0.0)
        h_out[g] = hg
        ss_out[g] += jnp.sum(hg * hg, axis=0, keepdims=True)


def _hyena_filter_spectrum(n_tok, i, p, plan, mats):
    t = jnp.linspace(0.0, 1.0, n_tok, dtype=F32)[:, None]
    w = 2.0 * math.pi * jnp.arange(n_tok, dtype=F32) / n_tok
    f = jnp.linspace(1e-4, HY_BANDS - 1, HY_BANDS, dtype=F32)
    fw = w[:, None] * f[None, :]
    feat = jnp.concatenate([t, jnp.cos(fw), -jnp.sin(fw)], axis=-1)
    n_feat = feat.shape[1]
    hid = p['hy_w1'].shape[2]
    feat = jnp.pad(feat, ((0, 0), (0, hid - n_feat)))
    w1 = jnp.pad(p['hy_w1'][i], ((0, hid - n_feat), (0, 0)))
    tl = _tile(n_tok, 512)
    groups = 2 * HY_ORDER
    full = lambda a: pl.BlockSpec(a.shape, lambda s: (0,) * a.ndim)
    wts = [w1, p['hy_b1'][i][None], p['hy_w2'][i], p['hy_b2'][i][None], p['hy_sin_freq'][i],
           p['hy_w3'][i], p['hy_decay'][i][None]]
    h, ss = pl.pallas_call(
        functools.partial(_filt_kernel, tl=tl),
        grid=(n_tok // tl,),
        in_specs=[pl.BlockSpec((tl, hid), lambda s: (s, 0))] + [full(a) for a in wts],
        out_specs=[pl.BlockSpec((groups, tl, HY_CH), lambda s: (0, s, 0)),
                   pl.BlockSpec((groups, 1, HY_CH), lambda s: (0, 0, 0))],
        out_shape=[jax.ShapeDtypeStruct((groups, n_tok, HY_CH), F32),
                   jax.ShapeDtypeStruct((groups, 1, HY_CH), F32)],
        compiler_params=_cparams(("arbitrary",)),
        name="hyena_filter_mlp",
    )(feat, *wts)
    return _filter_spectrum(h, ss, plan, mats)


def _hyena(v, x1, x2, skip, plan, mats, g):
    y = v
    for o, gate in enumerate((x1, x2)):
        y = _hyena_conv(y, gate, skip, g, o, plan, mats)
    return y


def _fft_mats(plan):
    return {'f1': jnp.asarray(plan.f1, BF16), 'f1_32': jnp.asarray(plan.f1, F32),
            'g1': jnp.asarray(plan.g1, BF16), 'mf': jnp.asarray(plan.mf, BF16),
            'mf32': jnp.asarray(plan.mf, F32), 'mi': jnp.asarray(plan.mi, BF16)}


def _cumsum_lanes(x):
    lane = lax.broadcasted_iota(jnp.int32, x.shape, 1)
    s = 1
    while s < x.shape[1]:
        x = x + jnp.where(lane >= s, pltpu.roll(x, s, 1), 0.0)
        s *= 2
    return x


def _topk_kernel(aff_ref, idx_out, gate_out, pos_ref, *, cap, chunk, nb):
    a = aff_ref[0]
    n_e, n_tok = a.shape
    bits = lax.bitcast_convert_type(a, jnp.int32)

    def count(mask):
        return jnp.sum(jnp.where(mask, 1.0, 0.0), axis=1, keepdims=True)

    def bisect(_, lohi):
        lo, hi = lohi
        mid = lo + ((hi - lo) >> 1)
        ok = count(bits >= mid) >= cap
        return jnp.where(ok, mid, lo), jnp.where(ok, hi, mid)

    lo0 = jnp.zeros((n_e, 1), jnp.int32)
    hi0 = jnp.full((n_e, 1), 0x7F800000, jnp.int32)
    thr, _ = lax.fori_loop(0, 31, bisect, (lo0, hi0))
    gt = bits > thr
    eqf = jnp.where(bits == thr, 1.0, 0.0)
    need = cap - count(gt)
    rank = _cumsum_lanes(eqf) - eqf
    self_ = jnp.where(gt, 1.0, jnp.where(rank < need, eqf, 0.0))
    pos_ref[...] = _cumsum_lanes(self_) * self_

    na = cap // nb
    tok = lax.broadcasted_iota(jnp.int32, (1, n_tok), 1)
    t_hi = (tok >> 6).astype(F32)
    t_lo = (tok & 63).astype(F32)
    row_a = lax.broadcasted_iota(jnp.int32, (na, 1), 0)
    row_b = lax.broadcasted_iota(jnp.int32, (nb, 1), 0)
    nt = (((1,), (1,)), ((), ()))

    def per_expert(e, carry):
        aff = aff_ref[0, pl.ds(e, 1), :]
        a_hi = aff.astype(BF16).astype(F32)
        a_mid = (aff - a_hi).astype(BF16).astype(F32)
        a_lo = aff - a_hi - a_mid
        slot = pos_ref[pl.ds(e, 1), :].astype(jnp.int32) - 1
        hi, lo = slot >> (nb.bit_length() - 1), slot & (nb - 1)
        acc = jnp.zeros((na, 5 * nb), F32)
        for c in range(n_tok // chunk):
            sl = slice(c * chunk, (c + 1) * chunk)
            one_a = jnp.where(hi[:, sl] == row_a, 1.0, 0.0).astype(BF16)
            in_b = lo[:, sl] == row_b
            pay = jnp.concatenate([jnp.where(in_b, v[:, sl], 0.0) for v in (t_hi, t_lo, a_hi, a_mid, a_lo)],
                                  axis=0).astype(BF16)
            acc = acc + lax.dot_general(one_a, pay, nt, preferred_element_type=F32)
        rows = pl.ds(pl.multiple_of(e * na, na), na) if isinstance(e, jax.Array) else slice(e * na, (e + 1) * na)
        idx_out[0, rows, :] = (acc[:, :nb] * 64.0 + acc[:, nb:2 * nb]).astype(jnp.int32)
        gate_out[0, rows, :] = acc[:, 2 * nb:3 * nb] + acc[:, 3 * nb:4 * nb] + acc[:, 4 * nb:]
        return carry

    if na % SUBLANE == 0:
        lax.fori_loop(0, n_e, per_expert, 0)
    else:
        for e in range(n_e):
            per_expert(e, 0)


def _route(aff_t, cap):
    bsz, n_e, n_tok = aff_t.shape
    nb = 16
    out = pl.BlockSpec((1, n_e * cap // nb, nb), lambda b: (b, 0, 0))
    idx, gates = pl.pallas_call(
        functools.partial(_topk_kernel, cap=cap, chunk=_tile(n_tok, 1024), nb=nb),
        grid=(bsz,), in_specs=[pl.BlockSpec((1, n_e, n_tok), lambda b: (b, 0, 0))],
        out_specs=[out, out],
        out_shape=[jax.ShapeDtypeStruct((bsz, n_e * cap // nb, nb), jnp.int32),
                   jax.ShapeDtypeStruct((bsz, n_e * cap // nb, nb), F32)],
        scratch_shapes=[pltpu.VMEM((n_e, n_tok), F32)],
        compiler_params=_cparams(("parallel",)),
        name="topk_route",
    )(aff_t)
    return idx.reshape(bsz, n_e, cap), gates.reshape(bsz, n_e, cap)


def kernel(x, c, ctx, c_ctx, mod_w, mod_b, norm1_g, w_in, q_a_g, w_q_b, kv_a_g, w_kv_b, conv_dw_w, conv_dw_b, conv_ln_g, conv_ln_b, hy_short_w, hy_short_b, hy_w1, hy_b1, hy_w2, hy_b2, hy_w3, hy_sin_freq, hy_decay, hy_skip, group_norm_g, w_out, norm2_g, router_w, w_gate, w_up, w_down, final_norm_g):
    p = dict(mod_w=mod_w, mod_b=mod_b, norm1_g=norm1_g, w_in=w_in, q_a_g=q_a_g, w_q_b=w_q_b,
             kv_a_g=kv_a_g, w_kv_b=w_kv_b, conv_dw_w=conv_dw_w, conv_dw_b=conv_dw_b,
             conv_ln_g=conv_ln_g, conv_ln_b=conv_ln_b, hy_short_w=hy_short_w,
             hy_short_b=hy_short_b, hy_w1=hy_w1, hy_b1=hy_b1, hy_w2=hy_w2, hy_b2=hy_b2,
             hy_w3=hy_w3, hy_sin_freq=hy_sin_freq, hy_decay=hy_decay, hy_skip=hy_skip,
             group_norm_g=group_norm_g, w_out=w_out, norm2_g=norm2_g, router_w=router_w,
             w_gate=w_gate, w_up=w_up, w_down=w_down)
    depth = mod_w.shape[0]
    bsz, n_lat, d = x.shape
    n_ctx = ctx.shape[1]

    rows = -(-(bsz + 1) // 8) * 8
    cc = jnp.concatenate([c, c_ctx[None, :], jnp.zeros((rows - bsz - 1, d), F32)], axis=0)
    mod_all = _modulation(cc, mod_w, mod_b)
    lat_row = lambda b: b
    ctx_row = lambda b: bsz

    tabs_l = _rope_tables(n_lat)
    tabs_c = _identity_tables(n_ctx)
    tm_l, tm_c = _tile(n_lat, 512), _tile(n_ctx, 256)
    tq_l, tq_c = _tile(n_lat, 256), _tile(n_ctx, 256)
    tl_l, tl_c = _tile(n_lat, 512), _tile(n_ctx, 256)

    xl, xc = x, ctx
    prev_l = prev_c = None
    for i in range(depth):
        last = i == depth - 1
        lw = _layer_weights(i, p)
        mod = mod_all[i].reshape(rows, 1, N_MOD * d)

        def side(xs, prev, row_of_b, tabs, tm, tq, tl, extra_keys, need_mix):
            xs, q, k, v, uc, uh = _inproj(xs, prev, mod, row_of_b, lw, tabs, tm)
            if not need_mix:
                return xs, (k, v), None
            att = _attention(q, [(k, v)] + extra_keys, tq)
            cn, hv, hx1, hx2 = _conv(uc, uh, lw, tl)
            plan = _fft_plan(xs.shape[1])
            mats = _fft_mats(plan)
            filt = _hyena_filter_spectrum(xs.shape[1], i, p, plan, mats)
            hy = _hyena(hv, hx1, hx2, p['hy_skip'][i], plan, mats, filt)
            x1, h2p, aff_t = _outproj(att, cn, hy, xs, mod, row_of_b, lw, tm)
            cap = CAPACITY_FACTOR * xs.shape[1] // N_EXPERTS
            idx, gates = _route(aff_t, cap)
            moe = _moe(h2p, idx, gates, lw)
            return x1, (k, v), moe

        xc, kv_c, moe_c = side(xc, prev_c, ctx_row, tabs_c, tm_c, tq_c, tl_c, [], not last)
        xl, _, moe_l = side(xl, prev_l, lat_row, tabs_l, tm_l, tq_l, tl_l, [kv_c], True)
        prev_l = (moe_l, mod)
        prev_c = None if moe_c is None else (moe_c, mod)

    mod = mod_all[depth - 1].reshape(rows, 1, N_MOD * d)
    return _final(xl, prev_l[0], mod, final_norm_g.reshape(1, d), tm_l)
```

```python
import functools
import math
from typing import NamedTuple

import numpy as np
import jax
import jax.numpy as jnp
from jax import lax
from jax.experimental import pallas as pl
from jax.experimental.pallas import tpu as pltpu

F32 = jnp.float32
BF16 = jnp.bfloat16
EPS = 1e-6

GRID_W = 64
N_MOD = 6
HEADS = 8
NOPE = 64
ROPE = 32
VDIM = 64
Q_LORA = 256
KV_LORA = 128
ROPE_BASE = 10000.0
CONV_CH = 256
CONV_K = 31
HY_CH = 256
HY_ORDER = 2
HY_BANDS = 16
N_EXPERTS = 16
CAPACITY_FACTOR = 2

LANE = 128
SUBLANE = 8
MXU_DIM = 256
HEAD_PAD = LANE
BF16_ROWS = 16
V_ROWS = -(-(VDIM + 1) // BF16_ROWS) * BF16_ROWS
HALO = 16
VMEM_LIMIT = 56 * 1024 * 1024
MOE_ACC_BUDGET = 16 * 1024 * 1024

C_Q = 0
C_KVN = C_Q + Q_LORA
C_KR = C_KVN + KV_LORA
C_CONV = C_KR + LANE
C_HY = C_CONV + 2 * CONV_CH
C_END = C_HY + 3 * HY_CH


def _cparams(sem):
    return pltpu.CompilerParams(dimension_semantics=sem, vmem_limit_bytes=VMEM_LIMIT)


def _rms(x, g):
    return x * lax.rsqrt(jnp.mean(x * x, axis=-1, keepdims=True) + EPS) * g


def _split(a):
    hi = a.astype(BF16)
    lo = (a - hi.astype(F32)).astype(BF16)
    return hi, lo


def _dot(a, b):
    return jnp.dot(a, b, preferred_element_type=F32)


def _from_token_major(ref, n_rows):
    fc = ref.shape[1] // n_rows
    return jnp.concatenate([ref[0, pl.ds(c, n_rows, stride=fc), :] for c in range(fc)], axis=1)


def _mod_kernel(c_ref, w_ref, b_ref, o_ref):
    c = c_ref[...]
    a = c * jax.nn.sigmoid(c)
    a_hi, a_lo = _split(a)
    w_hi, w_lo = _split(w_ref[0])
    o_ref[0] = _dot(a_hi, w_hi) + _dot(a_lo, w_hi) + _dot(a_hi, w_lo) + b_ref[0]


def _modulation(cc, mod_w, mod_b):
    depth, d, n = mod_w.shape
    rows = cc.shape[0]
    tn = 1536
    return pl.pallas_call(
        _mod_kernel,
        grid=(depth, n // tn),
        in_specs=[
            pl.BlockSpec((rows, d), lambda l, j: (0, 0)),
            pl.BlockSpec((1, d, tn), lambda l, j: (l, 0, j)),
            pl.BlockSpec((1, 1, tn), lambda l, j: (l, 0, j)),
        ],
        out_specs=pl.BlockSpec((1, rows, tn), lambda l, j: (l, 0, j)),
        out_shape=jax.ShapeDtypeStruct((depth, rows, n), F32),
        compiler_params=_cparams(("arbitrary", "arbitrary")),
        name="modulation",
    )(cc, mod_w, mod_b.reshape(depth, 1, n))


def _rope(x, c, s1, s2):
    return x * c + pltpu.roll(x, LANE - 8, 1) * s1 + pltpu.roll(x, 8, 1) * s2


def _inproj_kernel(*refs, fuse_prev, scale):
    if fuse_prev:
        xa_ref, xb_ref, g2_ref = refs[:3]
        refs = refs[3:]
    else:
        xa_ref = refs[0]
        refs = refs[1:]
    (sh_ref, sc_ref, n1g_ref, win_ref, qag_ref, wq_ref, kvg_ref, wk_ref, wv_ref,
     c_ref, s1_ref, s2_ref, vone_ref) = refs[:13]
    outs = refs[13:]
    if fuse_prev:
        x_out, q_out, k_out, v_out, uc_out, uh_out = outs
        x = xa_ref[0] + g2_ref[...] * _from_token_major(xb_ref, xa_ref.shape[1])
        x_out[0] = x
    else:
        q_out, k_out, v_out, uc_out, uh_out = outs
        x = xa_ref[0]
    hn = _rms(x, n1g_ref[...]) * (1.0 + sc_ref[...]) + sh_ref[...]
    u = _dot(hn.astype(BF16), win_ref[...])
    c, s1, s2 = c_ref[...], s1_ref[...], s2_ref[...]

    nq = _rms(u[:, C_Q:C_KVN], qag_ref[...]).astype(BF16)
    q = _dot(nq, wq_ref[...])
    for h in range(HEADS):
        sl = slice(h * HEAD_PAD, (h + 1) * HEAD_PAD)
        q_out[0, :, sl] = (_rope(q[:, sl], c, s1, s2) * scale).astype(BF16)

    kvn = _rms(u[:, C_KVN:C_KR], kvg_ref[...]).astype(BF16)
    kr = _rope(u[:, C_KR:C_CONV], c, s1, s2)
    k = _dot(kvn, wk_ref[...])
    for h in range(HEADS):
        sl = slice(h * HEAD_PAD, (h + 1) * HEAD_PAD)
        k_out[0, :, sl] = (k[:, sl] + kr).astype(BF16)
    vt = lax.dot_general(wv_ref[...], kvn, (((1,), (1,)), ((), ())), preferred_element_type=F32)
    v_out[0] = (vt + vone_ref[...]).astype(BF16)
    uc_out[0] = u[:, C_CONV:C_HY].astype(BF16)
    uh_out[0] = u[:, C_HY:C_END].astype(BF16)


def _inproj(x, prev, mod, row_of_b, lw, tabs, tm):
    bsz, n_tok, d = x.shape
    fuse_prev = prev is not None
    grid = (bsz, n_tok // tm)
    tok = lambda w: pl.BlockSpec((1, tm, w), lambda b, i: (b, i, 0))
    modspec = lambda k: pl.BlockSpec((None, 1, d), lambda b, i: (row_of_b(b), 0, k))
    full = lambda a: pl.BlockSpec(a.shape, lambda b, i: (0,) * a.ndim)
    tabspec = pl.BlockSpec((tm, LANE), lambda b, i: (i, 0))

    args, specs = [x], [tok(d)]
    if fuse_prev:
        args += [prev[0], prev[1]]
        specs += [pl.BlockSpec((1, tm * (d // LANE), LANE), lambda b, i: (b, i, 0)), modspec(5)]
    args += [mod, mod, lw['norm1_g'], lw['w_in'], lw['q_a_g'], lw['w_q'], lw['kv_a_g'],
             lw['w_k'], lw['w_v'], tabs[0], tabs[1], tabs[2], lw['v_one']]
    specs += [modspec(0), modspec(1), full(lw['norm1_g']), full(lw['w_in']), full(lw['q_a_g']),
              full(lw['w_q']), full(lw['kv_a_g']), full(lw['w_k']), full(lw['w_v']),
              tabspec, tabspec, tabspec, full(lw['v_one'])]
    hp = HEADS * HEAD_PAD
    out_shape, out_specs = [], []
    if fuse_prev:
        out_shape.append(jax.ShapeDtypeStruct((bsz, n_tok, d), F32))
        out_specs.append(tok(d))
    out_shape += [jax.ShapeDtypeStruct((bsz, n_tok, hp), BF16)] * 2
    out_specs += [tok(hp)] * 2
    out_shape += [jax.ShapeDtypeStruct((bsz, hp, n_tok), BF16),
                  jax.ShapeDtypeStruct((bsz, n_tok, 2 * CONV_CH), BF16),
                  jax.ShapeDtypeStruct((bsz, n_tok, 3 * HY_CH), BF16)]
    out_specs += [pl.BlockSpec((1, hp, tm), lambda b, i: (b, 0, i)),
                  tok(2 * CONV_CH), tok(3 * HY_CH)]
    scale = float((NOPE + ROPE) ** -0.5 * math.log2(math.e))
    res = pl.pallas_call(
        functools.partial(_inproj_kernel, fuse_prev=fuse_prev, scale=scale),
        grid=grid, in_specs=specs, out_specs=out_specs, out_shape=out_shape,
        compiler_params=_cparams(("parallel", "parallel")),
        name="inproj",
    )(*args)
    if not fuse_prev:
        res = [x] + list(res)
    return res


def _attn_kernel(*refs, n_sets, hps, tk, ahead):
    q_ref = refs[0]
    kv = refs[1:1 + 2 * n_sets]
    o_ref = refs[1 + 2 * n_sets]
    nt = (((1,), (1,)), ((), ()))
    sls = [slice(hh * HEAD_PAD, (hh + 1) * HEAD_PAD) for hh in range(hps)]
    qs = [q_ref[0, :, sl] for sl in sls]
    tks = [min(tk, kv[2 * i].shape[1]) for i in range(n_sets)]
    items = [(hh, i, c) for hh in range(hps) for i in range(n_sets)
             for c in range(kv[2 * i].shape[1] // tks[i])]

    def scores(item):
        hh, i, c = item
        return lax.dot_general(kv[2 * i][0, c * tks[i]:(c + 1) * tks[i], sls[hh]], qs[hh], nt,
                               preferred_element_type=F32)

    m, o = [None] * hps, [None] * hps
    pending = [scores(it) for it in items[:ahead]]
    for n, (hh, i, c) in enumerate(items):
        s = pending.pop(0)
        if n + ahead < len(items):
            pending.append(scores(items[n + ahead]))
        mt = jnp.max(s, axis=0, keepdims=True)
        m_new = mt if m[hh] is None else jnp.maximum(m[hh], mt)
        pv = _dot(kv[2 * i + 1][0, hh * HEAD_PAD:hh * HEAD_PAD + V_ROWS, c * tks[i]:(c + 1) * tks[i]],
                  jnp.exp2(s - m_new).astype(BF16))
        o[hh] = pv if o[hh] is None else o[hh] * jnp.exp2(m[hh] - m_new) + pv
        m[hh] = m_new
    pad = jnp.zeros((HEAD_PAD - V_ROWS, o[0].shape[1]), F32)
    outs = [jnp.concatenate([oh * (1.0 / oh[VDIM:VDIM + 1, :]), pad], axis=0).T for oh in o]
    lane = lax.broadcasted_iota(jnp.int32, outs[0].shape, 1)
    for pr in range(hps // 2):
        o_ref[0, :, pr * LANE:(pr + 1) * LANE] = jnp.where(
            lane < VDIM, outs[2 * pr], pltpu.roll(outs[2 * pr + 1], VDIM, 1))


def _attention(q, key_sets, tq):
    bsz, n_q, hp = q.shape
    hps = 8
    grid = (bsz, HEADS // hps, n_q // tq)
    args = [q]
    specs = [pl.BlockSpec((1, tq, hps * HEAD_PAD), lambda b, h, i: (b, i, h))]
    for k, v in key_sets:
        n_k = k.shape[1]
        args += [k, v]
        specs += [pl.BlockSpec((1, n_k, hps * HEAD_PAD), lambda b, h, i: (b, 0, h)),
                  pl.BlockSpec((1, hps * HEAD_PAD, n_k), lambda b, h, i: (b, h, 0))]
    return pl.pallas_call(
        functools.partial(_attn_kernel, n_sets=len(key_sets), hps=hps,
                          tk=MXU_DIM, ahead=5),
        grid=grid, in_specs=specs,
        out_specs=pl.BlockSpec((1, tq, hps * VDIM), lambda b, h, i: (b, i, h)),
        out_shape=jax.ShapeDtypeStruct((bsz, n_q, HEADS * VDIM), F32),
        compiler_params=_cparams(("parallel", "parallel", "arbitrary")),
        name="attention",
    )(*args)


def _conv_kernel(ucp_ref, uc_ref, ucn_ref, uhp_ref, uh_ref, uhn_ref,
                 cw_ref, cb_ref, lg_ref, lb_ref, gn_ref, hw_ref, hb_ref,
                 cn_out, v_out, x1_out, x2_out, ypad, hpad, yph, *, tl, row_tile):
    i = pl.program_id(1)
    has_prev = (i > 0).astype(F32)
    has_next = (i < pl.num_programs(1) - 1).astype(F32)

    def glu(u):
        return u[:, :CONV_CH] * jax.nn.sigmoid(u[:, CONV_CH:])

    f32 = lambda ref: ref[0].astype(F32)
    ypad[0:HALO, :] = glu(f32(ucp_ref)) * has_prev
    ypad[HALO:HALO + tl, :] = glu(f32(uc_ref))
    ypad[HALO + tl:, :] = glu(f32(ucn_ref)) * has_next
    hpad[0:HALO, :] = f32(uhp_ref) * has_prev
    hpad[HALO:HALO + tl, :] = f32(uh_ref)
    hpad[HALO + tl:, :] = f32(uhn_ref) * has_next

    span = tl + 2 * HALO - SUBLANE
    for ph in range(1, SUBLANE):
        yph[ph] = ypad[ph:ph + span, :]

    half = CONV_K // 2
    for r in range(tl // row_tile):
        base = r * row_tile
        acc = jnp.zeros((row_tile, CONV_CH), F32) + cb_ref[...]
        for k in range(CONV_K):
            off = HALO + base + k - half
            ph = off % SUBLANE
            src = ypad[off:off + row_tile, :] if ph == 0 else yph[ph, off - ph:off - ph + row_tile, :]
            acc = acc + cw_ref[k:k + 1, :] * src
        mu = jnp.mean(acc, axis=-1, keepdims=True)
        cen = acc - mu
        var = jnp.mean(cen * cen, axis=-1, keepdims=True)
        y = cen * lax.rsqrt(var + EPS) * lg_ref[...] + lb_ref[...]
        y = y * jax.nn.sigmoid(y)
        cn_out[0, base:base + row_tile, :] = _rms(y, gn_ref[...])

        z = jnp.zeros((row_tile, 3 * HY_CH), F32) + hb_ref[...]
        for k in range(3):
            off = HALO + base + k - 1
            z = z + hw_ref[k:k + 1, :] * hpad[off:off + row_tile, :]
        v_out[0, base:base + row_tile, :] = z[:, :HY_CH]
        x1_out[0, base:base + row_tile, :] = z[:, HY_CH:2 * HY_CH]
        x2_out[0, base:base + row_tile, :] = z[:, 2 * HY_CH:]


def _conv(uc, uh, lw, tl):
    bsz, n_tok, _ = uc.shape
    nh = tl // HALO
    last = n_tok // HALO - 1
    cur = lambda w: pl.BlockSpec((1, tl, w), lambda b, i: (b, i, 0))
    prv = lambda w: pl.BlockSpec((1, HALO, w), lambda b, i: (b, jnp.maximum(i * nh - 1, 0), 0))
    nxt = lambda w: pl.BlockSpec((1, HALO, w), lambda b, i: (b, jnp.minimum((i + 1) * nh, last), 0))
    full = lambda a: pl.BlockSpec(a.shape, lambda b, i: (0,) * a.ndim)
    wts = [lw['conv_dw_w'], lw['conv_dw_b'], lw['conv_ln_g'], lw['conv_ln_b'], lw['gn_conv'],
           lw['hy_short_w'], lw['hy_short_b']]
    return pl.pallas_call(
        functools.partial(_conv_kernel, tl=tl, row_tile=min(64, tl)),
        grid=(bsz, n_tok // tl),
        in_specs=[prv(2 * CONV_CH), cur(2 * CONV_CH), nxt(2 * CONV_CH),
                  prv(3 * HY_CH), cur(3 * HY_CH), nxt(3 * HY_CH)] + [full(w) for w in wts],
        out_specs=[cur(CONV_CH)] + [cur(HY_CH)] * 3,
        out_shape=[jax.ShapeDtypeStruct((bsz, n_tok, CONV_CH), F32)]
        + [jax.ShapeDtypeStruct((bsz, n_tok, HY_CH), F32)] * 3,
        scratch_shapes=[pltpu.VMEM((tl + 2 * HALO, CONV_CH), F32),
                        pltpu.VMEM((tl + 2 * HALO, 3 * HY_CH), F32),
                        pltpu.VMEM((SUBLANE, tl + 2 * HALO - SUBLANE, CONV_CH), F32)],
        compiler_params=_cparams(("parallel", "parallel")),
        name="conv",
    )(uc, uc, uc, uh, uh, uh, *wts)


def _outproj_kernel(att_ref, cn_ref, *refs, n_hy):
    hy_refs = refs[:n_hy]
    (x_ref, g1_ref, sh_ref, sc_ref, gna_ref, gnh_ref, wo_ref, n2g_ref, rw_ref,
     x1_out, h2p_out, aff_out) = refs[n_hy:]
    a = _rms(att_ref[0], gna_ref[...]).astype(BF16)
    c = cn_ref[0].astype(BF16)
    h = _rms(_lanes(hy_refs, 0), gnh_ref[...]).astype(BF16)
    na, nc = a.shape[1], c.shape[1]
    y = (_dot(a, wo_ref[0:na, :]) + _dot(c, wo_ref[na:na + nc, :]) + _dot(h, wo_ref[na + nc:, :]))
    x1 = x_ref[0] + g1_ref[...] * y
    x1_out[0] = x1
    h2 = _rms(x1, n2g_ref[...]) * (1.0 + sc_ref[...]) + sh_ref[...]
    d = h2.shape[1]
    lo = lax.bitcast_convert_type(h2[:, :d // 2].astype(BF16).astype(F32), jnp.uint32)
    hi = lax.bitcast_convert_type(h2[:, d // 2:].astype(BF16).astype(F32), jnp.uint32)
    packed = hi | (lo >> 16)
    n_rows, pc = packed.shape[0], packed.shape[1] // LANE
    for c in range(pc):
        h2p_out[0, pl.ds(c, n_rows, stride=pc), :] = packed[:, c * LANE:(c + 1) * LANE]
    h_hi, h_lo = _split(h2)
    t = _dot(h_hi, rw_ref[...])
    logits = t[:, :LANE] + t[:, LANE:] + _dot(h_lo, rw_ref[:, :LANE])
    lane = lax.broadcasted_iota(jnp.int32, logits.shape, 1)
    logits = jnp.where(lane < N_EXPERTS, logits, -1e30)
    e = jnp.exp(logits - jnp.max(logits, axis=1, keepdims=True))
    aff = e / jnp.sum(e, axis=1, keepdims=True)
    aff_out[0] = aff.T[:N_EXPERTS, :]


def _outproj(att, cn, hy, x, mod, row_of_b, lw, tm):
    bsz, n_tok, d = x.shape
    tok = lambda w: pl.BlockSpec((1, tm, w), lambda b, i: (b, i, 0))
    modspec = lambda k: pl.BlockSpec((None, 1, d), lambda b, i: (row_of_b(b), 0, k))
    full = lambda a: pl.BlockSpec(a.shape, lambda b, i: (0,) * a.ndim)
    wts = [lw['gn_att'], lw['gn_hy'], lw['w_out'], lw['norm2_g'], lw['router_w']]
    pc = d // 2 // LANE
    return pl.pallas_call(
        functools.partial(_outproj_kernel, n_hy=len(hy)),
        grid=(bsz, n_tok // tm),
        in_specs=[tok(att.shape[2]), tok(cn.shape[2])] + [tok(LANE)] * len(hy) + [tok(d),
                  modspec(2), modspec(3), modspec(4)] + [full(w) for w in wts],
        out_specs=[tok(d), pl.BlockSpec((1, tm * pc, LANE), lambda b, i: (b, i, 0)),
                   pl.BlockSpec((1, N_EXPERTS, tm), lambda b, i: (b, 0, i))],
        out_shape=[jax.ShapeDtypeStruct((bsz, n_tok, d), F32),
                   jax.ShapeDtypeStruct((bsz, n_tok * pc, LANE), jnp.uint32),
                   jax.ShapeDtypeStruct((bsz, N_EXPERTS, n_tok), F32)],
        compiler_params=_cparams(("parallel", "parallel")),
        name="outproj",
    )(att, cn, *hy, x, mod, mod, mod, *wts)


def _moe_kernel(idx_ref, idx_nxt_ref, idx_prv_ref, gate_ref, gate_prv_ref, h2p_ref,
                wg_ref, wu_ref, wd_ref, out_hbm, acc_ref, xg0, xg1, y0, y1, sem, *, cap, pc, fc):
    b, e = pl.program_id(0), pl.program_id(1)
    n_e = pl.num_programs(1)
    group = math.gcd(cap, SUBLANE)

    def gather_rows(idx_r, xg, js):
        rows = [h2p_ref[0, pl.ds(pl.multiple_of(idx_r[0, 0, j] * pc, pc), pc), :] for j in js]
        for j, row in zip(js, rows):
            xg[pl.ds(pl.multiple_of(j * pc, pc), pc), :] = row

    def scatter_rows(idx_r, gate_r, y_r, js):
        dsts = [pl.ds(pl.multiple_of(idx_r[0, 0, j] * fc, fc), fc) for j in js]
        new = [acc_ref[dst, :] + gate_r[0, 0, j] * y_r[pl.ds(pl.multiple_of(j * fc, fc), fc), :]
               for j, dst in zip(js, dsts)]
        for dst, val in zip(dsts, new):
            acc_ref[dst, :] = val

    def in_groups(fn, unrolled):
        if unrolled:
            for jg in range(cap // group):
                fn([jg * group + u for u in range(group)])
        else:
            def body(jg, carry):
                fn([jg * group + u for u in range(group)])
                return carry
            lax.fori_loop(0, cap // group, body, 0)

    @pl.when(e == 0)
    def _():
        acc_ref[...] = jnp.zeros_like(acc_ref)
        y1[...] = jnp.zeros_like(y1)
        in_groups(functools.partial(gather_rows, idx_ref, xg0), False)

    def step(xg_cur, xg_nxt, y_cur, y_prv):
        in_groups(functools.partial(gather_rows, idx_nxt_ref, xg_nxt), True)
        lo, hi = [], []
        for c in range(pc):
            w = xg_cur[pl.ds(c, cap, stride=pc), :]
            lo.append(lax.bitcast_convert_type(w << 16, F32).astype(BF16))
            hi.append(lax.bitcast_convert_type(w & jnp.uint32(0xFFFF0000), F32).astype(BF16))
        x = jnp.concatenate(lo + hi, axis=1)
        a = _dot(x, wg_ref[0])
        u = _dot(x, wu_ref[0])
        hmid = (a * jax.nn.sigmoid(a) * u).astype(BF16)
        y = _dot(hmid, wd_ref[0])
        in_groups(functools.partial(scatter_rows, idx_prv_ref, gate_prv_ref, y_prv), True)
        for c in range(fc):
            y_cur[pl.ds(c, cap, stride=fc), :] = y[:, c * LANE:(c + 1) * LANE]

    @pl.when(e % 2 == 0)
    def _():
        step(xg0, xg1, y0, y1)

    @pl.when(e % 2 == 1)
    def _():
        step(xg1, xg0, y1, y0)

    def finish(y_last):
        in_groups(functools.partial(scatter_rows, idx_ref, gate_ref, y_last), False)
        cp = pltpu.make_async_copy(acc_ref, out_hbm.at[b], sem)
        cp.start()
        cp.wait()

    @pl.when((e == n_e - 1) & (e % 2 == 0))
    def _():
        finish(y0)

    @pl.when((e == n_e - 1) & (e % 2 == 1))
    def _():
        finish(y1)


def _moe(h2p, idx, gates, lw):
    d = lw['w_gate'].shape[1]
    pc, fc = d // 2 // LANE, d // LANE
    bsz, n_tok = h2p.shape[0], h2p.shape[1] // pc
    n_e, cap = idx.shape[1], idx.shape[2]
    ff = lw['w_gate'].shape[2]
    if bsz > 1 and bsz * n_tok * d * 4 <= MOE_ACC_BUDGET:
        offs = (jnp.arange(bsz, dtype=jnp.int32) * n_tok)[:, None, None]
        idx = jnp.swapaxes(idx + offs, 0, 1).reshape(1, n_e, bsz * cap)
        gates = jnp.swapaxes(gates, 0, 1).reshape(1, n_e, bsz * cap)
        out = _moe(h2p.reshape(1, bsz * n_tok * pc, LANE), idx, gates, lw)
        return out.reshape(bsz, n_tok * fc, LANE)
    last = n_e - 1
    smem = lambda f: pl.BlockSpec((1, 1, cap), lambda b, e: (b * n_e + f(e), 0, 0),
                                  memory_space=pltpu.SMEM)
    cur, nxt, prv = (lambda e: e), (lambda e: jnp.minimum(e + 1, last)), (lambda e: jnp.maximum(e - 1, 0))
    idx_r, gates_r = idx.reshape(bsz * n_e, 1, cap), gates.reshape(bsz * n_e, 1, cap)
    return pl.pallas_call(
        functools.partial(_moe_kernel, cap=cap, pc=pc, fc=fc),
        grid=(bsz, n_e),
        in_specs=[smem(cur), smem(nxt), smem(prv), smem(cur), smem(prv),
                  pl.BlockSpec((1, n_tok * pc, LANE), lambda b, e: (b, 0, 0)),
                  pl.BlockSpec((1, d, ff), lambda b, e: (e, 0, 0)),
                  pl.BlockSpec((1, d, ff), lambda b, e: (e, 0, 0)),
                  pl.BlockSpec((1, ff, d), lambda b, e: (e, 0, 0))],
        out_specs=pl.BlockSpec(memory_space=pl.ANY),
        out_shape=jax.ShapeDtypeStruct((bsz, n_tok * fc, LANE), F32),
        scratch_shapes=[pltpu.VMEM((n_tok * fc, LANE), F32)]
        + [pltpu.VMEM((cap * pc, LANE), jnp.uint32)] * 2
        + [pltpu.VMEM((cap * fc, LANE), F32)] * 2
        + [pltpu.SemaphoreType.DMA(())],
        compiler_params=_cparams(("arbitrary", "arbitrary")),
        name="moe",
    )(idx_r, idx_r, idx_r, gates_r, gates_r, h2p, lw['w_gate'], lw['w_up'], lw['w_down'])


def _final_kernel(x_ref, m_ref, g2_ref, fg_ref, o_ref):
    o_ref[0] = _rms(x_ref[0] + g2_ref[...] * _from_token_major(m_ref, x_ref.shape[1]), fg_ref[...])


def _final(x1, moe, mod, fg, tm):
    bsz, n_tok, d = x1.shape
    tok = pl.BlockSpec((1, tm, d), lambda b, i: (b, i, 0))
    return pl.pallas_call(
        _final_kernel,
        grid=(bsz, n_tok // tm),
        in_specs=[tok, pl.BlockSpec((1, tm * (d // LANE), LANE), lambda b, i: (b, i, 0)),
                  pl.BlockSpec((None, 1, d), lambda b, i: (b, 0, 5)),
                  pl.BlockSpec((1, d), lambda b, i: (0, 0))],
        out_specs=tok,
        out_shape=jax.ShapeDtypeStruct((bsz, n_tok, d), F32),
        compiler_params=_cparams(("parallel", "parallel")),
        name="final_norm",
    )(x1, moe, mod, fg)


def _rope_tables(n_tok):
    rows = n_tok // GRID_W
    row = jnp.repeat(jnp.arange(rows, dtype=F32), GRID_W)
    col = jnp.tile(jnp.arange(GRID_W, dtype=F32), rows)
    half = ROPE // 2
    inv = ROPE_BASE ** (-jnp.arange(0, half, 2, dtype=F32) / half)
    cr, sr = jnp.cos(row[:, None] * inv), jnp.sin(row[:, None] * inv)
    cc, sc = jnp.cos(col[:, None] * inv), jnp.sin(col[:, None] * inv)
    z8 = jnp.zeros_like(cr)
    ones = jnp.ones((n_tok, NOPE), F32)
    pad = jnp.zeros((n_tok, HEAD_PAD - NOPE - ROPE), F32)
    c = jnp.concatenate([ones, cr, cr, cc, cc, pad + 1.0], axis=1)
    s1 = jnp.concatenate([ones * 0.0, -sr, z8, -sc, z8, pad], axis=1)
    s2 = jnp.concatenate([ones * 0.0, z8, sr, z8, sc, pad], axis=1)
    return c, s1, s2


def _identity_tables(n_tok):
    return (jnp.ones((n_tok, HEAD_PAD), F32), jnp.zeros((n_tok, HEAD_PAD), F32),
            jnp.zeros((n_tok, HEAD_PAD), F32))


def _pad_heads(w, per_head, take):
    k = w.shape[0]
    w = w.reshape(k, HEADS, per_head)[:, :, take]
    w = jnp.pad(w, ((0, 0), (0, 0), (0, HEAD_PAD - w.shape[2])))
    return w.reshape(k, HEADS * HEAD_PAD).astype(BF16)


def _layer_weights(i, p):
    d = p['w_in'].shape[1]
    w_in = p['w_in'][i]
    off_kv = Q_LORA
    off_conv = off_kv + KV_LORA + ROPE
    off_hy = off_conv + 2 * CONV_CH
    kr = jnp.zeros((d, LANE), F32).at[:, NOPE:NOPE + ROPE].set(w_in[:, off_kv + KV_LORA:off_conv])
    w_in_r = jnp.concatenate([w_in[:, :off_kv], w_in[:, off_kv:off_kv + KV_LORA], kr,
                              w_in[:, off_conv:off_hy], w_in[:, off_hy:]], axis=1).astype(BF16)
    gn = p['group_norm_g'][i]
    n_att = HEADS * VDIM
    rw_hi, rw_lo = _split(jnp.pad(p['router_w'][i], ((0, 0), (0, LANE - N_EXPERTS))))
    v_one = jnp.zeros((HEADS, HEAD_PAD), F32).at[:, VDIM].set(1.0).reshape(HEADS * HEAD_PAD, 1)
    row = lambda a: a.reshape(1, -1)
    return {
        'norm1_g': row(p['norm1_g'][i]), 'w_in': w_in_r,
        'q_a_g': row(p['q_a_g'][i]), 'w_q': _pad_heads(p['w_q_b'][i], NOPE + ROPE, slice(None)),
        'kv_a_g': row(p['kv_a_g'][i]),
        'w_k': _pad_heads(p['w_kv_b'][i], NOPE + VDIM, slice(0, NOPE)),
        'w_v': _pad_heads(p['w_kv_b'][i], NOPE + VDIM, slice(NOPE, NOPE + VDIM)).T,
        'v_one': v_one,
        'conv_dw_w': p['conv_dw_w'][i], 'conv_dw_b': row(p['conv_dw_b'][i]),
        'conv_ln_g': row(p['conv_ln_g'][i]), 'conv_ln_b': row(p['conv_ln_b'][i]),
        'gn_att': row(gn[:n_att]), 'gn_conv': row(gn[n_att:n_att + CONV_CH]),
        'gn_hy': row(gn[n_att + CONV_CH:]),
        'hy_short_w': p['hy_short_w'][i], 'hy_short_b': row(p['hy_short_b'][i]),
        'w_out': p['w_out'][i].astype(BF16), 'norm2_g': row(p['norm2_g'][i]),
        'router_w': jnp.concatenate([rw_hi, rw_lo], axis=1),
        'w_gate': p['w_gate'][i].astype(BF16), 'w_up': p['w_up'][i].astype(BF16),
        'w_down': p['w_down'][i].astype(BF16),
    }


def _tile(n, pref):
    return pref if n % pref == 0 else n


def _dot3(a, b):
    a_hi, a_lo = _split(a)
    b_hi, b_lo = _split(b)
    return _dot(a_hi, b_hi) + _dot(a_lo, b_hi) + _dot(a_hi, b_lo)


class _FftPlan(NamedTuple):
    n2: int
    nh: int
    k1p: int
    f1: np.ndarray
    g1: np.ndarray
    mf: np.ndarray
    mi: np.ndarray


@functools.lru_cache(maxsize=None)
def _fft_plan(n_tok):
    n = 2 * n_tok
    n2 = 64 if n_tok >= 2048 else 16
    n1 = n // n2
    nh = n1 // 2
    k1 = nh + 1
    k1p = -(-k1 // SUBLANE) * SUBLANE
    two_pi = 2.0 * np.pi
    r = np.arange(k1)
    ang1 = two_pi * ((np.arange(nh)[None, :] * r[:, None]) % n1) / n1
    f1 = np.zeros((2 * k1p, nh))
    f1[:k1], f1[k1p:k1p + k1] = np.cos(ang1), -np.sin(ang1)
    w = np.where((r == 0) | (r == nh), 1.0, 2.0)[None, :] / n
    g1 = np.zeros((nh, 2 * k1p))
    g1[:, :k1], g1[:, k1p:k1p + k1] = np.cos(ang1.T) * w, -np.sin(ang1.T) * w
    k = r[:, None, None] + n1 * np.arange(n2)[None, :, None]
    th = two_pi * ((k * np.arange(n2)[None, None, :]) % n) / n
    tc, ts = np.cos(th), -np.sin(th)
    mf = np.zeros((k1p, 2 * n2, 2 * n2))
    mi = np.zeros((k1p, 2 * n2, 2 * n2))
    mf[:k1, :n2, :n2], mf[:k1, :n2, n2:], mf[:k1, n2:, :n2], mf[:k1, n2:, n2:] = tc, -ts, ts, tc
    tct, tst = tc.transpose(0, 2, 1), ts.transpose(0, 2, 1)
    mi[:k1, :n2, :n2], mi[:k1, :n2, n2:], mi[:k1, n2:, :n2], mi[:k1, n2:, n2:] = tct, tst, -tst, tct
    return _FftPlan(n2, nh, k1p, f1, g1, mf, mi)


def _lanes(refs, index):
    return jnp.concatenate([r[index] for r in refs], axis=1)


def _put_lanes(refs, index, val):
    for t, r in enumerate(refs):
        r[index] = val[:, t * LANE:(t + 1) * LANE]


def _odd_pitch(n):
    tiles = -(-n // SUBLANE)
    return (tiles + 1 - tiles % 2) * SUBLANE


def _stage1(load_x, f_ref, a_refs, n2, rows, pitch, mm):
    def body(j, carry):
        _put_lanes(a_refs, (pl.ds(pl.multiple_of(j * pitch, SUBLANE), rows), slice(None)),
                   mm(f_ref[...], load_x(j)))
        return carry
    lax.fori_loop(0, n2, body, 0, unroll=8)


def _spectrum_rows(a_refs, k, k1p, n2, pitch):
    return jnp.concatenate([_lanes(a_refs, (pl.ds(k, n2, stride=pitch), slice(None))),
                            _lanes(a_refs, (pl.ds(k1p + k, n2, stride=pitch), slice(None)))], axis=0)


def _hyena_conv_kernel(*refs, nt, nh, n2, k1p):
    u_refs, gate_refs = refs[:nt], refs[nt:2 * nt]
    skip_ref, f1_ref, g1_ref, mf_ref, mi_ref, g_ref = refs[2 * nt:2 * nt + 6]
    y_outs = refs[2 * nt + 6:3 * nt + 6]
    a_refs, t_refs = refs[3 * nt + 6:4 * nt + 6], refs[4 * nt + 6:]
    rows = 2 * k1p
    pa, pt = _odd_pitch(rows), _odd_pitch(n2)
    group = lambda n1, pitch: pl.ds(pl.multiple_of(n1 * pitch, SUBLANE), n2)

    def regroup(n1, carry):
        for t in range(nt):
            t_refs[t][group(n1, pt), :] = u_refs[t][0, group(n1, n2), :]
        return carry
    lax.fori_loop(0, nh, regroup, 0, unroll=8)

    bdot = lambda f, x: _dot(f, x.astype(BF16))
    _stage1(lambda j: _lanes(t_refs, (pl.ds(j, nh, stride=pt), slice(None))),
            f1_ref, a_refs, n2, rows, pa, bdot)

    def stage2(k, carry):
        y = _dot(mf_ref[k], _spectrum_rows(a_refs, k, k1p, n2, pa).astype(BF16))
        yr, yi = y[:n2], y[n2:]
        gr, gi = g_ref[0, 0, k], g_ref[0, 1, k]
        z = jnp.concatenate([yr * gr - yi * gi, yr * gi + yi * gr], axis=0).astype(BF16)
        v = _dot(mi_ref[k], z)
        _put_lanes(a_refs, (pl.ds(k, n2, stride=pa), slice(None)), v[:n2])
        _put_lanes(a_refs, (pl.ds(k1p + k, n2, stride=pa), slice(None)), v[n2:])
        return carry
    lax.fori_loop(0, k1p, stage2, 0, unroll=8)

    def inverse1(j, carry):
        v = _lanes(a_refs, (pl.ds(pl.multiple_of(j * pa, SUBLANE), rows), slice(None))).astype(BF16)
        _put_lanes(t_refs, (pl.ds(j, nh, stride=pt), slice(None)), _dot(g1_ref[...], v))
        return carry
    lax.fori_loop(0, n2, inverse1, 0, unroll=8)

    def epilogue(n1, carry):
        sl = (0, group(n1, n2), slice(None))
        for t in range(nt):
            y_outs[t][sl] = gate_refs[t][sl] * (
                t_refs[t][group(n1, pt), :] + u_refs[t][sl] * skip_ref[:, t * LANE:(t + 1) * LANE])
        return carry
    lax.fori_loop(0, nh, epilogue, 0, unroll=8)


def _lane_tiles(x):
    if isinstance(x, (list, tuple)):
        return [(a, 0) for a in x]
    return [(x, t) for t in range(x.shape[2] // LANE)]


def _hyena_conv(u, gate, skip, g, order, plan, mats):
    ut, gt = _lane_tiles(u), _lane_tiles(gate)
    nt = len(ut)
    bsz, n_tok = ut[0][0].shape[:2]
    ch = nt * LANE
    nh, n2, k1p = plan.nh, plan.n2, plan.k1p
    rows = 2 * k1p
    tok = lambda t: pl.BlockSpec((1, n_tok, LANE), lambda b: (b, 0, t))
    const = lambda a: pl.BlockSpec(a.shape, lambda b: (0,) * a.ndim, pipeline_mode=pl.Buffered(1))
    return pl.pallas_call(
        functools.partial(_hyena_conv_kernel, nt=nt, nh=nh, n2=n2, k1p=k1p),
        grid=(bsz,),
        in_specs=[tok(t) for _, t in ut] + [tok(t) for _, t in gt]
        + [pl.BlockSpec((None, 1, ch), lambda b: (order, 0, 0)),
           const(mats['f1']), const(mats['g1']), const(mats['mf']), const(mats['mi']),
           pl.BlockSpec((1, 2, k1p, n2, ch), lambda b: (order, 0, 0, 0, 0),
                        pipeline_mode=pl.Buffered(1))],
        out_specs=[tok(0)] * nt,
        out_shape=[jax.ShapeDtypeStruct((bsz, n_tok, LANE), F32)] * nt,
        scratch_shapes=[pltpu.VMEM((n2 * _odd_pitch(rows), LANE), F32)] * nt
        + [pltpu.VMEM((nh * _odd_pitch(n2), LANE), F32)] * nt,
        compiler_params=_cparams(("parallel",)),
        name="hyena_conv",
    )(*[a for a, _ in ut], *[a for a, _ in gt], skip.reshape(skip.shape[0], 1, ch),
      mats['f1'], mats['g1'], mats['mf'], mats['mi'], g)


def _filter_spectrum_kernel(h_ref, ss_ref, f1_ref, mf_ref, g_out, a_ref, *, nh, n2, k1p):
    rows = 2 * k1p
    pa = _odd_pitch(rows)
    s = lax.rsqrt(ss_ref[0] + ss_ref[1] + EPS)
    for direction in range(2):
        _stage1(lambda j: h_ref[direction, pl.ds(j, nh, stride=n2), :], f1_ref, [a_ref], n2, rows, pa,
                _dot3)

        def stage2(k, carry):
            y = _dot3(mf_ref[k], _spectrum_rows([a_ref], k, k1p, n2, pa))
            if direction == 0:
                g_out[0, 0, k] = y[:n2] * s
                g_out[0, 1, k] = y[n2:] * s
            else:
                g_out[0, 0, k] = g_out[0, 0, k] + y[:n2] * s
                g_out[0, 1, k] = g_out[0, 1, k] - y[n2:] * s
            return carry
        lax.fori_loop(0, k1p, stage2, 0)


def _filter_spectrum(h, ss, plan, mats):
    groups, n_tok, ch = h.shape
    nh, n2, k1p = plan.nh, plan.n2, plan.k1p
    full = lambda a: pl.BlockSpec(a.shape, lambda o, ci: (0,) * a.ndim)
    return pl.pallas_call(
        functools.partial(_filter_spectrum_kernel, nh=nh, n2=n2, k1p=k1p),
        grid=(groups // 2, ch // LANE),
        in_specs=[pl.BlockSpec((2, n_tok, LANE), lambda o, ci: (o, 0, ci)),
                  pl.BlockSpec((2, 1, LANE), lambda o, ci: (o, 0, ci)),
                  full(mats['f1_32']), full(mats['mf32'])],
        out_specs=pl.BlockSpec((1, 2, k1p, n2, LANE), lambda o, ci: (o, 0, 0, 0, ci)),
        out_shape=jax.ShapeDtypeStruct((groups // 2, 2, k1p, n2, ch), F32),
        scratch_shapes=[pltpu.VMEM((n2 * _odd_pitch(2 * k1p), LANE), F32)],
        compiler_params=_cparams(("arbitrary", "arbitrary")),
        name="hyena_filter_spectrum",
    )(h, ss, mats['f1_32'], mats['mf32'])


def _filt_kernel(z_ref, w1_ref, b1_ref, w2_ref, b2_ref, fr_ref, w3_ref, dec_ref, h_out, ss_out, *, tl):
    i = pl.program_id(0)
    z = z_ref[...]
    h = jnp.sin(fr_ref[0:1, :] * (_dot3(z, w1_ref[...]) + b1_ref[...]))
    h = jnp.sin(fr_ref[1:2, :] * (_dot3(h, w2_ref[...]) + b2_ref[...]))
    h = _dot3(h, w3_ref[...])
    decay = jnp.exp(-z[:, 0:1] * jnp.abs(dec_ref[...]))
    row = i * tl + lax.broadcasted_iota(jnp.int32, (tl, 1), 0)

    @pl.when(i == 0)
    def _():
        ss_out[...] = jnp.zeros_like(ss_out)

    for g in range(2 * HY_ORDER):
        hg = h[:, g * HY_CH:(g + 1) * HY_CH] * decay
        if g % 2 == 1:
            hg = jnp.where(row > 0, hg, 0.0)
        h_out[g] = hg
        ss_out[g] += jnp.sum(hg * hg, axis=0, keepdims=True)


def _hyena_filter_spectrum(n_tok, i, p, plan, mats):
    t = jnp.linspace(0.0, 1.0, n_tok, dtype=F32)[:, None]
    w = 2.0 * math.pi * jnp.arange(n_tok, dtype=F32) / n_tok
    f = jnp.linspace(1e-4, HY_BANDS - 1, HY_BANDS, dtype=F32)
    fw = w[:, None] * f[None, :]
    feat = jnp.concatenate([t, jnp.cos(fw), -jnp.sin(fw)], axis=-1)
    n_feat = feat.shape[1]
    hid = p['hy_w1'].shape[2]
    feat = jnp.pad(feat, ((0, 0), (0, hid - n_feat)))
    w1 = jnp.pad(p['hy_w1'][i], ((0, hid - n_feat), (0, 0)))
    tl = _tile(n_tok, 512)
    groups = 2 * HY_ORDER
    full = lambda a: pl.BlockSpec(a.shape, lambda s: (0,) * a.ndim)
    wts = [w1, p['hy_b1'][i][None], p['hy_w2'][i], p['hy_b2'][i][None], p['hy_sin_freq'][i],
           p['hy_w3'][i], p['hy_decay'][i][None]]
    h, ss = pl.pallas_call(
        functools.partial(_filt_kernel, tl=tl),
        grid=(n_tok // tl,),
        in_specs=[pl.BlockSpec((tl, hid), lambda s: (s, 0))] + [full(a) for a in wts],
        out_specs=[pl.BlockSpec((groups, tl, HY_CH), lambda s: (0, s, 0)),
                   pl.BlockSpec((groups, 1, HY_CH), lambda s: (0, 0, 0))],
        out_shape=[jax.ShapeDtypeStruct((groups, n_tok, HY_CH), F32),
                   jax.ShapeDtypeStruct((groups, 1, HY_CH), F32)],
        compiler_params=_cparams(("arbitrary",)),
        name="hyena_filter_mlp",
    )(feat, *wts)
    return _filter_spectrum(h, ss, plan, mats)


def _hyena(v, x1, x2, skip, plan, mats, g):
    y = v
    for o, gate in enumerate((x1, x2)):
        y = _hyena_conv(y, gate, skip, g, o, plan, mats)
    return y


def _fft_mats(plan):
    return {'f1': jnp.asarray(plan.f1, BF16), 'f1_32': jnp.asarray(plan.f1, F32),
            'g1': jnp.asarray(plan.g1, BF16), 'mf': jnp.asarray(plan.mf, BF16),
            'mf32': jnp.asarray(plan.mf, F32), 'mi': jnp.asarray(plan.mi, BF16)}


def _cumsum_lanes(x):
    lane = lax.broadcasted_iota(jnp.int32, x.shape, 1)
    s = 1
    while s < x.shape[1]:
        x = x + jnp.where(lane >= s, pltpu.roll(x, s, 1), 0.0)
        s *= 2
    return x


def _topk_kernel(aff_ref, idx_out, gate_out, pos_ref, *, cap, chunk, nb):
    a = aff_ref[0]
    n_e, n_tok = a.shape
    bits = lax.bitcast_convert_type(a, jnp.int32)

    def count(mask):
        return jnp.sum(jnp.where(mask, 1.0, 0.0), axis=1, keepdims=True)

    def bisect(_, lohi):
        lo, hi = lohi
        mid = lo + ((hi - lo) >> 1)
        ok = count(bits >= mid) >= cap
        return jnp.where(ok, mid, lo), jnp.where(ok, hi, mid)

    lo0 = jnp.zeros((n_e, 1), jnp.int32)
    hi0 = jnp.full((n_e, 1), 0x7F800000, jnp.int32)
    thr, _ = lax.fori_loop(0, 31, bisect, (lo0, hi0))
    gt = bits > thr
    eqf = jnp.where(bits == thr, 1.0, 0.0)
    need = cap - count(gt)
    rank = _cumsum_lanes(eqf) - eqf
    self_ = jnp.where(gt, 1.0, jnp.where(rank < need, eqf, 0.0))
    pos_ref[...] = _cumsum_lanes(self_) * self_

    na = cap // nb
    tok = lax.broadcasted_iota(jnp.int32, (1, n_tok), 1)
    t_hi = (tok >> 6).astype(F32)
    t_lo = (tok & 63).astype(F32)
    row_a = lax.broadcasted_iota(jnp.int32, (na, 1), 0)
    row_b = lax.broadcasted_iota(jnp.int32, (nb, 1), 0)
    nt = (((1,), (1,)), ((), ()))

    def per_expert(e, carry):
        aff = aff_ref[0, pl.ds(e, 1), :]
        a_hi = aff.astype(BF16).astype(F32)
        a_mid = (aff - a_hi).astype(BF16).astype(F32)
        a_lo = aff - a_hi - a_mid
        slot = pos_ref[pl.ds(e, 1), :].astype(jnp.int32) - 1
        hi, lo = slot >> (nb.bit_length() - 1), slot & (nb - 1)
        acc = jnp.zeros((na, 5 * nb), F32)
        for c in range(n_tok // chunk):
            sl = slice(c * chunk, (c + 1) * chunk)
            one_a = jnp.where(hi[:, sl] == row_a, 1.0, 0.0).astype(BF16)
            in_b = lo[:, sl] == row_b
            pay = jnp.concatenate([jnp.where(in_b, v[:, sl], 0.0) for v in (t_hi, t_lo, a_hi, a_mid, a_lo)],
                                  axis=0).astype(BF16)
            acc = acc + lax.dot_general(one_a, pay, nt, preferred_element_type=F32)
        rows = pl.ds(pl.multiple_of(e * na, na), na) if isinstance(e, jax.Array) else slice(e * na, (e + 1) * na)
        idx_out[0, rows, :] = (acc[:, :nb] * 64.0 + acc[:, nb:2 * nb]).astype(jnp.int32)
        gate_out[0, rows, :] = acc[:, 2 * nb:3 * nb] + acc[:, 3 * nb:4 * nb] + acc[:, 4 * nb:]
        return carry

    if na % SUBLANE == 0:
        lax.fori_loop(0, n_e, per_expert, 0)
    else:
        for e in range(n_e):
            per_expert(e, 0)


def _route(aff_t, cap):
    bsz, n_e, n_tok = aff_t.shape
    nb = 16
    out = pl.BlockSpec((1, n_e * cap // nb, nb), lambda b: (b, 0, 0))
    idx, gates = pl.pallas_call(
        functools.partial(_topk_kernel, cap=cap, chunk=_tile(n_tok, 1024), nb=nb),
        grid=(bsz,), in_specs=[pl.BlockSpec((1, n_e, n_tok), lambda b: (b, 0, 0))],
        out_specs=[out, out],
        out_shape=[jax.ShapeDtypeStruct((bsz, n_e * cap // nb, nb), jnp.int32),
                   jax.ShapeDtypeStruct((bsz, n_e * cap // nb, nb), F32)],
        scratch_shapes=[pltpu.VMEM((n_e, n_tok), F32)],
        compiler_params=_cparams(("parallel",)),
        name="topk_route",
    )(aff_t)
    return idx.reshape(bsz, n_e, cap), gates.reshape(bsz, n_e, cap)


def kernel(x, c, ctx, c_ctx, mod_w, mod_b, norm1_g, w_in, q_a_g, w_q_b, kv_a_g, w_kv_b, conv_dw_w, conv_dw_b, conv_ln_g, conv_ln_b, hy_short_w, hy_short_b, hy_w1, hy_b1, hy_w2, hy_b2, hy_w3, hy_sin_freq, hy_decay, hy_skip, group_norm_g, w_out, norm2_g, router_w, w_gate, w_up, w_down, final_norm_g):
    p = dict(mod_w=mod_w, mod_b=mod_b, norm1_g=norm1_g, w_in=w_in, q_a_g=q_a_g, w_q_b=w_q_b,
             kv_a_g=kv_a_g, w_kv_b=w_kv_b, conv_dw_w=conv_dw_w, conv_dw_b=conv_dw_b,
             conv_ln_g=conv_ln_g, conv_ln_b=conv_ln_b, hy_short_w=hy_short_w,
             hy_short_b=hy_short_b, hy_w1=hy_w1, hy_b1=hy_b1, hy_w2=hy_w2, hy_b2=hy_b2,
             hy_w3=hy_w3, hy_sin_freq=hy_sin_freq, hy_decay=hy_decay, hy_skip=hy_skip,
             group_norm_g=group_norm_g, w_out=w_out, norm2_g=norm2_g, router_w=router_w,
             w_gate=w_gate, w_up=w_up, w_down=w_down)
    depth = mod_w.shape[0]
    bsz, n_lat, d = x.shape
    n_ctx = ctx.shape[1]

    rows = -(-(bsz + 1) // 8) * 8
    cc = jnp.concatenate([c, c_ctx[None, :], jnp.zeros((rows - bsz - 1, d), F32)], axis=0)
    mod_all = _modulation(cc, mod_w, mod_b)
    lat_row = lambda b: b
    ctx_row = lambda b: bsz

    tabs_l = _rope_tables(n_lat)
    tabs_c = _identity_tables(n_ctx)
    tm_l, tm_c = _tile(n_lat, 512), _tile(n_ctx, 256)
    tq_l, tq_c = _tile(n_lat, 256), _tile(n_ctx, 256)
    tl_l, tl_c = _tile(n_lat, 512), _tile(n_ctx, 256)

    xl, xc = x, ctx
    prev_l = prev_c = None
    for i in range(depth):
        last = i == depth - 1
        lw = _layer_weights(i, p)
        mod = mod_all[i].reshape(rows, 1, N_MOD * d)

        def side(xs, prev, row_of_b, tabs, tm, tq, tl, extra_keys, need_mix):
            xs, q, k, v, uc, uh = _inproj(xs, prev, mod, row_of_b, lw, tabs, tm)
            if not need_mix:
                return xs, (k, v), None
            att = _attention(q, [(k, v)] + extra_keys, tq)
            cn, hv, hx1, hx2 = _conv(uc, uh, lw, tl)
            plan = _fft_plan(xs.shape[1])
            mats = _fft_mats(plan)
            filt = _hyena_filter_spectrum(xs.shape[1], i, p, plan, mats)
            hy = _hyena(hv, hx1, hx2, p['hy_skip'][i], plan, mats, filt)
            x1, h2p, aff_t = _outproj(att, cn, hy, xs, mod, row_of_b, lw, tm)
            cap = CAPACITY_FACTOR * xs.shape[1] // N_EXPERTS
            idx, gates = _route(aff_t, cap)
            moe = _moe(h2p, idx, gates, lw)
            return x1, (k, v), moe

        xc, kv_c, moe_c = side(xc, prev_c, ctx_row, tabs_c, tm_c, tq_c, tl_c, [], not last)
        xl, _, moe_l = side(xl, prev_l, lat_row, tabs_l, tm_l, tq_l, tl_l, [kv_c], True)
        prev_l = (moe_l, mod)
        prev_c = None if moe_c is None else (moe_c, mod)

    mod = mod_all[depth - 1].reshape(rows, 1, N_MOD * d)
    return _final(xl, prev_l[0], mod, final_norm_g.reshape(1, d), tm_l)
```
